```python
import math
import numpy as np
import jax, jax.numpy as jnp
from jax import lax

D_MODEL = 1024
BATCH = 8
SEQ = 2048
DEPTH = 1
DEC_BATCH = 128
DEC_SEQ = 4
PAST_LEN = 8192
PAGE_SIZE = 128

N_HEADS = 8
N_KV = 2
HEAD_DIM = 64
Q_PER_KV = N_HEADS // N_KV
ATT_WIDTH = N_HEADS * HEAD_DIM
KV_WIDTH = N_KV * HEAD_DIM
CMP_BLOCK = 32
SEL_BLOCK = 64
CMP_PER_SEL = SEL_BLOCK // CMP_BLOCK
N_SEL = 16
WINDOW = 512
SCALE = HEAD_DIM ** -0.5
SSM_HEADS = 8
SSM_HEAD_DIM = 64
SSM_WIDTH = SSM_HEADS * SSM_HEAD_DIM
SSM_GROUPS = 2
SSM_STATE = 64
CONV_W = 4
SSD_CHUNK = 128
CONV_DIM = SSM_WIDTH + 2 * SSM_GROUPS * SSM_STATE
MIX_WIDTH = ATT_WIDTH + SSM_WIDTH
IN_SPLITS = (ATT_WIDTH, KV_WIDTH, KV_WIDTH, KV_WIDTH, KV_WIDTH, KV_WIDTH, KV_WIDTH, 3 * N_HEADS, SSM_WIDTH, CONV_DIM, SSM_HEADS)
IN_WIDTH = sum(IN_SPLITS)
N_EGROUPS = 4
E_PER_GROUP = 4
N_EXPERTS = N_EGROUPS * E_PER_GROUP
TOP_IN_GROUP = 2
EXPERT_FF = 256
EPS = 1e-6
NEG = -1e30
BIG = 1e9
TINY = 1e-30

kernel_name = 'hybrid_nsa_ssd_hmoe_adaln_step'


def rmsnorm(x, g):
    xf = x.astype(jnp.float32)
    y = xf * lax.rsqrt(jnp.mean(xf * xf, axis=-1, keepdims=True) + EPS)
    return (y * g.astype(jnp.float32)).astype(x.dtype)


def masked_softmax(s, mask):
    s = jnp.where(mask, s, NEG)
    e = jnp.where(mask, jnp.exp(s - jnp.max(s, axis=-1, keepdims=True)), 0.0)
    return e / jnp.maximum(jnp.sum(e, axis=-1, keepdims=True), TINY)


def dense_attend(q, k, v, mask):
    s = jnp.einsum('btgqd,bmgd->btgqm', q, k).astype(jnp.float32) * SCALE
    p = masked_softmax(s, mask[None, :, None, None, :])
    o = jnp.einsum('btgqm,bmgd->btgqd', p.astype(v.dtype), v)
    return o, p


def sel_attend(q, kv_g, g_valid, kv_cur, cur_mask):
    b, t, g, k, sb = kv_g.shape[:5]
    c = kv_cur.shape[1]
    s_g = jnp.einsum('btgqd,btgksd->btgqks', q, kv_g[..., 0, :]).reshape(b, t, g, Q_PER_KV, k * sb)
    s_c = jnp.einsum('btgqd,bcgd->btgqc', q, kv_cur[:, :, 0])
    s = jnp.concatenate([s_g, s_c], axis=-1).astype(jnp.float32) * SCALE
    m_g = jnp.broadcast_to(g_valid[:, :, :, None, :, None], (b, t, g, Q_PER_KV, k, sb)).reshape(b, t, g, Q_PER_KV, k * sb)
    m_c = jnp.broadcast_to(cur_mask[None, :, None, None, :], (b, t, g, Q_PER_KV, c))
    p = masked_softmax(s, jnp.concatenate([m_g, m_c], axis=-1)).astype(q.dtype)
    o_g = jnp.einsum('btgqks,btgksd->btgqd', p[..., :k * sb].reshape(b, t, g, Q_PER_KV, k, sb), kv_g[..., 1, :])
    o_c = jnp.einsum('btgqc,bcgd->btgqd', p[..., k * sb:], kv_cur[:, :, 1])
    return o_g + o_c


def compress(rows, w_pos):
    b, l, g, d = rows.shape
    return jnp.einsum('bcjgd,j->bcgd', rows.reshape(b, l // CMP_BLOCK, CMP_BLOCK, g, d), w_pos)


def select_blocks(p_cmp, n_blocks, cur):
    b, t, g = p_cmp.shape[:3]
    imp = p_cmp[..., :n_blocks * CMP_PER_SEL].sum(axis=3).reshape(b, t, g, n_blocks, CMP_PER_SEL).sum(axis=-1)
    j = jnp.arange(n_blocks)[None, :]
    allowed = (j < cur[:, None])[None, :, None, :]
    forced = ((j == 0) | (j == cur[:, None] - 1))[None, :, None, :]
    score = jnp.where(allowed, jnp.where(forced, BIG, imp), NEG)
    vals, idx = lax.top_k(score, min(N_SEL - 1, n_blocks))
    return idx, vals > 0.5 * NEG


def in_proj(h, w_in, g_q, g_k_sel, g_k_win):
    b, l, _ = h.shape
    u = h @ w_in
    offs = np.cumsum(IN_SPLITS)[:-1].tolist()
    q, kc, vc, ks, vs, kw, vw, gt, z, xbc, dt = jnp.split(u, offs, axis=-1)
    heads = lambda a: a.reshape(b, l, N_KV, HEAD_DIM)
    q = rmsnorm(q.reshape(b, l, N_KV, Q_PER_KV, HEAD_DIM), g_q)
    kv_sel = jnp.stack([rmsnorm(heads(ks), g_k_sel), heads(vs)], axis=2)
    kv_win = jnp.stack([rmsnorm(heads(kw), g_k_win), heads(vw)], axis=2)
    gates = jax.nn.sigmoid(gt.astype(jnp.float32)).reshape(b, l, 3, N_KV, Q_PER_KV)
    return q, heads(kc), heads(vc), kv_sel, kv_win, gates, z, xbc, dt


def nsa_prompt(q, kc_raw, vc, kv_sel, kv_win, w_pos_k, w_pos_v, g_k_cmp):
    b, s = q.shape[:2]
    qpos = jnp.arange(s)
    kcmp = rmsnorm(compress(kc_raw, w_pos_k), g_k_cmp)
    vcmp = compress(vc, w_pos_v)
    ends = jnp.arange(s // CMP_BLOCK) * CMP_BLOCK + CMP_BLOCK - 1
    o_cmp, p_cmp = dense_attend(q, kcmp, vcmp, ends[None, :] <= qpos[:, None])
    nb = s // SEL_BLOCK
    idx, g_valid = select_blocks(p_cmp, nb, qpos // SEL_BLOCK)
    kv_blocks = kv_sel.reshape(b, nb, SEL_BLOCK, 2, N_KV, HEAD_DIM)
    kv_win_pad = jnp.pad(kv_win, ((0, 0), (WINDOW, 0), (0, 0), (0, 0), (0, 0)))
    bi = jnp.arange(b)[:, None, None, None]
    gi = jnp.arange(N_KV)[None, None, :, None]
    causal = jnp.tril(jnp.ones((SEL_BLOCK, SEL_BLOCK), bool))
    rel = jnp.arange(SEL_BLOCK)[:, None] - (jnp.arange(WINDOW + SEL_BLOCK)[None, :] - WINDOW)
    to_blocks = lambda a: jnp.moveaxis(a.reshape(b, nb, SEL_BLOCK, *a.shape[2:]), 1, 0)

    def body(inp):
        n, qb, idxb, vb = inp
        kv_g = kv_blocks[bi, idxb, :, :, gi]
        kv_cur = lax.dynamic_index_in_dim(kv_blocks, n, axis=1, keepdims=False)
        o_s = sel_attend(qb, kv_g, vb, kv_cur, causal)
        start = n * SEL_BLOCK
        slab = lax.dynamic_slice_in_dim(kv_win_pad, start, WINDOW + SEL_BLOCK, axis=1)
        pos = start - WINDOW + jnp.arange(WINDOW + SEL_BLOCK)
        wmask = (rel >= 0) & (rel < WINDOW) & (pos[None, :] >= 0)
        o_w, _ = dense_attend(qb, slab[:, :, 0], slab[:, :, 1], wmask)
        return o_s, o_w

    o_sel, o_win = lax.map(body, (jnp.arange(nb), to_blocks(q), to_blocks(idx), to_blocks(g_valid)))
    back = lambda a: jnp.moveaxis(a, 0, 1).reshape(b, s, *a.shape[3:])
    return o_cmp, back(o_sel), back(o_win)


def nsa_sample(q, kv_sel, kv_win, cache_cmp, cache_sel, cache_win, page_table, layer, w_pos_k, w_pos_v, g_k_cmp):
    bd, t = q.shape[:2]
    qpos = PAST_LEN + jnp.arange(t)
    rows = cache_cmp[layer, page_table].reshape(bd, PAST_LEN, 2, N_KV, HEAD_DIM)
    kcmp = rmsnorm(compress(rows[:, :, 0], w_pos_k), g_k_cmp)
    vcmp = compress(rows[:, :, 1], w_pos_v)
    ends = jnp.arange(PAST_LEN // CMP_BLOCK) * CMP_BLOCK + CMP_BLOCK - 1
    o_cmp, p_cmp = dense_attend(q, kcmp, vcmp, ends[None, :] <= qpos[:, None])
    nb = PAST_LEN // SEL_BLOCK
    idx, g_valid = select_blocks(p_cmp, nb, qpos // SEL_BLOCK)
    sub = PAGE_SIZE // SEL_BLOCK
    pool = cache_sel.reshape(cache_sel.shape[0], -1, SEL_BLOCK, 2, N_KV, HEAD_DIM)
    bi = jnp.arange(bd)[:, None, None, None]
    gi = jnp.arange(N_KV)[None, None, :, None]
    phys = page_table[bi, idx // sub] * sub + idx % sub
    kv_g = pool[layer, phys, :, :, gi]
    o_sel = sel_attend(q, kv_g, g_valid, kv_sel, jnp.tril(jnp.ones((t, t), bool)))
    w_buf = cache_win.shape[2]
    kv_w = jnp.concatenate([cache_win[layer].astype(kv_win.dtype), kv_win], axis=1)
    pos = PAST_LEN - w_buf + jnp.arange(w_buf + t)
    rel = qpos[:, None] - pos[None, :]
    o_win, _ = dense_attend(q, kv_w[:, :, 0], kv_w[:, :, 1], (rel >= 0) & (rel < WINDOW))
    return o_cmp, o_sel, o_win, kv_w[:, -min(WINDOW, w_buf + t):]


def combine_branches(gates, o_cmp, o_sel, o_win):
    b, t = o_cmp.shape[:2]
    g = gates.astype(o_cmp.dtype)[..., None]
    o = g[:, :, 0] * o_cmp + g[:, :, 1] * o_sel + g[:, :, 2] * o_win
    return o.reshape(b, t, ATT_WIDTH)


def ssd_scan(x, dt, a, bm, cm, h0):
    b, l = x.shape[:2]
    cl = min(SSD_CHUNK, l)
    nc = l // cl
    rep = SSM_HEADS // SSM_GROUPS
    bh = jnp.repeat(bm, rep, axis=2)
    ch = jnp.repeat(cm, rep, axis=2)
    chunk = lambda v: jnp.moveaxis(v.reshape(b, nc, cl, *v.shape[2:]), 1, 0)
    causal = jnp.tril(jnp.ones((cl, cl), bool))[None, :, :, None]

    def step(h, inp):
        xc, dtc, bc, cc = inp
        acs = jnp.cumsum(dtc * a, axis=1)
        seg = acs[:, :, None, :] - acs[:, None, :, :]
        decay = jnp.where(causal, jnp.exp(jnp.where(causal, seg, 0.0)), 0.0)
        cb = jnp.einsum('blhn,bshn->blsh', cc, bc)
        y_diag = jnp.einsum('blsh,bsh,bshp->blhp', cb * decay, dtc, xc)
        y_off = jnp.einsum('blhn,bhpn,blh->blhp', cc, h, jnp.exp(acs))
        w_end = jnp.exp(acs[:, -1:, :] - acs) * dtc
        h_new = jnp.exp(acs[:, -1])[:, :, None, None] * h + jnp.einsum('bsh,bshn,bshp->bhpn', w_end, bc, xc)
        return h_new, y_diag + y_off

    h_t, ys = lax.scan(step, h0, (chunk(x), chunk(dt), chunk(bh), chunk(ch)))
    return jnp.moveaxis(ys, 0, 1).reshape(b, l, SSM_HEADS, SSM_HEAD_DIM), h_t


def ssd_mixer(z, xbc, dt_raw, conv_prev, h0, conv_w, conv_b, dt_bias, a_log, d_skip, g_out):
    b, l, _ = xbc.shape
    xpad = jnp.concatenate([conv_prev.astype(xbc.dtype), xbc], axis=1)
    new_conv = xpad[:, -(CONV_W - 1):]
    xc = lax.conv_general_dilated(xpad, conv_w[:, None, :].astype(xbc.dtype), (1,), 'VALID',
                                  dimension_numbers=('NWC', 'WIO', 'NWC'), feature_group_count=CONV_DIM)
    xc = jax.nn.silu(xc + conv_b)
    gn = SSM_GROUPS * SSM_STATE
    xs = xc[..., :SSM_WIDTH].reshape(b, l, SSM_HEADS, SSM_HEAD_DIM).astype(jnp.float32)
    bm = xc[..., SSM_WIDTH:SSM_WIDTH + gn].reshape(b, l, SSM_GROUPS, SSM_STATE).astype(jnp.float32)
    cm = xc[..., SSM_WIDTH + gn:].reshape(b, l, SSM_GROUPS, SSM_STATE).astype(jnp.float32)
    dt = jax.nn.softplus((dt_raw + dt_bias).astype(jnp.float32))
    a = -jnp.exp(a_log.astype(jnp.float32))
    y, h_t = ssd_scan(xs, dt, a, bm, cm, h0)
    y = (y + d_skip.astype(jnp.float32)[:, None] * xs).reshape(b, l, SSM_WIDTH).astype(z.dtype)
    return rmsnorm(y * jax.nn.silu(z), g_out), new_conv, h_t


def hier_moe(h, w_rg, b_rg, w_re, b_re, w_gate, w_up, w_down):
    b, l, _ = h.shape
    lg = (h @ w_rg + b_rg).astype(jnp.float32)
    _, gsel = lax.top_k(lg, 1)
    p_top = jnp.take_along_axis(jax.nn.softmax(lg, axis=-1), gsel, axis=-1)
    le = (jnp.einsum('bld,gde->blge', h, w_re) + b_re).astype(jnp.float32)
    le = jnp.take_along_axis(le, jnp.broadcast_to(gsel[..., None], (b, l, 1, E_PER_GROUP)), axis=2)[:, :, 0]
    v2, i2 = lax.top_k(le, TOP_IN_GROUP)
    weight = p_top * jax.nn.softmax(v2, axis=-1)
    eid = gsel * E_PER_GROUP + i2
    combine = jnp.sum(jax.nn.one_hot(eid, N_EXPERTS, dtype=jnp.float32) * weight[..., None], axis=-2).astype(h.dtype)
    y = jnp.zeros_like(h)
    for e in range(N_EXPERTS):
        he = jax.nn.silu(h @ w_gate[e]) * (h @ w_up[e])
        y = y + combine[..., e:e + 1] * (he @ w_down[e])
    return y


def adaln_mods(c, w_ada, b_ada):
    m = (jax.nn.silu(c) @ w_ada + b_ada)[:, None, :]
    return jnp.split(m, 6, axis=-1)


def finish(x, att, y_ssm, mods, g_att_out, w_out, g_norm2, w_rg, b_rg, w_re, b_re, w_gate, w_up, w_down):
    mixed = jnp.concatenate([rmsnorm(att, g_att_out), y_ssm], axis=-1) @ w_out
    x = x + mods[2] * mixed
    h = rmsnorm(x, g_norm2) * (1 + mods[4]) + mods[3]
    return x + mods[5] * hier_moe(h, w_rg, b_rg, w_re, b_re, w_gate, w_up, w_down)


def setup_inputs(seed: int = 0) -> dict:
    key = jax.random.key(seed)
    k = jax.random.split(key, 40)
    nrm = lambda kk, shape, s=1.0: jax.random.normal(kk, shape, jnp.float32) * s
    n_pages = PAST_LEN // PAGE_SIZE
    n_pool = (5 * DEC_BATCH * n_pages + 3) // 4
    w_buf = min(WINDOW, PAST_LEN)
    page_table = jax.random.permutation(k[0], n_pool)[:DEC_BATCH * n_pages].reshape(DEC_BATCH, n_pages).astype(jnp.int32)
    dt0 = jnp.exp(jax.random.uniform(k[1], (DEPTH, SSM_HEADS), jnp.float32, math.log(1e-3), math.log(1e-1)))
    dt_bias = dt0 + jnp.log(-jnp.expm1(-dt0))
    a_log = jnp.log(jax.random.uniform(k[2], (DEPTH, SSM_HEADS), jnp.float32, 1.0, 16.0))
    gain = lambda kk, n: 1.0 + nrm(kk, (DEPTH, n), 0.02)
    return {
        'x_prompt': nrm(k[3], (BATCH, SEQ, D_MODEL)),
        'x_sample': nrm(k[4], (DEC_BATCH, DEC_SEQ, D_MODEL)),
        'cache_cmp': nrm(k[5], (DEPTH, n_pool, PAGE_SIZE, 2, N_KV, HEAD_DIM)),
        'cache_sel': nrm(k[6], (DEPTH, n_pool, PAGE_SIZE, 2, N_KV, HEAD_DIM)),
        'cache_win': nrm(k[7], (DEPTH, DEC_BATCH, w_buf, 2, N_KV, HEAD_DIM)),
        'state_ssm': nrm(k[8], (DEPTH, DEC_BATCH, SSM_HEADS, SSM_HEAD_DIM, SSM_STATE), 0.5),
        'state_conv': nrm(k[9], (DEPTH, DEC_BATCH, CONV_W - 1, CONV_DIM)),
        'page_table': page_table,
        'c_prompt': nrm(k[10], (BATCH, D_MODEL)),
        'c_sample': nrm(k[11], (DEC_BATCH, D_MODEL)),
        'g_norm1': gain(k[12], D_MODEL),
        'g_norm2': gain(k[13], D_MODEL),
        'w_ada': nrm(k[14], (DEPTH, D_MODEL, 6 * D_MODEL), 0.3 * D_MODEL ** -0.5),
        'b_ada': nrm(k[15], (DEPTH, 6 * D_MODEL), 0.02),
        'w_in': nrm(k[16], (DEPTH, D_MODEL, IN_WIDTH), D_MODEL ** -0.5),
        'g_q': gain(k[17], HEAD_DIM),
        'g_k_cmp': gain(k[18], HEAD_DIM),
        'g_k_sel': gain(k[19], HEAD_DIM),
        'g_k_win': gain(k[20], HEAD_DIM),
        'w_pos_k': (1.0 + nrm(k[21], (DEPTH, CMP_BLOCK), 0.1)) / CMP_BLOCK,
        'w_pos_v': (1.0 + nrm(k[22], (DEPTH, CMP_BLOCK), 0.1)) / CMP_BLOCK,
        'conv_w': nrm(k[23], (DEPTH, CONV_W, CONV_DIM), CONV_W ** -0.5),
        'conv_b': nrm(k[24], (DEPTH, CONV_DIM), 0.02),
        'dt_bias': dt_bias,
        'a_log': a_log,
        'd_skip': 1.0 + nrm(k[25], (DEPTH, SSM_HEADS), 0.1),
        'g_att_out': gain(k[26], ATT_WIDTH),
        'g_ssm_out': gain(k[27], SSM_WIDTH),
        'w_out': nrm(k[28], (DEPTH, MIX_WIDTH, D_MODEL), MIX_WIDTH ** -0.5),
        'w_rg': nrm(k[29], (DEPTH, D_MODEL, N_EGROUPS), D_MODEL ** -0.5),
        'b_rg': nrm(k[30], (DEPTH, N_EGROUPS), 0.01),
        'w_re': nrm(k[31], (DEPTH, N_EGROUPS, D_MODEL, E_PER_GROUP), D_MODEL ** -0.5),
        'b_re': nrm(k[32], (DEPTH, N_EGROUPS, E_PER_GROUP), 0.01),
        'w_gate': nrm(k[33], (DEPTH, N_EXPERTS, D_MODEL, EXPERT_FF), D_MODEL ** -0.5),
        'w_up': nrm(k[34], (DEPTH, N_EXPERTS, D_MODEL, EXPERT_FF), D_MODEL ** -0.5),
        'w_down': nrm(k[35], (DEPTH, N_EXPERTS, EXPERT_FF, D_MODEL), EXPERT_FF ** -0.5),
    }


def reference(x_prompt, x_sample, cache_cmp, cache_sel, cache_win, state_ssm, state_conv, page_table, c_prompt, c_sample,
              g_norm1, g_norm2, w_ada, b_ada, w_in, g_q, g_k_cmp, g_k_sel, g_k_win, w_pos_k, w_pos_v, conv_w, conv_b,
              dt_bias, a_log, d_skip, g_att_out, g_ssm_out, w_out, w_rg, b_rg, w_re, b_re, w_gate, w_up, w_down):
    xp, xs = x_prompt, x_sample
    cmp_p, cmp_s, sel_p, sel_s, win_p, win_s, ssm_p, ssm_s, conv_p, conv_s = [[] for _ in range(10)]
    for l in range(DEPTH):
        mods = adaln_mods(c_prompt, w_ada[l], b_ada[l])
        h = rmsnorm(xp, g_norm1[l]) * (1 + mods[1]) + mods[0]
        q, kc, vc, kv_s, kv_w, gates, z, xbc, dt = in_proj(h, w_in[l], g_q[l], g_k_sel[l], g_k_win[l])
        o_c, o_s, o_w = nsa_prompt(q, kc, vc, kv_s, kv_w, w_pos_k[l], w_pos_v[l], g_k_cmp[l])
        att = combine_branches(gates, o_c, o_s, o_w)
        b = xp.shape[0]
        y_ssm, cv, hs = ssd_mixer(z, xbc, dt, jnp.zeros((b, CONV_W - 1, CONV_DIM), xbc.dtype),
                                  jnp.zeros((b, SSM_HEADS, SSM_HEAD_DIM, SSM_STATE), jnp.float32),
                                  conv_w[l], conv_b[l], dt_bias[l], a_log[l], d_skip[l], g_ssm_out[l])
        xp = finish(xp, att, y_ssm, mods, g_att_out[l], w_out[l], g_norm2[l], w_rg[l], b_rg[l], w_re[l], b_re[l],
                    w_gate[l], w_up[l], w_down[l])
        cmp_p.append(jnp.stack([kc, vc], axis=2))
        sel_p.append(kv_s)
        win_p.append(kv_w[:, -min(WINDOW, kv_w.shape[1]):])
        ssm_p.append(hs)
        conv_p.append(cv)
        mods = adaln_mods(c_sample, w_ada[l], b_ada[l])
        h = rmsnorm(xs, g_norm1[l]) * (1 + mods[1]) + mods[0]
        q, kc, vc, kv_s, kv_w, gates, z, xbc, dt = in_proj(h, w_in[l], g_q[l], g_k_sel[l], g_k_win[l])
        o_c, o_s, o_w, wbuf = nsa_sample(q, kv_s, kv_w, cache_cmp, cache_sel, cache_win, page_table, l,
                                         w_pos_k[l], w_pos_v[l], g_k_cmp[l])
        att = combine_branches(gates, o_c, o_s, o_w)
        y_ssm, cv, hs = ssd_mixer(z, xbc, dt, state_conv[l], state_ssm[l].astype(jnp.float32),
                                  conv_w[l], conv_b[l], dt_bias[l], a_log[l], d_skip[l], g_ssm_out[l])
        xs = finish(xs, att, y_ssm, mods, g_att_out[l], w_out[l], g_norm2[l], w_rg[l], b_rg[l], w_re[l], b_re[l],
                    w_gate[l], w_up[l], w_down[l])
        cmp_s.append(jnp.stack([kc, vc], axis=2))
        sel_s.append(kv_s)
        win_s.append(wbuf)
        ssm_s.append(hs.astype(state_ssm.dtype))
        conv_s.append(cv)
    return (xp, xs, jnp.stack(cmp_p), jnp.stack(cmp_s), jnp.stack(sel_p), jnp.stack(sel_s), jnp.stack(win_p),
            jnp.stack(win_s), jnp.stack(ssm_p), jnp.stack(ssm_s), jnp.stack(conv_p), jnp.stack(conv_s))
```

```python
import functools

import numpy as np
import jax
import jax.numpy as jnp
from jax import lax
from jax.experimental import pallas as pl
from jax.experimental.pallas import tpu as pltpu

F32 = jnp.float32
BF16 = jnp.bfloat16

D_MODEL = 1024
PAGE_SIZE = 128
N_HEADS = 8
N_KV = 2
HEAD_DIM = 64
Q_PER_KV = N_HEADS // N_KV
ATT_WIDTH = N_HEADS * HEAD_DIM
KV_WIDTH = N_KV * HEAD_DIM
CMP_BLOCK = 32
SEL_BLOCK = 64
N_SEL = 16
WINDOW = 512
SCALE = HEAD_DIM ** -0.5
SSM_HEADS = 8
SSM_HEAD_DIM = 64
SSM_WIDTH = SSM_HEADS * SSM_HEAD_DIM
SSM_GROUPS = 2
SSM_STATE = 64
CONV_W = 4
SSD_CHUNK = 128
CONV_DIM = SSM_WIDTH + 2 * SSM_GROUPS * SSM_STATE
N_EGROUPS = 4
E_PER_GROUP = 4
N_EXPERTS = N_EGROUPS * E_PER_GROUP
EXPERT_FF = 256
EPS = 1e-6
NEG = -1e30
BIG = 1e9
TINY = 1e-30

LANES = 128
SUBLANES = 8
VMEM_LIMIT = 48 * 1024 * 1024

PK_Q = 0
PK_CMP = 512
PK_SEL = 768
PK_WIN = 1024
PK_Z = 1280
PK_XBC = 1792
PK_GD = 2560
PK_WIDTH = 2688
GD_GATE0 = 8
RT_E0 = 4


def _dot(a, b):
    return jnp.dot(a, b, preferred_element_type=F32)


def _dot_nt(a, b):
    return lax.dot_general(a, b, (((1,), (1,)), ((), ())), preferred_element_type=F32)


def _dot_tn(a, b):
    return lax.dot_general(a, b, (((0,), (0,)), ((), ())), preferred_element_type=F32)


def _split2(x):
    hi = x.astype(BF16)
    lo = (x - hi.astype(F32)).astype(BF16)
    return hi, lo


def _split3(x):
    hi = x.astype(BF16)
    r = x - hi.astype(F32)
    mid = r.astype(BF16)
    lo = (r - mid.astype(F32)).astype(BF16)
    return hi, mid, lo


def _dot_exact_lhs(a_bf16, x):
    return sum(_dot(a_bf16, p) for p in _split3(x))


def _dot_exact_rhs(x, b_bf16):
    return sum(_dot(p, b_bf16) for p in _split3(x))


def _sigmoid(x):
    return 1.0 / (1.0 + jnp.exp(-x))


def _silu(x):
    return x * _sigmoid(x)


def _softplus(x):
    return jnp.maximum(x, 0.0) + jnp.log1p(jnp.exp(-jnp.abs(x)))


def _rms(x):
    return x * lax.rsqrt(jnp.mean(x * x, axis=-1, keepdims=True) + EPS)


def _head_rms(x, bd):
    hi, lo = _split2(x * x)
    ms = (_dot(hi, bd) + _dot(lo, bd)) * (1.0 / HEAD_DIM)
    return x * lax.rsqrt(ms + EPS)


def _topk_mask(score, nblk, k):
    lane = lax.broadcasted_iota(jnp.int32, score.shape, 1)
    rank = jnp.zeros(score.shape, F32)
    for i in range(nblk):
        si = score[:, i:i + 1]
        beats = jnp.where(si > score, 1.0, jnp.where(si == score, jnp.where(lane > i, 1.0, 0.0), 0.0))
        rank = rank + beats
    return jnp.where(rank < k, 1.0, 0.0)


def _params(*sem):
    return pltpu.CompilerParams(dimension_semantics=sem, vmem_limit_bytes=VMEM_LIMIT)


def _ada_body(c_ref, w_ref, b_ref, o_ref):
    c = _silu(c_ref[...]).astype(BF16)
    o_ref[...] = _dot(c, w_ref[...].astype(BF16)) + b_ref[...]


def _ada(c_all, w_ada, b_ada):
    n, d = c_all.shape
    width = w_ada.shape[1]
    tn = 1536
    return pl.pallas_call(
        _ada_body,
        grid=(width // tn,),
        in_specs=[pl.BlockSpec((n, d), lambda j: (0, 0)),
                  pl.BlockSpec((d, tn), lambda j: (0, j)),
                  pl.BlockSpec((1, tn), lambda j: (0, j))],
        out_specs=pl.BlockSpec((n, tn), lambda j: (0, j)),
        out_shape=jax.ShapeDtypeStruct((n, width), F32),
        compiler_params=_params("arbitrary"),
        name="ada",
    )(c_all, w_ada, b_ada.reshape(1, width))


def _inproj_body(with_cmp, x_ref, sh_ref, sc_ref, g1_ref, w_ref, gq_ref, gks_ref, gkw_ref, bd_ref, *rest):
    if with_cmp:
        wkv_ref, gkc_ref, q_ref, cmp_ref, sel_ref, win_ref, z_ref, xbc_ref, gd_ref, kc_ref = rest
    else:
        q_ref, cmp_ref, sel_ref, win_ref, z_ref, xbc_ref, gd_ref = rest
    h = _rms(x_ref[...]) * g1_ref[...]
    hb = (h * (1.0 + sc_ref[0]) + sh_ref[0]).astype(BF16)
    bd = bd_ref[...]
    bd_kv = bd_ref[0:KV_WIDTH, 0:KV_WIDTH]

    q = _dot(hb, w_ref[:, PK_Q:PK_CMP])
    q_ref[...] = (_head_rms(q, bd) * gq_ref[...]).astype(BF16)

    kv = _dot(hb, w_ref[:, PK_CMP:PK_SEL])
    cmp_ref[...] = kv
    if with_cmp:
        tm = kv.shape[0]
        pooled = (kv.reshape(tm // CMP_BLOCK, CMP_BLOCK, 2 * KV_WIDTH) * wkv_ref[...][None]).sum(axis=1)
        kc = pooled[:, :KV_WIDTH]
        kc_ref[:, 0:KV_WIDTH] = _head_rms(kc, bd_kv) * gkc_ref[...]
        kc_ref[:, KV_WIDTH:] = pooled[:, KV_WIDTH:]

    kv = _dot(hb, w_ref[:, PK_SEL:PK_WIN])
    sel_ref[:, 0:KV_WIDTH] = _head_rms(kv[:, :KV_WIDTH], bd_kv) * gks_ref[...]
    sel_ref[:, KV_WIDTH:] = kv[:, KV_WIDTH:]

    kv = _dot(hb, w_ref[:, PK_WIN:PK_Z])
    win_ref[:, 0:KV_WIDTH] = _head_rms(kv[:, :KV_WIDTH], bd_kv) * gkw_ref[...]
    win_ref[:, KV_WIDTH:] = kv[:, KV_WIDTH:]

    z_ref[...] = _dot(hb, w_ref[:, PK_Z:PK_XBC])
    xbc_ref[...] = _dot(hb, w_ref[:, PK_XBC:PK_GD])
    gd_ref[...] = _dot(hb, w_ref[:, PK_GD:PK_WIDTH])


def _in_proj(x2d, shift3, scale3, mod_index, tm, wts, with_cmp):
    t, d = x2d.shape
    rmod = shift3.shape[1]
    full = lambda a: pl.BlockSpec(a.shape, lambda i: (0,) * a.ndim)
    row = lambda w: pl.BlockSpec((tm, w), lambda i: (i, 0))
    mod = pl.BlockSpec((1, rmod, d), mod_index)
    ins = [x2d, shift3, scale3, wts["g1"], wts["w_in"], wts["gq"], wts["gks"], wts["gkw"], wts["bd"]]
    in_specs = [row(d), mod, mod] + [full(a) for a in ins[3:]]
    widths = [ATT_WIDTH, 2 * KV_WIDTH, 2 * KV_WIDTH, 2 * KV_WIDTH, SSM_WIDTH, CONV_DIM, LANES]
    dtypes = [BF16, F32, F32, F32, F32, F32, F32]
    out_shape = [jax.ShapeDtypeStruct((t, w), dt) for w, dt in zip(widths, dtypes)]
    out_specs = [row(w) for w in widths]
    if with_cmp:
        ins += [wts["wkv"], wts["gkc"]]
        in_specs += [full(wts["wkv"]), full(wts["gkc"])]
        out_shape.append(jax.ShapeDtypeStruct((t // CMP_BLOCK, 2 * KV_WIDTH), F32))
        out_specs.append(pl.BlockSpec((tm // CMP_BLOCK, 2 * KV_WIDTH), lambda i: (i, 0)))
    return pl.pallas_call(
        functools.partial(_inproj_body, with_cmp),
        grid=(t // tm,),
        in_specs=in_specs,
        out_specs=out_specs,
        out_shape=out_shape,
        compiler_params=_params("arbitrary"),
        name="in_proj",
    )(*ins)


def _attn_prompt_body(q_ref, sel_ref, win_ref, kc_ref, gd_ref, pair_ref, o_ref, msk_ref, *, tq, seq):
    i = pl.program_id(1)
    nch = seq // tq
    ncb = seq // CMP_BLOCK
    nblk = seq // SEL_BLOCK
    bpc = tq // SEL_BLOCK
    rows = Q_PER_KV * tq
    q = q_ref[...]
    t_col = i * tq + lax.broadcasted_iota(jnp.int32, (tq, 1), 0)
    cur_col = t_col // SEL_BLOCK
    lane_k = lax.broadcasted_iota(jnp.int32, (tq, tq), 1)
    lane_b = lax.broadcasted_iota(jnp.int32, (tq, LANES), 1)
    gates = _sigmoid(gd_ref[...])

    def flash(kv_ref, g, c_lo, c_hi, mask_fn):
        def step(c, carry):
            m, l, acc = carry
            off = pl.multiple_of(c * tq, tq)
            kblk = kv_ref[pl.ds(off, tq), g * HEAD_DIM:(g + 1) * HEAD_DIM].astype(BF16)
            vblk = kv_ref[pl.ds(off, tq), KV_WIDTH + g * HEAD_DIM:KV_WIDTH + (g + 1) * HEAD_DIM].astype(BF16)
            mask = mask_fn(c)[None]
            s = jnp.where(mask, _dot_nt(qg, kblk).reshape(Q_PER_KV, tq, tq), NEG)
            m_new = jnp.maximum(m, s.max(axis=-1, keepdims=True))
            alpha = jnp.exp(m - m_new)
            e = jnp.where(mask, jnp.exp(s - m_new), 0.0)
            l = alpha * l + e.sum(axis=-1, keepdims=True)
            pv = _dot(e.reshape(rows, tq).astype(BF16), vblk).reshape(Q_PER_KV, tq, HEAD_DIM)
            return m_new, l, alpha * acc + pv

        init = (jnp.full((Q_PER_KV, tq, 1), NEG, F32), jnp.zeros((Q_PER_KV, tq, 1), F32),
                jnp.zeros((Q_PER_KV, tq, HEAD_DIM), F32))
        _, l, acc = lax.fori_loop(c_lo, c_hi, step, init)
        return acc / jnp.maximum(l, TINY)

    for g in range(N_KV):
        qg = jnp.concatenate(
            [q[:, (g * Q_PER_KV + h) * HEAD_DIM:(g * Q_PER_KV + h + 1) * HEAD_DIM] for h in range(Q_PER_KV)], axis=0)

        kcg = kc_ref[:, g * HEAD_DIM:(g + 1) * HEAD_DIM].astype(BF16)
        vcg = kc_ref[:, KV_WIDTH + g * HEAD_DIM:KV_WIDTH + (g + 1) * HEAD_DIM].astype(BF16)
        cblk = lax.broadcasted_iota(jnp.int32, (tq, ncb), 1)
        cmask = ((cblk * CMP_BLOCK + CMP_BLOCK - 1) <= t_col)[None]
        s = jnp.where(cmask, _dot_nt(qg, kcg).reshape(Q_PER_KV, tq, ncb), NEG)
        e = jnp.where(cmask, jnp.exp(s - s.max(axis=-1, keepdims=True)), 0.0)
        p = e / jnp.maximum(e.sum(axis=-1, keepdims=True), TINY)
        o_cmp = _dot(p.reshape(rows, ncb).astype(BF16), vcg).reshape(Q_PER_KV, tq, HEAD_DIM)

        imp = _dot_exact_rhs(p.sum(axis=0), pair_ref[...])
        allowed = lane_b < cur_col
        forced = (lane_b == 0) | (lane_b == cur_col - 1)
        score = jnp.where(allowed, jnp.where(forced, BIG, imp), NEG)
        selm = jnp.where(allowed, _topk_mask(score, nblk, min(N_SEL - 1, nblk)), 0.0)
        for c in range(nch):
            mc = selm[:, c * bpc:c * bpc + 1]
            for j in range(1, bpc):
                mc = jnp.where(lane_k < j * SEL_BLOCK, mc, selm[:, c * bpc + j:c * bpc + j + 1])
            msk_ref[c] = mc

        def sel_mask(c):
            kp = c * tq + lane_k
            cur_blk = ((kp // SEL_BLOCK) == cur_col) & (kp <= t_col)
            return (msk_ref[c] > 0.5) | cur_blk

        def win_mask(c):
            rel = t_col - (c * tq + lane_k)
            return (rel >= 0) & (rel < WINDOW)

        o_sel = flash(sel_ref, g, 0, i + 1, sel_mask)
        o_win = flash(win_ref, g, jnp.maximum(i - WINDOW // tq, 0), i + 1, win_mask)

        for h in range(Q_PER_KV):
            hh = g * Q_PER_KV + h
            gate = lambda br: gates[:, GD_GATE0 + br * N_HEADS + hh:GD_GATE0 + br * N_HEADS + hh + 1]
            o_ref[:, hh * HEAD_DIM:(hh + 1) * HEAD_DIM] = (
                gate(0) * o_cmp[h] + gate(1) * o_sel[h] + gate(2) * o_win[h])


def _attn_prompt(q3, sel3, win3, kc3, gd3, pair):
    b, s, _ = q3.shape
    tq = 128
    ncb = s // CMP_BLOCK
    tile = lambda w: pl.BlockSpec((None, tq, w), lambda bi, i: (bi, i, 0))
    whole = lambda n, w: pl.BlockSpec((None, n, w), lambda bi, i: (bi, 0, 0))
    return pl.pallas_call(
        functools.partial(_attn_prompt_body, tq=tq, seq=s),
        grid=(b, s // tq),
        in_specs=[tile(ATT_WIDTH), whole(s, 2 * KV_WIDTH), whole(s, 2 * KV_WIDTH), whole(ncb, 2 * KV_WIDTH),
                  tile(LANES), pl.BlockSpec(pair.shape, lambda bi, i: (0, 0))],
        out_specs=tile(ATT_WIDTH),
        out_shape=jax.ShapeDtypeStruct((b, s, ATT_WIDTH), F32),
        scratch_shapes=[pltpu.VMEM((s // tq, tq, tq), F32)],
        compiler_params=_params("arbitrary", "arbitrary"),
        name="attn_prompt",
    )(q3, sel3, win3, kc3, gd3, pair)


def _ssd_body(xbc_ref, z_ref, gd_ref, cprev_ref, h0_ref, cw_ref, cb_ref, dtb_ref, alog_ref, dsk_ref, gout_ref,
              selx_ref, y_ref, hout_ref, cout_ref, h_scr, xe_scr, *, cl, l_valid):
    c = pl.program_id(1)
    gn = SSM_GROUPS * SSM_STATE
    hpg = SSM_HEADS // SSM_GROUPS

    @pl.when(c == 0)
    def _():
        h_scr[...] = h0_ref[...]
        xe_scr[0:SUBLANES] = cprev_ref[...]

    xe_scr[SUBLANES:SUBLANES + cl] = xbc_ref[...]
    base = SUBLANES - (CONV_W - 1)
    xc = cb_ref[...]
    for w in range(CONV_W):
        xc = xc + cw_ref[w:w + 1, :] * xe_scr[base + w:base + w + cl, :]
    tail = xe_scr[l_valid:l_valid + SUBLANES, :]
    cout_ref[...] = tail
    xe_scr[0:SUBLANES] = tail
    xc = _silu(xc)
    xs = xc[:, :SSM_WIDTH]
    bmb = xc[:, SSM_WIDTH:SSM_WIDTH + gn].astype(BF16)
    cmb = xc[:, SSM_WIDTH + gn:].astype(BF16)

    dt = _softplus(gd_ref[...] + dtb_ref[...])
    if l_valid < cl:
        dt = jnp.where(lax.broadcasted_iota(jnp.int32, dt.shape, 0) < l_valid, dt, 0.0)
    da = dt * (-jnp.exp(alog_ref[...]))
    ri = lax.broadcasted_iota(jnp.int32, (cl, cl), 0)
    ci = lax.broadcasted_iota(jnp.int32, (cl, cl), 1)
    causal = ri >= ci
    acs = _dot_exact_lhs(jnp.where(causal, 1.0, 0.0).astype(BF16), da)
    acs_rows = sum(_dot_nt(selx_ref[...], p) for p in _split3(acs))
    acs_last = acs[cl - 1:cl, :]
    w_end = jnp.exp(acs_last - acs) * dt
    e_acs = jnp.exp(acs)
    e_last = jnp.exp(acs_last)

    for g in range(SSM_GROUPS):
        bg = bmb[:, g * SSM_STATE:(g + 1) * SSM_STATE]
        cg = cmb[:, g * SSM_STATE:(g + 1) * SSM_STATE]
        cb = _dot_nt(cg, bg)
        for hq in range(hpg):
            h = g * hpg + hq
            col = lambda a: a[:, h:h + 1]
            seg = col(acs) - acs_rows[h * cl:(h + 1) * cl, :]
            decay = jnp.where(causal, jnp.exp(jnp.where(causal, seg, 0.0)), 0.0)
            xh = xs[:, h * SSM_HEAD_DIM:(h + 1) * SSM_HEAD_DIM]
            y_diag = _dot((cb * decay).astype(BF16), (xh * col(dt)).astype(BF16))
            hst = h_scr[h]
            y_off = _dot_nt(cg, hst.astype(BF16)) * col(e_acs)
            h_scr[h] = col(e_last) * hst + _dot_tn((xh * col(w_end)).astype(BF16), bg)
            y_ref[:, h * SSM_HEAD_DIM:(h + 1) * SSM_HEAD_DIM] = y_diag + y_off + col(dsk_ref[...]) * xh

    yg = y_ref[...] * _silu(z_ref[...])
    y_ref[...] = _rms(yg) * gout_ref[...]
    hout_ref[...] = h_scr[...]


def _ssd(xbc3, z3, gd3, cprev, h0, wts, cl, l_valid):
    b, l, _ = xbc3.shape
    nc = l // cl
    tile = lambda w: pl.BlockSpec((None, cl, w), lambda bi, c: (bi, c, 0))
    full = lambda a: pl.BlockSpec(a.shape, lambda bi, c: (0,) * a.ndim)
    names = ["conv_w", "conv_b", "dt_bias", "a_log", "d_skip", "g_ssm"]
    selx = (np.arange(SSM_HEADS * cl)[:, None] // cl == np.arange(LANES)[None, :]).astype(np.float32)
    selx = jnp.asarray(selx, BF16)
    consts = [wts[n] for n in names] + [selx]
    hshape = (SSM_HEADS, SSM_HEAD_DIM, SSM_STATE)
    return pl.pallas_call(
        functools.partial(_ssd_body, cl=cl, l_valid=l_valid),
        grid=(b, nc),
        in_specs=[tile(CONV_DIM), tile(SSM_WIDTH), tile(LANES),
                  pl.BlockSpec((None, SUBLANES, CONV_DIM), lambda bi, c: (bi, 0, 0)),
                  pl.BlockSpec((None,) + hshape, lambda bi, c: (bi, 0, 0, 0))] + [full(a) for a in consts],
        out_specs=[tile(SSM_WIDTH),
                   pl.BlockSpec((None,) + hshape, lambda bi, c: (bi, 0, 0, 0)),
                   pl.BlockSpec((None, SUBLANES, CONV_DIM), lambda bi, c: (bi, 0, 0))],
        out_shape=[jax.ShapeDtypeStruct((b, l, SSM_WIDTH), F32),
                   jax.ShapeDtypeStruct((b,) + hshape, F32),
                   jax.ShapeDtypeStruct((b, SUBLANES, CONV_DIM), F32)],
        scratch_shapes=[pltpu.VMEM(hshape, F32), pltpu.VMEM((SUBLANES + cl, CONV_DIM), F32)],
        compiler_params=_params("arbitrary", "arbitrary"),
        name="ssd",
    )(xbc3, z3, gd3, cprev, h0, *consts)


def _post_body(att_ref, y_ref, x_ref, g1m_ref, sh2_ref, sc2_ref, gatt_ref, wo_ref, gn2_ref, wrh_ref, wrl_ref, br_ref,
               x1_ref, h2_ref, comb_ref):
    an = (_rms(att_ref[...]) * gatt_ref[...]).astype(BF16)
    mixed = _dot(an, wo_ref[0:ATT_WIDTH, :]) + _dot(y_ref[...].astype(BF16), wo_ref[ATT_WIDTH:, :])
    x1 = x_ref[...] + g1m_ref[0] * mixed
    x1_ref[...] = x1
    h2 = _rms(x1) * gn2_ref[...] * (1.0 + sc2_ref[0]) + sh2_ref[0]
    h2_ref[...] = h2.astype(BF16)

    hh, hl = _split2(h2)
    logit = _dot(hh, wrh_ref[...]) + _dot(hl, wrh_ref[...]) + _dot(hh, wrl_ref[...]) + br_ref[...]
    lane = lax.broadcasted_iota(jnp.int32, logit.shape, 1).astype(F32)
    first = lambda cond: jnp.min(jnp.where(cond, lane, 1e6), axis=-1, keepdims=True)
    top = lambda cond: jnp.max(jnp.where(cond, logit, NEG), axis=-1, keepdims=True)
    is_g = lane < N_EGROUPS
    mg = top(is_g)
    gsel = first(is_g & (logit == mg))
    p_top = 1.0 / jnp.sum(jnp.where(is_g, jnp.exp(logit - mg), 0.0), axis=-1, keepdims=True)
    lo = RT_E0 + E_PER_GROUP * gsel
    is_e = (lane >= lo) & (lane < lo + E_PER_GROUP)
    v1 = top(is_e)
    i1 = first(is_e & (logit == v1))
    is_e2 = is_e & (lane != i1)
    v2 = top(is_e2)
    i2 = first(is_e2 & (logit == v2))
    e2 = jnp.exp(v2 - v1)
    comb_ref[...] = (jnp.where(lane == i1, p_top / (1.0 + e2), 0.0)
                     + jnp.where(lane == i2, p_top * e2 / (1.0 + e2), 0.0))


def _post(att2d, y2d, x2d, gate1, shift2, scale2, mod_index, tm, wts):
    t, d = x2d.shape
    rmod = gate1.shape[1]
    row = lambda w: pl.BlockSpec((tm, w), lambda i: (i, 0))
    mod = pl.BlockSpec((1, rmod, d), mod_index)
    full = lambda a: pl.BlockSpec(a.shape, lambda i: (0,) * a.ndim)
    consts = [wts["g_att"], wts["w_out"], wts["g2"], wts["wr_hi"], wts["wr_lo"], wts["br"]]
    return pl.pallas_call(
        _post_body,
        grid=(t // tm,),
        in_specs=[row(ATT_WIDTH), row(SSM_WIDTH), row(d), mod, mod, mod] + [full(a) for a in consts],
        out_specs=[row(d), row(d), row(LANES)],
        out_shape=[jax.ShapeDtypeStruct((t, d), F32), jax.ShapeDtypeStruct((t, d), BF16),
                   jax.ShapeDtypeStruct((t, LANES), F32)],
        compiler_params=_params("arbitrary"),
        name="post",
    )(att2d, y2d, x2d, gate1, shift2, scale2, *consts)


def _moe_body(h_ref, wg_ref, wu_ref, wd_ref, comb_ref, x1_ref, g2m_ref, o_ref, acc_ref):
    e = pl.program_id(1)

    @pl.when(e == 0)
    def _():
        acc_ref[...] = jnp.zeros_like(acc_ref)

    h = h_ref[...]
    he = (_silu(_dot(h, wg_ref[...])) * _dot(h, wu_ref[...])).astype(BF16)
    comb = comb_ref[...]
    lane = lax.broadcasted_iota(jnp.int32, comb.shape, 1)
    cw = jnp.sum(jnp.where(lane == RT_E0 + e, comb, 0.0), axis=-1, keepdims=True)
    acc_ref[...] += cw * _dot(he, wd_ref[...])

    @pl.when(e == N_EXPERTS - 1)
    def _():
        o_ref[...] = x1_ref[...] + g2m_ref[0] * acc_ref[...]


def _moe(h2, comb, x1, gate2, mod_index, tm, wts):
    t, d = x1.shape
    rmod = gate2.shape[1]
    row = lambda w: pl.BlockSpec((tm, w), lambda i, e: (i, 0))
    return pl.pallas_call(
        _moe_body,
        grid=(t // tm, N_EXPERTS),
        in_specs=[row(d),
                  pl.BlockSpec((None, d, EXPERT_FF), lambda i, e: (e, 0, 0)),
                  pl.BlockSpec((None, d, EXPERT_FF), lambda i, e: (e, 0, 0)),
                  pl.BlockSpec((None, EXPERT_FF, d), lambda i, e: (e, 0, 0)),
                  row(LANES), row(d),
                  pl.BlockSpec((1, rmod, d), lambda i, e: mod_index(i))],
        out_specs=row(d),
        out_shape=jax.ShapeDtypeStruct((t, d), F32),
        scratch_shapes=[pltpu.VMEM((tm, d), F32)],
        compiler_params=_params("arbitrary", "arbitrary"),
        name="moe",
    )(h2, wts["w_gate"], wts["w_up"], wts["w_down"], comb, x1, gate2)


def _page_fetch(pt_ref, cache_ref, buf, sem, npages):
    b = pl.program_id(0)
    slot = lax.rem(b, 2)

    def start_all(bb, sl):
        def body(p, carry):
            pltpu.make_async_copy(cache_ref.at[pt_ref[bb, p]], buf.at[sl, p], sem.at[sl]).start()
            return carry
        lax.fori_loop(0, npages, body, 0)

    @pl.when(b == 0)
    def _():
        start_all(0, 0)

    @pl.when(b + 1 < pl.num_programs(0))
    def _():
        start_all(b + 1, 1 - slot)

    def wait_one(p, carry):
        pltpu.make_async_copy(cache_ref.at[0], buf.at[slot, p], sem.at[slot]).wait()
        return carry
    lax.fori_loop(0, npages, wait_one, 0)
    return slot


def _cmp_sample_body(pt_ref, cache_ref, qbd_ref, wkv_ref, gkc_ref, bd_ref, pair_ref, rsum_ref, rexp_ref,
                     ocmp_ref, mask_ref, buf, sem, kc_scr, *, npages, past_len, tdec):
    slot = _page_fetch(pt_ref, cache_ref, buf, sem, npages)
    nblk = past_len // SEL_BLOCK
    bpp = PAGE_SIZE // CMP_BLOCK
    ppi = SUBLANES // bpp

    def pool(i, carry):
        pages = buf[slot, pl.ds(i * ppi, ppi)]
        kc_scr[pl.ds(pl.multiple_of(i * SUBLANES, SUBLANES), SUBLANES), :] = (
            pages.reshape(SUBLANES, CMP_BLOCK, 2 * KV_WIDTH) * wkv_ref[...][None]).sum(axis=1)
        return carry
    lax.fori_loop(0, npages // ppi, pool, 0)

    kcn = _head_rms(kc_scr[:, 0:KV_WIDTH], bd_ref[...]) * gkc_ref[...]
    vc = kc_scr[:, KV_WIDTH:]
    s = _dot_nt(qbd_ref[...], kcn.astype(BF16))
    e = jnp.exp(s - s.max(axis=-1, keepdims=True))
    p = e / jnp.maximum(e.sum(axis=-1, keepdims=True), TINY)
    o = _dot(p.astype(BF16), vc.astype(BF16))
    nrow = o.shape[0]
    first_group = lax.broadcasted_iota(jnp.int32, (nrow, HEAD_DIM), 0) < nrow // N_KV
    ocmp_ref[...] = jnp.where(first_group, o[:, :HEAD_DIM], o[:, HEAD_DIM:])

    imp = _dot_exact_rhs(_dot_exact_lhs(rsum_ref[...], p), pair_ref[...])
    lane = lax.broadcasted_iota(jnp.int32, imp.shape, 1)
    tok = lax.rem(lax.broadcasted_iota(jnp.int32, imp.shape, 0), tdec)
    cur = (past_len + tok) // SEL_BLOCK
    allowed = (lane < cur) & (lane < nblk)
    forced = (lane == 0) | (lane == cur - 1)
    score = jnp.where(allowed, jnp.where(forced, BIG, imp), NEG)
    selm = jnp.where(allowed, _topk_mask(score, nblk, min(N_SEL - 1, nblk)), 0.0)
    mask_ref[...] = _dot(rexp_ref[...], selm.astype(BF16))


def _sel_sample_body(pt_ref, cache_ref, qbd_ref, mask_ref, eexp_ref, selnew_ref, winc_ref, winnew_ref, gt_ref,
                     ocmp_ref, o_ref, buf, sem, *, npages, past_len, tdec, ppc):
    slot = _page_fetch(pt_ref, cache_ref, buf, sem, npages)
    qbd = qbd_ref[...]
    nrow = qbd.shape[0]
    tok = lax.rem(lax.broadcasted_iota(jnp.int32, (nrow, 1), 0), tdec)
    kch = ppc * PAGE_SIZE

    def update(carry, k, v, mask):
        m, l, acc = carry
        s = jnp.where(mask, _dot_nt(qbd, k), NEG)
        m_new = jnp.maximum(m, s.max(axis=-1, keepdims=True))
        alpha = jnp.exp(m - m_new)
        e = jnp.where(mask, jnp.exp(s - m_new), 0.0)
        return m_new, alpha * l + e.sum(axis=-1, keepdims=True), alpha * acc + _dot(e.astype(BF16), v)

    def finish(carry):
        _, l, acc = carry
        o = acc / jnp.maximum(l, TINY)
        first_group = lax.broadcasted_iota(jnp.int32, (nrow, HEAD_DIM), 0) < nrow // N_KV
        return jnp.where(first_group, o[:, :HEAD_DIM], o[:, HEAD_DIM:])

    init = (jnp.full((nrow, 1), NEG, F32), jnp.zeros((nrow, 1), F32), jnp.zeros((nrow, KV_WIDTH), F32))

    def new_rows(carry, new_ref):
        kv = new_ref[...]
        idx = lax.broadcasted_iota(jnp.int32, (nrow, kv.shape[0]), 1)
        return update(carry, kv[:, :KV_WIDTH].astype(BF16), kv[:, KV_WIDTH:].astype(BF16), (idx <= tok) & (idx < tdec))

    selm = mask_ref[...].astype(BF16)
    carry = init
    for c in range(npages // ppc):
        kv = buf[slot, c * ppc:(c + 1) * ppc]
        k = kv[:, :, 0:KV_WIDTH].reshape(kch, KV_WIDTH).astype(BF16)
        v = kv[:, :, KV_WIDTH:].reshape(kch, KV_WIDTH).astype(BF16)
        carry = update(carry, k, v, _dot(selm, eexp_ref[:, c * kch:(c + 1) * kch]) > 0.5)
    o_sel = finish(new_rows(carry, selnew_ref))

    wc = winc_ref[...]
    wbuf = wc.shape[0]
    rel = wbuf + tok - lax.broadcasted_iota(jnp.int32, (nrow, wbuf), 1)
    carry = update(init, wc[:, :KV_WIDTH].astype(BF16), wc[:, KV_WIDTH:].astype(BF16), (rel >= 0) & (rel < WINDOW))
    o_win = finish(new_rows(carry, winnew_ref))

    gates = _sigmoid(gt_ref[...])
    o_ref[...] = gates[:, 0:1] * ocmp_ref[...] + gates[:, 1:2] * o_sel + gates[:, 2:3] * o_win


def _attn_sample(qbd, gt, page_table, cache_cmp, cache_sel, cache_win, sel_new, win_new, wts, past_len, tdec):
    nb, nrow, _ = qbd.shape
    npages = page_table.shape[1]
    ncb = past_len // CMP_BLOCK
    nblk = past_len // SEL_BLOCK
    per_b = lambda r, w: pl.BlockSpec((None, r, w), lambda b, pt: (b, 0, 0))
    full = lambda a: pl.BlockSpec(a.shape, lambda b, pt: (0,) * a.ndim)
    hbm = pl.BlockSpec(memory_space=pl.ANY)
    page_buf = pltpu.VMEM((2, npages, PAGE_SIZE, 2 * KV_WIDTH), F32)

    r = np.arange(nrow)
    grp = (r // tdec) // Q_PER_KV * tdec + r % tdec
    rsum = (np.arange(N_KV * tdec)[:, None] == grp[None, :]).astype(np.float32)
    pair = np.zeros((ncb, LANES), np.float32)
    pair[np.arange(ncb), np.arange(ncb) // (SEL_BLOCK // CMP_BLOCK)] = 1.0
    eexp = (np.arange(LANES)[:, None] == np.arange(past_len)[None, :] // SEL_BLOCK).astype(np.float32)
    consts1 = [wts["wkv"], wts["gkc"], wts["bd_kv"], jnp.asarray(pair, BF16), jnp.asarray(rsum, BF16),
               jnp.asarray(rsum.T, BF16)]

    o_cmp, mask = pl.pallas_call(
        functools.partial(_cmp_sample_body, npages=npages, past_len=past_len, tdec=tdec),
        grid_spec=pltpu.PrefetchScalarGridSpec(
            num_scalar_prefetch=1, grid=(nb,),
            in_specs=[hbm, per_b(nrow, KV_WIDTH)] + [full(a) for a in consts1],
            out_specs=[per_b(nrow, HEAD_DIM), per_b(nrow, LANES)],
            scratch_shapes=[page_buf, pltpu.SemaphoreType.DMA((2,)), pltpu.VMEM((ncb, 2 * KV_WIDTH), F32)]),
        out_shape=[jax.ShapeDtypeStruct((nb, nrow, HEAD_DIM), F32), jax.ShapeDtypeStruct((nb, nrow, LANES), F32)],
        compiler_params=_params("arbitrary"),
        name="attn_sample_cmp",
    )(page_table, cache_cmp, qbd, *consts1)

    ppc = min(8, npages)
    eexp = jnp.asarray(eexp, BF16)
    return pl.pallas_call(
        functools.partial(_sel_sample_body, npages=npages, past_len=past_len, tdec=tdec, ppc=ppc),
        grid_spec=pltpu.PrefetchScalarGridSpec(
            num_scalar_prefetch=1, grid=(nb,),
            in_specs=[hbm, per_b(nrow, KV_WIDTH), per_b(nrow, LANES), full(eexp),
                      per_b(sel_new.shape[1], 2 * KV_WIDTH), per_b(cache_win.shape[1], 2 * KV_WIDTH),
                      per_b(win_new.shape[1], 2 * KV_WIDTH), per_b(nrow, SUBLANES), per_b(nrow, HEAD_DIM)],
            out_specs=per_b(nrow, HEAD_DIM),
            scratch_shapes=[page_buf, pltpu.SemaphoreType.DMA((2,))]),
        out_shape=jax.ShapeDtypeStruct((nb, nrow, HEAD_DIM), F32),
        compiler_params=_params("arbitrary"),
        name="attn_sample_sel",
    )(page_table, cache_sel, qbd, mask, eexp, sel_new, cache_win, win_new, gt, o_cmp)


def _prep_weights(l, g_norm1, g_norm2, w_in, g_q, g_k_cmp, g_k_sel, g_k_win, w_pos_k, w_pos_v, conv_w, conv_b,
                  dt_bias, a_log, d_skip, g_att_out, g_ssm_out, w_out, w_rg, b_rg, w_re, b_re, w_gate, w_up, w_down):
    w = w_in[l]
    o_gt = ATT_WIDTH + 6 * KV_WIDTH
    o_z = o_gt + 3 * N_HEADS
    o_xbc = o_z + SSM_WIDTH
    o_dt = o_xbc + CONV_DIM
    pad = jnp.zeros((D_MODEL, PK_WIDTH - PK_GD - SSM_HEADS - 3 * N_HEADS), F32)
    w_packed = jnp.concatenate([w[:, :o_gt], w[:, o_z:o_xbc], w[:, o_xbc:o_dt], w[:, o_dt:], w[:, o_gt:o_z], pad],
                               axis=1).astype(BF16)
    seg = np.arange(ATT_WIDTH) // HEAD_DIM
    bd = jnp.asarray((seg[:, None] == seg[None, :]).astype(np.float32), BF16)
    lane_pad = lambda v: jnp.pad(v.astype(F32), (0, LANES - v.shape[0])).reshape(1, LANES)
    wr = jnp.concatenate([w_rg[l], jnp.transpose(w_re[l], (1, 0, 2)).reshape(D_MODEL, N_EXPERTS)], axis=1)
    wr = jnp.pad(wr, ((0, 0), (0, LANES - wr.shape[1])))
    wr_hi = wr.astype(BF16)
    wr_lo = (wr - wr_hi.astype(F32)).astype(BF16)
    return {
        "g1": g_norm1[l].reshape(1, D_MODEL), "w_in": w_packed,
        "gq": (jnp.tile(g_q[l], N_HEADS) * SCALE).reshape(1, ATT_WIDTH),
        "gks": jnp.tile(g_k_sel[l], N_KV).reshape(1, KV_WIDTH),
        "gkw": jnp.tile(g_k_win[l], N_KV).reshape(1, KV_WIDTH),
        "gkc": jnp.tile(g_k_cmp[l], N_KV).reshape(1, KV_WIDTH),
        "bd": bd, "bd_kv": bd[:KV_WIDTH, :KV_WIDTH],
        "wkv": jnp.concatenate([jnp.broadcast_to(w_pos_k[l][:, None], (CMP_BLOCK, KV_WIDTH)),
                                jnp.broadcast_to(w_pos_v[l][:, None], (CMP_BLOCK, KV_WIDTH))], axis=1),
        "conv_w": conv_w[l], "conv_b": conv_b[l].reshape(1, CONV_DIM),
        "dt_bias": lane_pad(dt_bias[l]), "a_log": lane_pad(a_log[l]), "d_skip": lane_pad(d_skip[l]),
        "g_ssm": g_ssm_out[l].reshape(1, SSM_WIDTH),
        "g_att": g_att_out[l].reshape(1, ATT_WIDTH), "w_out": w_out[l].astype(BF16),
        "g2": g_norm2[l].reshape(1, D_MODEL), "wr_hi": wr_hi, "wr_lo": wr_lo,
        "br": lane_pad(jnp.concatenate([b_rg[l], b_re[l].reshape(-1)])),
        "w_gate": w_gate[l].astype(BF16), "w_up": w_up[l].astype(BF16), "w_down": w_down[l].astype(BF16),
    }


def _finish(att2d, y2d, x2d, mods, mod_index, tm_post, tm_moe, moe_index, wts):
    x1, h2, comb = _post(att2d, y2d, x2d, mods[2], mods[3], mods[4], mod_index, tm_post, wts)
    return _moe(h2, comb, x1, mods[5], moe_index, tm_moe, wts)


def kernel(x_prompt, x_sample, cache_cmp, cache_sel, cache_win, state_ssm, state_conv, page_table, c_prompt, c_sample,
           g_norm1, g_norm2, w_ada, b_ada, w_in, g_q, g_k_cmp, g_k_sel, g_k_win, w_pos_k, w_pos_v, conv_w, conv_b,
           dt_bias, a_log, d_skip, g_att_out, g_ssm_out, w_out, w_rg, b_rg, w_re, b_re, w_gate, w_up, w_down):
    depth = w_in.shape[0]
    assert depth == 1
    l = 0
    bp, seq, d = x_prompt.shape
    bs, tdec, _ = x_sample.shape
    npages = page_table.shape[1]
    past_len = npages * PAGE_SIZE
    n_pool = cache_cmp.shape[1]
    tp, ts = bp * seq, bs * tdec

    wts = _prep_weights(l, g_norm1, g_norm2, w_in, g_q, g_k_cmp, g_k_sel, g_k_win, w_pos_k, w_pos_v, conv_w, conv_b,
                        dt_bias, a_log, d_skip, g_att_out, g_ssm_out, w_out, w_rg, b_rg, w_re, b_re, w_gate, w_up,
                        w_down)
    m_all = _ada(jnp.concatenate([c_prompt, c_sample], axis=0), w_ada[l], b_ada[l])
    mods_p = [m.reshape(bp, 1, d) for m in jnp.split(m_all[:bp], 6, axis=-1)]
    tm_s = min(512, ts)
    mods_s = [jnp.repeat(m, tdec, axis=0).reshape(ts // tm_s, tm_s, d) for m in jnp.split(m_all[bp:], 6, axis=-1)]

    tm = 512
    per_seq = seq // tm
    xp2 = x_prompt.reshape(tp, d)
    q, cmp_p, sel_p, win_p, z, xbc, gd, kc = _in_proj(
        xp2, mods_p[0], mods_p[1], lambda i: (i // per_seq, 0, 0), tm, wts, True)
    ncb = seq // CMP_BLOCK
    pair = np.zeros((ncb, LANES), np.float32)
    pair[np.arange(ncb), np.arange(ncb) // (SEL_BLOCK // CMP_BLOCK)] = 1.0
    r3 = lambda a, n: a.reshape(bp, n, a.shape[-1])
    att = _attn_prompt(r3(q, seq), r3(sel_p, seq), r3(win_p, seq), r3(kc, ncb), r3(gd, seq), jnp.asarray(pair, BF16))
    y_ssm, ssm_p, conv_p = _ssd(r3(xbc, seq), r3(z, seq), r3(gd, seq), jnp.zeros((bp, SUBLANES, CONV_DIM), F32),
                                jnp.zeros((bp, SSM_HEADS, SSM_HEAD_DIM, SSM_STATE), F32), wts, SSD_CHUNK, SSD_CHUNK)
    tm_moe = 1024
    y_prompt = _finish(att.reshape(tp, ATT_WIDTH), y_ssm.reshape(tp, SSM_WIDTH), xp2, mods_p,
                       lambda i: (i // per_seq, 0, 0), tm, tm_moe, lambda i: (i // (seq // tm_moe), 0, 0), wts)

    xs2 = x_sample.reshape(ts, d)
    q, cmp_s, sel_s, win_s, z, xbc, gd = _in_proj(xs2, mods_s[0], mods_s[1], lambda i: (i, 0, 0), tm_s, wts, False)
    nrow = N_HEADS * tdec
    qh = q.reshape(bs, tdec, N_HEADS, HEAD_DIM).transpose(0, 2, 1, 3).reshape(bs, nrow, HEAD_DIM)
    g0 = (jnp.arange(nrow) < nrow // N_KV)[None, :, None]
    zero = jnp.zeros_like(qh)
    qbd = jnp.concatenate([jnp.where(g0, qh, zero), jnp.where(g0, zero, qh)], axis=-1)
    gt = gd[:, GD_GATE0:GD_GATE0 + 3 * N_HEADS].reshape(bs, tdec, 3, N_HEADS).transpose(0, 3, 1, 2)
    gt = jnp.pad(gt.reshape(bs, nrow, 3), ((0, 0), (0, 0), (0, SUBLANES - 3)))
    new_pad = 2 * SUBLANES - tdec
    padded = lambda a: jnp.pad(a.reshape(bs, tdec, 2 * KV_WIDTH), ((0, 0), (0, new_pad), (0, 0)))
    flat_cache = lambda c: c[l].reshape(n_pool, PAGE_SIZE, 2 * KV_WIDTH)
    wbuf = cache_win.shape[2]
    cwin = cache_win[l].reshape(bs, wbuf, 2 * KV_WIDTH)
    att_s = _attn_sample(qbd, gt, page_table, flat_cache(cache_cmp), flat_cache(cache_sel), cwin,
                         padded(sel_s), padded(win_s), wts, past_len, tdec)
    att_s = att_s.reshape(bs, N_HEADS, tdec, HEAD_DIM).transpose(0, 2, 1, 3).reshape(ts, ATT_WIDTH)

    lpad = SUBLANES - tdec
    pad_rows = lambda a: jnp.pad(a.reshape(bs, tdec, a.shape[-1]), ((0, 0), (0, lpad), (0, 0)))
    cprev = jnp.pad(state_conv[l], ((0, 0), (SUBLANES - (CONV_W - 1), 0), (0, 0)))
    y_s, ssm_s, conv_s = _ssd(pad_rows(xbc), pad_rows(z), pad_rows(gd), cprev, state_ssm[l].astype(F32), wts,
                              SUBLANES, tdec)
    y_s = y_s[:, :tdec].reshape(ts, SSM_WIDTH)
    y_sample = _finish(att_s, y_s, xs2, mods_s, lambda i: (i, 0, 0), tm_s, tm_s, lambda i: (i, 0, 0), wts)

    kv6 = lambda a, b, n: a.reshape(1, b, n, 2, N_KV, HEAD_DIM)
    wkeep = min(WINDOW, seq)
    win_prompt = win_p.reshape(bp, seq, 2 * KV_WIDTH)[:, seq - wkeep:]
    win_all = jnp.concatenate([cwin, win_s.reshape(bs, tdec, 2 * KV_WIDTH)], axis=1)
    wkeep_s = min(WINDOW, wbuf + tdec)
    tail = lambda a: a[:, SUBLANES - (CONV_W - 1):][None]
    return (y_prompt.reshape(bp, seq, d), y_sample.reshape(bs, tdec, d),
            kv6(cmp_p, bp, seq), kv6(cmp_s, bs, tdec), kv6(sel_p, bp, seq), kv6(sel_s, bs, tdec),
            kv6(win_prompt, bp, wkeep), kv6(win_all[:, wbuf + tdec - wkeep_s:], bs, wkeep_s),
            ssm_p[None], ssm_s[None].astype(state_ssm.dtype), tail(conv_p), tail(conv_s))
```

```python
import functools

import numpy as np
import jax
import jax.numpy as jnp
from jax import lax
from jax.experimental import pallas as pl
from jax.experimental.pallas import tpu as pltpu

F32 = jnp.float32
BF16 = jnp.bfloat16

D_MODEL = 1024
PAGE_SIZE = 128
N_HEADS = 8
N_KV = 2
HEAD_DIM = 64
Q_PER_KV = N_HEADS // N_KV
ATT_WIDTH = N_HEADS * HEAD_DIM
KV_WIDTH = N_KV * HEAD_DIM
CMP_BLOCK = 32
SEL_BLOCK = 64
N_SEL = 16
WINDOW = 512
SCALE = HEAD_DIM ** -0.5
SSM_HEADS = 8
SSM_HEAD_DIM = 64
SSM_WIDTH = SSM_HEADS * SSM_HEAD_DIM
SSM_GROUPS = 2
SSM_STATE = 64
CONV_W = 4
SSD_CHUNK = 128
CONV_DIM = SSM_WIDTH + 2 * SSM_GROUPS * SSM_STATE
N_EGROUPS = 4
E_PER_GROUP = 4
N_EXPERTS = N_EGROUPS * E_PER_GROUP
EXPERT_FF = 256
EPS = 1e-6
NEG = -1e30
BIG = 1e9
TINY = 1e-30

LANES = 128
SUBLANES = 8
VMEM_LIMIT = 48 * 1024 * 1024
KEY_CHUNK = 128

PK_Q = 0
PK_CMP = 512
PK_SEL = 768
PK_WIN = 1024
PK_Z = 1280
PK_XBC = 1792
PK_GD = 2560
PK_WIDTH = 2688
GD_GATE0 = 8
RT_E0 = 4


def _dot(a, b):
    return jnp.dot(a, b, preferred_element_type=F32)


def _dot_nt(a, b):
    return lax.dot_general(a, b, (((1,), (1,)), ((), ())), preferred_element_type=F32)


def _dot_tn(a, b):
    return lax.dot_general(a, b, (((0,), (0,)), ((), ())), preferred_element_type=F32)


def _split2(x):
    hi = x.astype(BF16)
    lo = (x - hi.astype(F32)).astype(BF16)
    return hi, lo


def _split3(x):
    hi = x.astype(BF16)
    r = x - hi.astype(F32)
    mid = r.astype(BF16)
    lo = (r - mid.astype(F32)).astype(BF16)
    return hi, mid, lo


def _dot_exact_lhs(a_bf16, x):
    return sum(_dot(a_bf16, p) for p in _split3(x))


def _dot_exact_rhs(x, b_bf16):
    return sum(_dot(p, b_bf16) for p in _split3(x))


def _sigmoid(x):
    return 1.0 / (1.0 + jnp.exp(-x))


def _silu(x):
    return x * _sigmoid(x)


def _softplus(x):
    return jnp.maximum(x, 0.0) + jnp.log1p(jnp.exp(-jnp.abs(x)))


def _rms(x):
    return x * lax.rsqrt(jnp.mean(x * x, axis=-1, keepdims=True) + EPS)


def _head_rms(x, bd):
    hi, lo = _split2(x * x)
    ms = (_dot(hi, bd) + _dot(lo, bd)) * (1.0 / HEAD_DIM)
    return x * lax.rsqrt(ms + EPS)


def _topk_mask_lanes(score, nblk, k):
    lane = lax.broadcasted_iota(jnp.int32, score.shape, 1)
    rank = jnp.zeros(score.shape, F32)
    for i in range(nblk):
        si = score[:, i:i + 1]
        beats = jnp.where(si > score, 1.0, jnp.where(si == score, jnp.where(lane > i, 1.0, 0.0), 0.0))
        rank = rank + beats
    return jnp.where(rank < k, 1.0, 0.0)


def _topk_mask_rows(score, k):
    row = lax.broadcasted_iota(jnp.int32, score.shape, 0)
    rank = jnp.zeros(score.shape, F32)
    for i in range(score.shape[0]):
        si = score[i:i + 1, :]
        beats = jnp.where(si > score, 1.0, jnp.where(si == score, jnp.where(row > i, 1.0, 0.0), 0.0))
        rank = rank + beats
    return jnp.where(rank < k, 1.0, 0.0)


def _params(*sem):
    return pltpu.CompilerParams(dimension_semantics=sem, vmem_limit_bytes=VMEM_LIMIT)


def _ada_body(c_ref, w_ref, b_ref, o_ref):
    c = _silu(c_ref[...]).astype(BF16)
    o_ref[...] = _dot(c, w_ref[...].astype(BF16)) + b_ref[...]


def _ada(c_all, w_ada, b_ada):
    n, d = c_all.shape
    width = w_ada.shape[1]
    tn = 1536
    return pl.pallas_call(
        _ada_body,
        grid=(width // tn,),
        in_specs=[pl.BlockSpec((n, d), lambda j: (0, 0)),
                  pl.BlockSpec((d, tn), lambda j: (0, j)),
                  pl.BlockSpec((1, tn), lambda j: (0, j))],
        out_specs=pl.BlockSpec((n, tn), lambda j: (0, j)),
        out_shape=jax.ShapeDtypeStruct((n, width), F32),
        compiler_params=_params("arbitrary"),
        name="ada",
    )(c_all, w_ada, b_ada.reshape(1, width))


def _inproj_body(with_cmp, x_ref, sh_ref, sc_ref, g1_ref, w_ref, gqc_ref, gks_ref, gkw_ref, bd_ref, *rest):
    if with_cmp:
        wkv_ref, gkc_ref = rest[:2]
        rest = rest[2:]
    (qt_ref, cmpt_ref, selt_ref, selk_ref, selvt_ref, wint_ref, wink_ref, winvt_ref,
     z_ref, xbc_ref, gd_ref, gdt_ref) = rest[:12]
    tm = x_ref.shape[0]
    h = _rms(x_ref[...]) * g1_ref[...]
    hb = (h * (1.0 + sc_ref[0]) + sh_ref[0]).astype(BF16)
    bd_kv = bd_ref[...]

    q3 = _dot(hb, w_ref[:, PK_Q:PK_CMP]).T.reshape(N_HEADS, HEAD_DIM, tm)
    q3 = q3 * lax.rsqrt(jnp.mean(q3 * q3, axis=1, keepdims=True) + EPS)
    qt_ref[...] = (q3.reshape(ATT_WIDTH, tm) * gqc_ref[...]).astype(BF16)

    kv = _dot(hb, w_ref[:, PK_CMP:PK_SEL])
    cmpt_ref[...] = kv.T
    if with_cmp:
        kc_ref = rest[12]
        pooled = (kv.reshape(tm // CMP_BLOCK, CMP_BLOCK, 2 * KV_WIDTH) * wkv_ref[...][None]).sum(axis=1)
        kc_ref[:, 0:KV_WIDTH] = _head_rms(pooled[:, :KV_WIDTH], bd_kv) * gkc_ref[...]
        kc_ref[:, KV_WIDTH:] = pooled[:, KV_WIDTH:]

    def normed_kv(lo, hi, gain_ref, t_ref, k_ref, vt_ref):
        kv = _dot(hb, w_ref[:, lo:hi])
        kn = _head_rms(kv[:, :KV_WIDTH], bd_kv) * gain_ref[...]
        k_ref[...] = kn.astype(BF16)
        kvt = jnp.concatenate([kn, kv[:, KV_WIDTH:]], axis=1).T
        t_ref[...] = kvt
        vt = kvt[KV_WIDTH:, :].astype(BF16)
        for j in range(tm // KEY_CHUNK):
            vt_ref[j] = vt[:, j * KEY_CHUNK:(j + 1) * KEY_CHUNK]

    normed_kv(PK_SEL, PK_WIN, gks_ref, selt_ref, selk_ref, selvt_ref)
    normed_kv(PK_WIN, PK_Z, gkw_ref, wint_ref, wink_ref, winvt_ref)

    z_ref[...] = _dot(hb, w_ref[:, PK_Z:PK_XBC])
    xbc_ref[...] = _dot(hb, w_ref[:, PK_XBC:PK_GD])
    gd = _dot(hb, w_ref[:, PK_GD:PK_WIDTH])
    gd_ref[...] = gd
    gdt_ref[...] = gd.T


def _in_proj(x2d, nbatch, shift3, scale3, mod_index, tm, wts, with_cmp):
    t, d = x2d.shape
    seq = t // nbatch
    per_seq = seq // tm
    rmod = shift3.shape[1]
    full = lambda a: pl.BlockSpec(a.shape, lambda i: (0,) * a.ndim)
    row = lambda w: pl.BlockSpec((tm, w), lambda i: (i, 0))
    fmaj = lambda w: pl.BlockSpec((None, w, tm), lambda i: (i // per_seq, 0, i % per_seq))
    chunks = pl.BlockSpec((tm // KEY_CHUNK, KV_WIDTH, KEY_CHUNK), lambda i: (i, 0, 0))
    mod = pl.BlockSpec((1, rmod, d), mod_index)
    ins = [x2d, shift3, scale3, wts["g1"], wts["w_in"], wts["gq_col"], wts["gks"], wts["gkw"], wts["bd_kv"]]
    in_specs = [row(d), mod, mod] + [full(a) for a in ins[3:]]
    if with_cmp:
        ins += [wts["wkv"], wts["gkc"]]
        in_specs += [full(wts["wkv"]), full(wts["gkc"])]
    sds = jax.ShapeDtypeStruct
    fm = lambda w, dt: sds((nbatch, w, seq), dt)
    ck = sds((t // KEY_CHUNK, KV_WIDTH, KEY_CHUNK), BF16)
    out_shape = [fm(ATT_WIDTH, BF16), fm(2 * KV_WIDTH, F32),
                 fm(2 * KV_WIDTH, F32), sds((t, KV_WIDTH), BF16), ck,
                 fm(2 * KV_WIDTH, F32), sds((t, KV_WIDTH), BF16), ck,
                 sds((t, SSM_WIDTH), F32), sds((t, CONV_DIM), F32), sds((t, LANES), F32), fm(LANES, F32)]
    out_specs = [fmaj(ATT_WIDTH), fmaj(2 * KV_WIDTH),
                 fmaj(2 * KV_WIDTH), row(KV_WIDTH), chunks,
                 fmaj(2 * KV_WIDTH), row(KV_WIDTH), chunks,
                 row(SSM_WIDTH), row(CONV_DIM), row(LANES), fmaj(LANES)]
    if with_cmp:
        out_shape.append(sds((t // CMP_BLOCK, 2 * KV_WIDTH), F32))
        out_specs.append(pl.BlockSpec((tm // CMP_BLOCK, 2 * KV_WIDTH), lambda i: (i, 0)))
    return pl.pallas_call(
        functools.partial(_inproj_body, with_cmp),
        grid=(t // tm,),
        in_specs=in_specs,
        out_specs=out_specs,
        out_shape=out_shape,
        compiler_params=_params("arbitrary"),
        name="in_proj",
    )(*ins)


def _attn_prompt_body(qt_ref, selk_ref, selvt_ref, wink_ref, winvt_ref, kc_ref, vct_ref, gdt_ref, o_ref, selm_scr,
                      *, tq, seq):
    i = pl.program_id(1)
    ncb = seq // CMP_BLOCK
    nblk = seq // SEL_BLOCK
    bpc = KEY_CHUNK // SEL_BLOCK
    wide = Q_PER_KV * tq
    qt = qt_ref[...]
    t_row = i * tq + lax.broadcasted_iota(jnp.int32, (1, tq), 1)
    cur_row = t_row // SEL_BLOCK
    key_i = lax.broadcasted_iota(jnp.int32, (KEY_CHUNK, tq), 0)
    gates = _sigmoid(gdt_ref[GD_GATE0:GD_GATE0 + 3 * N_HEADS, :])
    rep = lambda a: jnp.concatenate([a] * Q_PER_KV, axis=1)

    def flash(k_ref, vt_ref, qgts, c_lo, c_hi, mask_fn):
        def step(c, carry):
            off = pl.multiple_of(c * KEY_CHUNK, KEY_CHUNK)
            scores = [_dot(k_ref[pl.ds(off, KEY_CHUNK), g * HEAD_DIM:(g + 1) * HEAD_DIM], qgts[g])
                      for g in range(N_KV)]
            probs = []
            for g in range(N_KV):
                m, l, _ = carry[g]
                s = jnp.where(rep(mask_fn(c, g)), scores[g], NEG)
                m_new = jnp.maximum(m, s.max(axis=0, keepdims=True))
                alpha = jnp.exp(m - m_new)
                e = jnp.exp(s - m_new)
                probs.append((m_new, alpha, alpha * l + e.sum(axis=0, keepdims=True), e.astype(BF16)))
            out = []
            for g in range(N_KV):
                m_new, alpha, l, e = probs[g]
                pv = _dot(vt_ref[c, g * HEAD_DIM:(g + 1) * HEAD_DIM, :], e)
                out.append((m_new, l, alpha * carry[g][2] + pv))
            return tuple(out)

        init = (jnp.full((1, wide), NEG, F32), jnp.zeros((1, wide), F32), jnp.zeros((HEAD_DIM, wide), F32))
        res = lax.fori_loop(c_lo, c_hi, step, (init,) * N_KV)
        return [acc / jnp.maximum(l, TINY) for _, l, acc in res]

    qgts, o_cmps = [], []
    for g in range(N_KV):
        qgt = jnp.concatenate(
            [qt[(g * Q_PER_KV + h) * HEAD_DIM:(g * Q_PER_KV + h + 1) * HEAD_DIM, :] for h in range(Q_PER_KV)], axis=1)
        qgts.append(qgt)

        kcg = kc_ref[:, g * HEAD_DIM:(g + 1) * HEAD_DIM]
        vctg = vct_ref[g * HEAD_DIM:(g + 1) * HEAD_DIM, :]
        r = lax.broadcasted_iota(jnp.int32, (ncb, tq), 0)
        cblk = jnp.where(r < nblk, 2 * r, 2 * (r - nblk) + 1)
        cmask = rep((cblk * CMP_BLOCK + CMP_BLOCK - 1) <= t_row)
        s = jnp.where(cmask, _dot(kcg, qgt), NEG)
        e = jnp.where(cmask, jnp.exp(s - s.max(axis=0, keepdims=True)), 0.0)
        p = e / jnp.maximum(e.sum(axis=0, keepdims=True), TINY)
        o_cmps.append(_dot(vctg, p.astype(BF16)))

        imp = sum(p[:, h * tq:(h + 1) * tq] for h in range(Q_PER_KV))
        imp = imp[:nblk] + imp[nblk:]
        jb = lax.broadcasted_iota(jnp.int32, (nblk, tq), 0)
        allowed = jb < cur_row
        forced = (jb == 0) | (jb == cur_row - 1)
        score = jnp.where(allowed, jnp.where(forced, BIG, imp), NEG)
        selm = jnp.where(allowed, _topk_mask_rows(score, min(N_SEL - 1, nblk)), 0.0)
        for c in range(seq // KEY_CHUNK):
            selm_scr[g, c, 0:bpc] = selm[c * bpc:(c + 1) * bpc]

    def sel_mask(c, g):
        rows = selm_scr[g, c]
        picked = jnp.concatenate(
            [jnp.broadcast_to(rows[j:j + 1], (SEL_BLOCK, tq)) for j in range(bpc)], axis=0) > 0.5
        kp = c * KEY_CHUNK + key_i
        return picked | (((kp // SEL_BLOCK) == cur_row) & (kp <= t_row))

    def win_mask(c, g):
        rel = t_row - (c * KEY_CHUNK + key_i)
        return (rel >= 0) & (rel < WINDOW)

    o_sels = flash(selk_ref, selvt_ref, qgts, 0, i + 1, sel_mask)
    o_wins = flash(wink_ref, winvt_ref, qgts, jnp.maximum(i - WINDOW // KEY_CHUNK, 0), i + 1, win_mask)

    heads = []
    for g in range(N_KV):
        for h in range(Q_PER_KV):
            hh = g * Q_PER_KV + h
            sl = slice(h * tq, (h + 1) * tq)
            gate = lambda br: gates[br * N_HEADS + hh:br * N_HEADS + hh + 1, :]
            heads.append(gate(0) * o_cmps[g][:, sl] + gate(1) * o_sels[g][:, sl] + gate(2) * o_wins[g][:, sl])
    o_ref[...] = jnp.concatenate(heads, axis=0).T


def _attn_prompt(qt, selk, selvt, wink, winvt, kc, vct, gdt):
    b, _, s = qt.shape
    tq = KEY_CHUNK
    ncb = s // CMP_BLOCK
    nch = s // KEY_CHUNK
    tile_t = lambda w: pl.BlockSpec((None, w, tq), lambda bi, i: (bi, 0, i))
    whole = lambda shape: pl.BlockSpec((None,) + shape, lambda bi, i: (bi,) + (0,) * len(shape))
    return pl.pallas_call(
        functools.partial(_attn_prompt_body, tq=tq, seq=s),
        grid=(b, s // tq),
        in_specs=[tile_t(ATT_WIDTH), whole((s, KV_WIDTH)), whole((nch, KV_WIDTH, KEY_CHUNK)),
                  whole((s, KV_WIDTH)), whole((nch, KV_WIDTH, KEY_CHUNK)),
                  whole((ncb, KV_WIDTH)), whole((KV_WIDTH, ncb)), tile_t(LANES)],
        out_specs=pl.BlockSpec((None, tq, ATT_WIDTH), lambda bi, i: (bi, i, 0)),
        out_shape=jax.ShapeDtypeStruct((b, s, ATT_WIDTH), F32),
        scratch_shapes=[pltpu.VMEM((N_KV, nch, SUBLANES, tq), F32)],
        compiler_params=_params("arbitrary", "arbitrary"),
        name="attn_prompt",
    )(qt, selk, selvt, wink, winvt, kc, vct, gdt)


def _ssd_body(xbc_ref, z_ref, gd_ref, cprev_ref, h0_ref, cw_ref, cb_ref, dtb_ref, alog_ref, dsk_ref, gout_ref,
              selx_ref, y_ref, hout_ref, cout_ref, h_scr, xe_scr, *, cl, l_valid):
    c = pl.program_id(1)
    gn = SSM_GROUPS * SSM_STATE
    hpg = SSM_HEADS // SSM_GROUPS

    @pl.when(c == 0)
    def _():
        h_scr[...] = h0_ref[...]
        xe_scr[0:SUBLANES] = cprev_ref[...]

    xe_scr[SUBLANES:SUBLANES + cl] = xbc_ref[...]
    base = SUBLANES - (CONV_W - 1)
    xc = cb_ref[...]
    for w in range(CONV_W):
        xc = xc + cw_ref[w:w + 1, :] * xe_scr[base + w:base + w + cl, :]
    tail = xe_scr[l_valid:l_valid + SUBLANES, :]
    cout_ref[...] = tail
    xe_scr[0:SUBLANES] = tail
    xc = _silu(xc)
    xs = xc[:, :SSM_WIDTH]
    bmb = xc[:, SSM_WIDTH:SSM_WIDTH + gn].astype(BF16)
    cmb = xc[:, SSM_WIDTH + gn:].astype(BF16)

    dt = _softplus(gd_ref[...] + dtb_ref[...])
    if l_valid < cl:
        dt = jnp.where(lax.broadcasted_iota(jnp.int32, dt.shape, 0) < l_valid, dt, 0.0)
    da = dt * (-jnp.exp(alog_ref[...]))
    ri = lax.broadcasted_iota(jnp.int32, (cl, cl), 0)
    ci = lax.broadcasted_iota(jnp.int32, (cl, cl), 1)
    causal = ri >= ci
    acs = _dot_exact_lhs(jnp.where(causal, 1.0, 0.0).astype(BF16), da)
    acs_rows = sum(_dot_nt(selx_ref[...], p) for p in _split3(acs))
    acs_last = acs[cl - 1:cl, :]
    w_end = jnp.exp(acs_last - acs) * dt
    e_acs = jnp.exp(acs)
    e_last = jnp.exp(acs_last)

    for g in range(SSM_GROUPS):
        bg = bmb[:, g * SSM_STATE:(g + 1) * SSM_STATE]
        cg = cmb[:, g * SSM_STATE:(g + 1) * SSM_STATE]
        cb = _dot_nt(cg, bg)
        for hq in range(hpg):
            h = g * hpg + hq
            col = lambda a: a[:, h:h + 1]
            seg = col(acs) - acs_rows[h * cl:(h + 1) * cl, :]
            decay = jnp.where(causal, jnp.exp(jnp.where(causal, seg, 0.0)), 0.0)
            xh = xs[:, h * SSM_HEAD_DIM:(h + 1) * SSM_HEAD_DIM]
            y_diag = _dot((cb * decay).astype(BF16), (xh * col(dt)).astype(BF16))
            hst = h_scr[h]
            y_off = _dot_nt(cg, hst.astype(BF16)) * col(e_acs)
            h_scr[h] = col(e_last) * hst + _dot_tn((xh * col(w_end)).astype(BF16), bg)
            y_ref[:, h * SSM_HEAD_DIM:(h + 1) * SSM_HEAD_DIM] = y_diag + y_off + col(dsk_ref[...]) * xh

    yg = y_ref[...] * _silu(z_ref[...])
    y_ref[...] = _rms(yg) * gout_ref[...]
    hout_ref[...] = h_scr[...]


def _ssd(xbc3, z3, gd3, cprev, h0, wts, cl, l_valid):
    b, l, _ = xbc3.shape
    nc = l // cl
    tile = lambda w: pl.BlockSpec((None, cl, w), lambda bi, c: (bi, c, 0))
    full = lambda a: pl.BlockSpec(a.shape, lambda bi, c: (0,) * a.ndim)
    names = ["conv_w", "conv_b", "dt_bias", "a_log", "d_skip", "g_ssm"]
    selx = (np.arange(SSM_HEADS * cl)[:, None] // cl == np.arange(LANES)[None, :]).astype(np.float32)
    selx = jnp.asarray(selx, BF16)
    consts = [wts[n] for n in names] + [selx]
    hshape = (SSM_HEADS, SSM_HEAD_DIM, SSM_STATE)
    return pl.pallas_call(
        functools.partial(_ssd_body, cl=cl, l_valid=l_valid),
        grid=(b, nc),
        in_specs=[tile(CONV_DIM), tile(SSM_WIDTH), tile(LANES),
                  pl.BlockSpec((None, SUBLANES, CONV_DIM), lambda bi, c: (bi, 0, 0)),
                  pl.BlockSpec((None,) + hshape, lambda bi, c: (bi, 0, 0, 0))] + [full(a) for a in consts],
        out_specs=[tile(SSM_WIDTH),
                   pl.BlockSpec((None,) + hshape, lambda bi, c: (bi, 0, 0, 0)),
                   pl.BlockSpec((None, SUBLANES, CONV_DIM), lambda bi, c: (bi, 0, 0))],
        out_shape=[jax.ShapeDtypeStruct((b, l, SSM_WIDTH), F32),
                   jax.ShapeDtypeStruct((b,) + hshape, F32),
                   jax.ShapeDtypeStruct((b, SUBLANES, CONV_DIM), F32)],
        scratch_shapes=[pltpu.VMEM(hshape, F32), pltpu.VMEM((SUBLANES + cl, CONV_DIM), F32)],
        compiler_params=_params("arbitrary", "arbitrary"),
        name="ssd",
    )(xbc3, z3, gd3, cprev, h0, *consts)


def _post_body(att_ref, y_ref, x_ref, g1m_ref, sh2_ref, sc2_ref, gatt_ref, wo_ref, gn2_ref, wrh_ref, wrl_ref, br_ref,
               x1_ref, h2_ref, comb_ref):
    an = (_rms(att_ref[...]) * gatt_ref[...]).astype(BF16)
    mixed = _dot(an, wo_ref[0:ATT_WIDTH, :]) + _dot(y_ref[...].astype(BF16), wo_ref[ATT_WIDTH:, :])
    x1 = x_ref[...] + g1m_ref[0] * mixed
    x1_ref[...] = x1
    h2 = _rms(x1) * gn2_ref[...] * (1.0 + sc2_ref[0]) + sh2_ref[0]
    h2_ref[...] = h2.astype(BF16)

    hh, hl = _split2(h2)
    logit = _dot(hh, wrh_ref[...]) + _dot(hl, wrh_ref[...]) + _dot(hh, wrl_ref[...]) + br_ref[...]
    lane = lax.broadcasted_iota(jnp.int32, logit.shape, 1).astype(F32)
    first = lambda cond: jnp.min(jnp.where(cond, lane, 1e6), axis=-1, keepdims=True)
    top = lambda cond: jnp.max(jnp.where(cond, logit, NEG), axis=-1, keepdims=True)
    is_g = lane < N_EGROUPS
    mg = top(is_g)
    gsel = first(is_g & (logit == mg))
    p_top = 1.0 / jnp.sum(jnp.where(is_g, jnp.exp(logit - mg), 0.0), axis=-1, keepdims=True)
    lo = RT_E0 + E_PER_GROUP * gsel
    is_e = (lane >= lo) & (lane < lo + E_PER_GROUP)
    v1 = top(is_e)
    i1 = first(is_e & (logit == v1))
    is_e2 = is_e & (lane != i1)
    v2 = top(is_e2)
    i2 = first(is_e2 & (logit == v2))
    e2 = jnp.exp(v2 - v1)
    comb_ref[...] = (jnp.where(lane == i1, p_top / (1.0 + e2), 0.0)
                     + jnp.where(lane == i2, p_top * e2 / (1.0 + e2), 0.0))


def _post(att2d, y2d, x2d, gate1, shift2, scale2, mod_index, tm, wts):
    t, d = x2d.shape
    rmod = gate1.shape[1]
    row = lambda w: pl.BlockSpec((tm, w), lambda i: (i, 0))
    mod = pl.BlockSpec((1, rmod, d), mod_index)
    full = lambda a: pl.BlockSpec(a.shape, lambda i: (0,) * a.ndim)
    consts = [wts["g_att"], wts["w_out"], wts["g2"], wts["wr_hi"], wts["wr_lo"], wts["br"]]
    return pl.pallas_call(
        _post_body,
        grid=(t // tm,),
        in_specs=[row(ATT_WIDTH), row(SSM_WIDTH), row(d), mod, mod, mod] + [full(a) for a in consts],
        out_specs=[row(d), row(d), row(LANES)],
        out_shape=[jax.ShapeDtypeStruct((t, d), F32), jax.ShapeDtypeStruct((t, d), BF16),
                   jax.ShapeDtypeStruct((t, LANES), F32)],
        compiler_params=_params("arbitrary"),
        name="post",
    )(att2d, y2d, x2d, gate1, shift2, scale2, *consts)


def _moe_body(h_ref, wg_ref, wu_ref, wd_ref, comb_ref, x1_ref, g2m_ref, o_ref, acc_ref):
    e = pl.program_id(1)

    @pl.when(e == 0)
    def _():
        acc_ref[...] = jnp.zeros_like(acc_ref)

    h = h_ref[...]
    he = (_silu(_dot(h, wg_ref[...])) * _dot(h, wu_ref[...])).astype(BF16)
    comb = comb_ref[...]
    lane = lax.broadcasted_iota(jnp.int32, comb.shape, 1)
    cw = jnp.sum(jnp.where(lane == RT_E0 + e, comb, 0.0), axis=-1, keepdims=True)
    acc_ref[...] += cw * _dot(he, wd_ref[...])

    @pl.when(e == N_EXPERTS - 1)
    def _():
        o_ref[...] = x1_ref[...] + g2m_ref[0] * acc_ref[...]


def _moe(h2, comb, x1, gate2, mod_index, tm, wts):
    t, d = x1.shape
    rmod = gate2.shape[1]
    row = lambda w: pl.BlockSpec((tm, w), lambda i, e: (i, 0))
    return pl.pallas_call(
        _moe_body,
        grid=(t // tm, N_EXPERTS),
        in_specs=[row(d),
                  pl.BlockSpec((None, d, EXPERT_FF), lambda i, e: (e, 0, 0)),
                  pl.BlockSpec((None, d, EXPERT_FF), lambda i, e: (e, 0, 0)),
                  pl.BlockSpec((None, EXPERT_FF, d), lambda i, e: (e, 0, 0)),
                  row(LANES), row(d),
                  pl.BlockSpec((1, rmod, d), lambda i, e: mod_index(i))],
        out_specs=row(d),
        out_shape=jax.ShapeDtypeStruct((t, d), F32),
        scratch_shapes=[pltpu.VMEM((tm, d), F32)],
        compiler_params=_params("arbitrary", "arbitrary"),
        name="moe",
    )(h2, wts["w_gate"], wts["w_up"], wts["w_down"], comb, x1, gate2)


def _page_fetch(pt_ref, cache_ref, buf, sem, npages):
    b = pl.program_id(0)
    slot = lax.rem(b, 2)

    def start_all(bb, sl):
        def body(p, carry):
            pltpu.make_async_copy(cache_ref.at[pt_ref[bb, p]], buf.at[sl, p], sem.at[sl]).start()
            return carry
        lax.fori_loop(0, npages, body, 0)

    @pl.when(b == 0)
    def _():
        start_all(0, 0)

    @pl.when(b + 1 < pl.num_programs(0))
    def _():
        start_all(b + 1, 1 - slot)

    def wait_one(p, carry):
        pltpu.make_async_copy(cache_ref.at[0], buf.at[slot, p], sem.at[slot]).wait()
        return carry
    lax.fori_loop(0, npages, wait_one, 0)
    return slot


def _cmp_sample_body(pt_ref, cache_ref, qbd_ref, pw_hi_ref, pw_all_ref, gkc_ref, bd_ref, pair_ref, rsum_ref, rexp_ref,
                     ocmp_ref, mask_ref, buf, sem, kc_scr, *, npages, past_len, tdec):
    slot = _page_fetch(pt_ref, cache_ref, buf, sem, npages)
    nblk = past_len // SEL_BLOCK
    ppi = SUBLANES // (PAGE_SIZE // CMP_BLOCK)
    kw = KV_WIDTH

    def pool(i, carry):
        kc8 = jnp.zeros((SUBLANES, kw), F32)
        vc8 = jnp.zeros((SUBLANES, kw), F32)
        for j in range(ppi):
            hi, lo = _split2(buf[slot, i * ppi + j])
            r1 = _dot_nt(pw_all_ref[j], hi)
            r2 = _dot_nt(pw_hi_ref[j], lo)
            s8 = SUBLANES
            kc8 = kc8 + r1[0:s8, :kw] + r1[s8:2 * s8, :kw] + r2[0:s8, :kw]
            vc8 = vc8 + r1[2 * s8:3 * s8, kw:] + r1[3 * s8:, kw:] + r2[s8:, kw:]
        rows = pl.ds(pl.multiple_of(i * SUBLANES, SUBLANES), SUBLANES)
        kc_scr[rows, 0:kw] = kc8
        kc_scr[rows, kw:] = vc8
        return carry
    lax.fori_loop(0, npages // ppi, pool, 0)

    kcn = _head_rms(kc_scr[:, 0:kw], bd_ref[...]) * gkc_ref[...]
    vc = kc_scr[:, kw:]
    s = _dot_nt(qbd_ref[...], kcn.astype(BF16))
    e = jnp.exp(s - s.max(axis=-1, keepdims=True))
    p = e / jnp.maximum(e.sum(axis=-1, keepdims=True), TINY)
    o = _dot(p.astype(BF16), vc.astype(BF16))
    nrow = o.shape[0]
    first_group = lax.broadcasted_iota(jnp.int32, (nrow, HEAD_DIM), 0) < nrow // N_KV
    ocmp_ref[...] = jnp.where(first_group, o[:, :HEAD_DIM], o[:, HEAD_DIM:])

    imp = _dot_exact_rhs(_dot_exact_lhs(rsum_ref[...], p), pair_ref[...])
    lane = lax.broadcasted_iota(jnp.int32, imp.shape, 1)
    tok = lax.rem(lax.broadcasted_iota(jnp.int32, imp.shape, 0), tdec)
    cur = (past_len + tok) // SEL_BLOCK
    allowed = (lane < cur) & (lane < nblk)
    forced = (lane == 0) | (lane == cur - 1)
    score = jnp.where(allowed, jnp.where(forced, BIG, imp), NEG)
    selm = jnp.where(allowed, _topk_mask_lanes(score, nblk, min(N_SEL - 1, nblk)), 0.0)
    mask_ref[...] = _dot(rexp_ref[...], selm.astype(BF16))


def _sel_sample_body(pt_ref, cache_ref, qbd_ref, mask_ref, eexp_ref, selnew_ref, winc_ref, winnew_ref, gt_ref,
                     ocmp_ref, o_ref, buf, sem, *, npages, past_len, tdec, ppc):
    slot = _page_fetch(pt_ref, cache_ref, buf, sem, npages)
    qbd = qbd_ref[...]
    nrow = qbd.shape[0]
    tok = lax.rem(lax.broadcasted_iota(jnp.int32, (nrow, 1), 0), tdec)
    kch = ppc * PAGE_SIZE
    kw = KV_WIDTH

    def update(carry, s, mask, pv):
        m, l, acc = carry
        s = jnp.where(mask, s, NEG)
        m_new = jnp.maximum(m, s.max(axis=-1, keepdims=True))
        alpha = jnp.exp(m - m_new)
        e = jnp.where(mask, jnp.exp(s - m_new), 0.0)
        return m_new, alpha * l + e.sum(axis=-1, keepdims=True), alpha * acc + pv(e.astype(BF16))

    def finish(carry):
        _, l, acc = carry
        o = acc / jnp.maximum(l, TINY)
        first_group = lax.broadcasted_iota(jnp.int32, (nrow, HEAD_DIM), 0) < nrow // N_KV
        return jnp.where(first_group, o[:, :HEAD_DIM], o[:, HEAD_DIM:])

    init = (jnp.full((nrow, 1), NEG, F32), jnp.zeros((nrow, 1), F32), jnp.zeros((nrow, kw), F32))

    def feature_major(carry, kt, vt, mask):
        return update(carry, _dot(qbd, kt.astype(BF16)), mask, lambda e: _dot_nt(e, vt.astype(BF16)))

    def new_rows(carry, new_ref):
        kv = new_ref[...]
        idx = lax.broadcasted_iota(jnp.int32, (nrow, kv.shape[0]), 1)
        v = kv[:, kw:].astype(BF16)
        return update(carry, _dot_nt(qbd, kv[:, :kw].astype(BF16)), (idx <= tok) & (idx < tdec), lambda e: _dot(e, v))

    selm = mask_ref[...].astype(BF16)
    carry = init
    for c in range(npages // ppc):
        pages = [buf[slot, c * ppc + j] for j in range(ppc)]
        kt = jnp.concatenate([pg[:kw] for pg in pages], axis=1)
        vt = jnp.concatenate([pg[kw:] for pg in pages], axis=1)
        carry = feature_major(carry, kt, vt, _dot(selm, eexp_ref[:, c * kch:(c + 1) * kch]) > 0.5)
    o_sel = finish(new_rows(carry, selnew_ref))

    wbuf = winc_ref.shape[1]
    rel = wbuf + tok - lax.broadcasted_iota(jnp.int32, (nrow, wbuf), 1)
    carry = feature_major(init, winc_ref[0:kw, :], winc_ref[kw:, :], (rel >= 0) & (rel < WINDOW))
    o_win = finish(new_rows(carry, winnew_ref))

    gates = _sigmoid(gt_ref[...])
    o_ref[...] = gates[:, 0:1] * ocmp_ref[...] + gates[:, 1:2] * o_sel + gates[:, 2:3] * o_win


def _attn_sample(qbd, gt, page_table, cache_cmp_t, cache_sel_t, cache_win_t, sel_new, win_new, wts, past_len, tdec):
    nb, nrow, _ = qbd.shape
    npages = page_table.shape[1]
    ncb = past_len // CMP_BLOCK
    per_b = lambda r, w: pl.BlockSpec((None, r, w), lambda b, pt: (b, 0, 0))
    full = lambda a: pl.BlockSpec(a.shape, lambda b, pt: (0,) * a.ndim)
    hbm = pl.BlockSpec(memory_space=pl.ANY)
    page_buf = pltpu.VMEM((2, npages, 2 * KV_WIDTH, PAGE_SIZE), F32)

    r = np.arange(nrow)
    grp = (r // tdec) // Q_PER_KV * tdec + r % tdec
    rsum = (np.arange(N_KV * tdec)[:, None] == grp[None, :]).astype(np.float32)
    pair = np.zeros((ncb, LANES), np.float32)
    pair[np.arange(ncb), np.arange(ncb) // (SEL_BLOCK // CMP_BLOCK)] = 1.0
    eexp = (np.arange(LANES)[:, None] == np.arange(past_len)[None, :] // SEL_BLOCK).astype(np.float32)
    consts1 = [wts["pw_hi"], wts["pw_all"], wts["gkc"], wts["bd_kv"], jnp.asarray(pair, BF16),
               jnp.asarray(rsum, BF16), jnp.asarray(rsum.T, BF16)]

    o_cmp, mask = pl.pallas_call(
        functools.partial(_cmp_sample_body, npages=npages, past_len=past_len, tdec=tdec),
        grid_spec=pltpu.PrefetchScalarGridSpec(
            num_scalar_prefetch=1, grid=(nb,),
            in_specs=[hbm, per_b(nrow, KV_WIDTH)] + [full(a) for a in consts1],
            out_specs=[per_b(nrow, HEAD_DIM), per_b(nrow, LANES)],
            scratch_shapes=[page_buf, pltpu.SemaphoreType.DMA((2,)), pltpu.VMEM((ncb, 2 * KV_WIDTH), F32)]),
        out_shape=[jax.ShapeDtypeStruct((nb, nrow, HEAD_DIM), F32), jax.ShapeDtypeStruct((nb, nrow, LANES), F32)],
        compiler_params=_params("arbitrary"),
        name="attn_sample_cmp",
    )(page_table, cache_cmp_t, qbd, *consts1)

    ppc = min(8, npages)
    eexp = jnp.asarray(eexp, BF16)
    return pl.pallas_call(
        functools.partial(_sel_sample_body, npages=npages, past_len=past_len, tdec=tdec, ppc=ppc),
        grid_spec=pltpu.PrefetchScalarGridSpec(
            num_scalar_prefetch=1, grid=(nb,),
            in_specs=[hbm, per_b(nrow, KV_WIDTH), per_b(nrow, LANES), full(eexp),
                      per_b(sel_new.shape[1], 2 * KV_WIDTH), per_b(2 * KV_WIDTH, cache_win_t.shape[2]),
                      per_b(win_new.shape[1], 2 * KV_WIDTH), per_b(nrow, SUBLANES), per_b(nrow, HEAD_DIM)],
            out_specs=per_b(nrow, HEAD_DIM),
            scratch_shapes=[page_buf, pltpu.SemaphoreType.DMA((2,))]),
        out_shape=jax.ShapeDtypeStruct((nb, nrow, HEAD_DIM), F32),
        compiler_params=_params("arbitrary"),
        name="attn_sample_sel",
    )(page_table, cache_sel_t, qbd, mask, eexp, sel_new, cache_win_t, win_new, gt, o_cmp)


def _pool_weights(w_pos_k, w_pos_v):
    bpp = PAGE_SIZE // CMP_BLOCK
    ppi = SUBLANES // bpp
    rows = np.arange(PAGE_SIZE)
    place = np.zeros((ppi, SUBLANES, PAGE_SIZE), np.float32)
    for j in range(ppi):
        place[j, j * bpp + rows // CMP_BLOCK, rows] = 1.0
    place = jnp.asarray(place)

    def hi_lo(w):
        full = place * jnp.tile(w, bpp)[None, None, :]
        hi = full.astype(BF16)
        return hi, (full - hi.astype(F32)).astype(BF16)

    k_hi, k_lo = hi_lo(w_pos_k)
    v_hi, v_lo = hi_lo(w_pos_v)
    return jnp.concatenate([k_hi, v_hi], axis=1), jnp.concatenate([k_hi, k_lo, v_hi, v_lo], axis=1)


def _prep_weights(l, g_norm1, g_norm2, w_in, g_q, g_k_cmp, g_k_sel, g_k_win, w_pos_k, w_pos_v, conv_w, conv_b,
                  dt_bias, a_log, d_skip, g_att_out, g_ssm_out, w_out, w_rg, b_rg, w_re, b_re, w_gate, w_up, w_down):
    w = w_in[l]
    o_gt = ATT_WIDTH + 6 * KV_WIDTH
    o_z = o_gt + 3 * N_HEADS
    o_xbc = o_z + SSM_WIDTH
    o_dt = o_xbc + CONV_DIM
    pad = jnp.zeros((D_MODEL, PK_WIDTH - PK_GD - SSM_HEADS - 3 * N_HEADS), F32)
    w_packed = jnp.concatenate([w[:, :o_gt], w[:, o_z:o_xbc], w[:, o_xbc:o_dt], w[:, o_dt:], w[:, o_gt:o_z], pad],
                               axis=1).astype(BF16)
    seg = np.arange(KV_WIDTH) // HEAD_DIM
    bd_kv = jnp.asarray((seg[:, None] == seg[None, :]).astype(np.float32), BF16)
    lane_pad = lambda v: jnp.pad(v.astype(F32), (0, LANES - v.shape[0])).reshape(1, LANES)
    wr = jnp.concatenate([w_rg[l], jnp.transpose(w_re[l], (1, 0, 2)).reshape(D_MODEL, N_EXPERTS)], axis=1)
    wr = jnp.pad(wr, ((0, 0), (0, LANES - wr.shape[1])))
    wr_hi = wr.astype(BF16)
    wr_lo = (wr - wr_hi.astype(F32)).astype(BF16)
    pw_hi, pw_all = _pool_weights(w_pos_k[l], w_pos_v[l])
    return {
        "g1": g_norm1[l].reshape(1, D_MODEL), "w_in": w_packed,
        "gq_col": (jnp.tile(g_q[l], N_HEADS) * SCALE).reshape(ATT_WIDTH, 1),
        "gks": jnp.tile(g_k_sel[l], N_KV).reshape(1, KV_WIDTH),
        "gkw": jnp.tile(g_k_win[l], N_KV).reshape(1, KV_WIDTH),
        "gkc": jnp.tile(g_k_cmp[l], N_KV).reshape(1, KV_WIDTH),
        "bd_kv": bd_kv,
        "wkv": jnp.concatenate([jnp.broadcast_to(w_pos_k[l][:, None], (CMP_BLOCK, KV_WIDTH)),
                                jnp.broadcast_to(w_pos_v[l][:, None], (CMP_BLOCK, KV_WIDTH))], axis=1),
        "pw_hi": pw_hi, "pw_all": pw_all,
        "conv_w": conv_w[l], "conv_b": conv_b[l].reshape(1, CONV_DIM),
        "dt_bias": lane_pad(dt_bias[l]), "a_log": lane_pad(a_log[l]), "d_skip": lane_pad(d_skip[l]),
        "g_ssm": g_ssm_out[l].reshape(1, SSM_WIDTH),
        "g_att": g_att_out[l].reshape(1, ATT_WIDTH), "w_out": w_out[l].astype(BF16),
        "g2": g_norm2[l].reshape(1, D_MODEL), "wr_hi": wr_hi, "wr_lo": wr_lo,
        "br": lane_pad(jnp.concatenate([b_rg[l], b_re[l].reshape(-1)])),
        "w_gate": w_gate[l].astype(BF16), "w_up": w_up[l].astype(BF16), "w_down": w_down[l].astype(BF16),
    }


def _finish(att2d, y2d, x2d, mods, mod_index, tm_post, tm_moe, moe_index, wts):
    x1, h2, comb = _post(att2d, y2d, x2d, mods[2], mods[3], mods[4], mod_index, tm_post, wts)
    return _moe(h2, comb, x1, mods[5], moe_index, tm_moe, wts)


def _token_major_cache(t):
    b, _, n = t.shape
    return jnp.transpose(t.reshape(b, 2, N_KV, HEAD_DIM, n), (0, 4, 1, 2, 3))[None]


def _feature_major_cache(c):
    n, rows = c.shape[:2]
    return jnp.transpose(c, (0, 2, 3, 4, 1)).reshape(n, 2 * KV_WIDTH, rows)


def kernel(x_prompt, x_sample, cache_cmp, cache_sel, cache_win, state_ssm, state_conv, page_table, c_prompt, c_sample,
           g_norm1, g_norm2, w_ada, b_ada, w_in, g_q, g_k_cmp, g_k_sel, g_k_win, w_pos_k, w_pos_v, conv_w, conv_b,
           dt_bias, a_log, d_skip, g_att_out, g_ssm_out, w_out, w_rg, b_rg, w_re, b_re, w_gate, w_up, w_down):
    depth = w_in.shape[0]
    assert depth == 1
    l = 0
    bp, seq, d = x_prompt.shape
    bs, tdec, _ = x_sample.shape
    npages = page_table.shape[1]
    past_len = npages * PAGE_SIZE
    tp, ts = bp * seq, bs * tdec

    wts = _prep_weights(l, g_norm1, g_norm2, w_in, g_q, g_k_cmp, g_k_sel, g_k_win, w_pos_k, w_pos_v, conv_w, conv_b,
                        dt_bias, a_log, d_skip, g_att_out, g_ssm_out, w_out, w_rg, b_rg, w_re, b_re, w_gate, w_up,
                        w_down)
    m_all = _ada(jnp.concatenate([c_prompt, c_sample], axis=0), w_ada[l], b_ada[l])
    mods_p = [m.reshape(bp, 1, d) for m in jnp.split(m_all[:bp], 6, axis=-1)]
    tm_s = min(512, ts)
    mods_s = [jnp.repeat(m, tdec, axis=0).reshape(ts // tm_s, tm_s, d) for m in jnp.split(m_all[bp:], 6, axis=-1)]

    tm = 512
    per_seq = seq // tm
    xp2 = x_prompt.reshape(tp, d)
    qt, cmpt_p, selt_p, selk, selvt, wint_p, wink, winvt, z, xbc, gd, gdt, kc = _in_proj(
        xp2, bp, mods_p[0], mods_p[1], lambda i: (i // per_seq, 0, 0), tm, wts, True)
    ncb = seq // CMP_BLOCK
    nch = seq // KEY_CHUNK
    r3 = lambda a, n: a.reshape(bp, n, a.shape[-1])
    kc_eo = kc.reshape(bp, ncb // 2, 2, 2 * KV_WIDTH).transpose(0, 2, 1, 3).reshape(bp, ncb, 2 * KV_WIDTH)
    kcmp = kc_eo[:, :, :KV_WIDTH].astype(BF16)
    vcmp_t = jnp.transpose(kc_eo[:, :, KV_WIDTH:], (0, 2, 1)).astype(BF16)
    chunked = lambda a: a.reshape(bp, nch, KV_WIDTH, KEY_CHUNK)
    att = _attn_prompt(qt, r3(selk, seq), chunked(selvt), r3(wink, seq), chunked(winvt), kcmp, vcmp_t, gdt)
    y_ssm, ssm_p, conv_p = _ssd(r3(xbc, seq), r3(z, seq), r3(gd, seq), jnp.zeros((bp, SUBLANES, CONV_DIM), F32),
                                jnp.zeros((bp, SSM_HEADS, SSM_HEAD_DIM, SSM_STATE), F32), wts, SSD_CHUNK, SSD_CHUNK)
    tm_moe = 1024
    y_prompt = _finish(att.reshape(tp, ATT_WIDTH), y_ssm.reshape(tp, SSM_WIDTH), xp2, mods_p,
                       lambda i: (i // per_seq, 0, 0), tm, tm_moe, lambda i: (i // (seq // tm_moe), 0, 0), wts)

    xs2 = x_sample.reshape(ts, d)
    qt_s, cmpt_s, selt_s, _, _, wint_s, _, _, z, xbc, gd, _ = _in_proj(
        xs2, 1, mods_s[0], mods_s[1], lambda i: (i, 0, 0), tm_s, wts, False)
    nrow = N_HEADS * tdec
    qh = qt_s[0].T.reshape(bs, tdec, N_HEADS, HEAD_DIM).transpose(0, 2, 1, 3).reshape(bs, nrow, HEAD_DIM)
    g0 = (jnp.arange(nrow) < nrow // N_KV)[None, :, None]
    zero = jnp.zeros_like(qh)
    qbd = jnp.concatenate([jnp.where(g0, qh, zero), jnp.where(g0, zero, qh)], axis=-1)
    gt = gd[:, GD_GATE0:GD_GATE0 + 3 * N_HEADS].reshape(bs, tdec, 3, N_HEADS).transpose(0, 3, 1, 2)
    gt = jnp.pad(gt.reshape(bs, nrow, 3), ((0, 0), (0, 0), (0, SUBLANES - 3)))
    rows_s = lambda t: t[0].T.reshape(bs, tdec, 2 * KV_WIDTH)
    sel_s, win_s = rows_s(selt_s), rows_s(wint_s)
    new_pad = 2 * SUBLANES - tdec
    padded = lambda a: jnp.pad(a, ((0, 0), (0, new_pad), (0, 0)))
    cwin_t = _feature_major_cache(cache_win[l])
    att_s = _attn_sample(qbd, gt, page_table, _feature_major_cache(cache_cmp[l]), _feature_major_cache(cache_sel[l]),
                         cwin_t, padded(sel_s), padded(win_s), wts, past_len, tdec)
    att_s = att_s.reshape(bs, N_HEADS, tdec, HEAD_DIM).transpose(0, 2, 1, 3).reshape(ts, ATT_WIDTH)

    lpad = SUBLANES - tdec
    pad_rows = lambda a: jnp.pad(a.reshape(bs, tdec, a.shape[-1]), ((0, 0), (0, lpad), (0, 0)))
    cprev = jnp.pad(state_conv[l], ((0, 0), (SUBLANES - (CONV_W - 1), 0), (0, 0)))
    y_s, ssm_s, conv_s = _ssd(pad_rows(xbc), pad_rows(z), pad_rows(gd), cprev, state_ssm[l].astype(F32), wts,
                              SUBLANES, tdec)
    y_s = y_s[:, :tdec].reshape(ts, SSM_WIDTH)
    y_sample = _finish(att_s, y_s, xs2, mods_s, lambda i: (i, 0, 0), tm_s, tm_s, lambda i: (i, 0, 0), wts)

    kv6 = lambda a: a.reshape(1, bs, tdec, 2, N_KV, HEAD_DIM)
    wkeep = min(WINDOW, seq)
    wbuf = cwin_t.shape[2]
    wkeep_s = min(WINDOW, wbuf + tdec)
    win_all_t = jnp.concatenate([cwin_t, jnp.transpose(win_s, (0, 2, 1))], axis=2)[:, :, wbuf + tdec - wkeep_s:]
    tail = lambda a: a[:, SUBLANES - (CONV_W - 1):][None]
    return (y_prompt.reshape(bp, seq, d), y_sample.reshape(bs, tdec, d),
            _token_major_cache(cmpt_p), kv6(rows_s(cmpt_s)), _token_major_cache(selt_p), kv6(sel_s),
            _token_major_cache(wint_p[:, :, seq - wkeep:]), _token_major_cache(win_all_t),
            ssm_p[None], ssm_s[None].astype(state_ssm.dtype), tail(conv_p), tail(conv_s))
```

```python
import functools

import numpy as np
import jax
import jax.numpy as jnp
from jax import lax
from jax.experimental import pallas as pl
from jax.experimental.pallas import tpu as pltpu

F32 = jnp.float32
BF16 = jnp.bfloat16

D_MODEL = 1024
PAGE_SIZE = 128
N_HEADS = 8
N_KV = 2
HEAD_DIM = 64
Q_PER_KV = N_HEADS // N_KV
ATT_WIDTH = N_HEADS * HEAD_DIM
KV_WIDTH = N_KV * HEAD_DIM
CMP_BLOCK = 32
SEL_BLOCK = 64
N_SEL = 16
WINDOW = 512
SCALE = HEAD_DIM ** -0.5
SSM_HEADS = 8
SSM_HEAD_DIM = 64
SSM_WIDTH = SSM_HEADS * SSM_HEAD_DIM
SSM_GROUPS = 2
SSM_STATE = 64
CONV_W = 4
SSD_CHUNK = 128
CONV_DIM = SSM_WIDTH + 2 * SSM_GROUPS * SSM_STATE
N_EGROUPS = 4
E_PER_GROUP = 4
N_EXPERTS = N_EGROUPS * E_PER_GROUP
EXPERT_FF = 256
EPS = 1e-6
NEG = -1e30
BIG = 1e9
TINY = 1e-30

LANES = 128
SUBLANES = 8
VMEM_LIMIT = 48 * 1024 * 1024
KEY_CHUNK = 128
FLASH_UNROLL = 2
DEAD_SHIFT = 1 << 24
POOL_UNROLL = 16

PK_Q = 0
PK_CMP = 512
PK_SEL = 768
PK_WIN = 1024
PK_Z = 1280
PK_XBC = 1792
PK_GD = 2560
PK_WIDTH = 2688
GD_GATE0 = 8
RT_E0 = 4


def _dot(a, b):
    return jnp.dot(a, b, preferred_element_type=F32)


def _dot_nt(a, b):
    return lax.dot_general(a, b, (((1,), (1,)), ((), ())), preferred_element_type=F32)


def _dot_tn(a, b):
    return lax.dot_general(a, b, (((0,), (0,)), ((), ())), preferred_element_type=F32)


def _split2(x):
    hi = x.astype(BF16)
    lo = (x - hi.astype(F32)).astype(BF16)
    return hi, lo


def _split3(x):
    hi = x.astype(BF16)
    r = x - hi.astype(F32)
    mid = r.astype(BF16)
    lo = (r - mid.astype(F32)).astype(BF16)
    return hi, mid, lo


def _dot_exact_lhs(a_bf16, x):
    return sum(_dot(a_bf16, p) for p in _split3(x))


def _dot_exact_rhs(x, b_bf16):
    return sum(_dot(p, b_bf16) for p in _split3(x))


def _sigmoid(x):
    return 1.0 / (1.0 + jnp.exp(-x))


def _silu(x):
    return x * _sigmoid(x)


def _softplus(x):
    return jnp.maximum(x, 0.0) + jnp.log1p(jnp.exp(-jnp.abs(x)))


def _rms(x):
    return x * lax.rsqrt(jnp.mean(x * x, axis=-1, keepdims=True) + EPS)


def _head_rms(x, bd):
    hi, lo = _split2(x * x)
    ms = (_dot(hi, bd) + _dot(lo, bd)) * (1.0 / HEAD_DIM)
    return x * lax.rsqrt(ms + EPS)


def _topk_mask_lanes(score, nblk, k):
    lane = lax.broadcasted_iota(jnp.int32, score.shape, 1)
    rank = jnp.zeros(score.shape, F32)
    for i in range(nblk):
        si = score[:, i:i + 1]
        beats = jnp.where(si > score, 1.0, jnp.where(si == score, jnp.where(lane > i, 1.0, 0.0), 0.0))
        rank = rank + beats
    return jnp.where(rank < k, 1.0, 0.0)


def _topk_mask_rows(score, k):
    row = lax.broadcasted_iota(jnp.int32, score.shape, 0)
    rank = jnp.zeros(score.shape, F32)
    for i in range(score.shape[0]):
        si = score[i:i + 1, :]
        beats = jnp.where(si > score, 1.0, jnp.where(si == score, jnp.where(row > i, 1.0, 0.0), 0.0))
        rank = rank + beats
    return jnp.where(rank < k, 1.0, 0.0)


def _params(*sem):
    return pltpu.CompilerParams(dimension_semantics=sem, vmem_limit_bytes=VMEM_LIMIT)


def _ada_body(c_ref, w_ref, b_ref, o_ref):
    c = _silu(c_ref[...]).astype(BF16)
    o_ref[...] = _dot(c, w_ref[...].astype(BF16)) + b_ref[...]


def _ada(c_all, w_ada, b_ada):
    n, d = c_all.shape
    width = w_ada.shape[1]
    tn = 1536
    return pl.pallas_call(
        _ada_body,
        grid=(width // tn,),
        in_specs=[pl.BlockSpec((n, d), lambda j: (0, 0)),
                  pl.BlockSpec((d, tn), lambda j: (0, j)),
                  pl.BlockSpec((1, tn), lambda j: (0, j))],
        out_specs=pl.BlockSpec((n, tn), lambda j: (0, j)),
        out_shape=jax.ShapeDtypeStruct((n, width), F32),
        compiler_params=_params("arbitrary"),
        name="ada",
    )(c_all, w_ada, b_ada.reshape(1, width))


def _inproj_body(with_cmp, x_ref, sh_ref, sc_ref, g1_ref, w_ref, gqc_ref, gks_ref, gkw_ref, bd_ref, *rest):
    if with_cmp:
        wkv_ref, gkc_ref = rest[:2]
        rest = rest[2:]
    (qt_ref, cmpt_ref, selt_ref, selk_ref, selvt_ref, wint_ref, wink_ref, winvt_ref,
     z_ref, xbc_ref, gd_ref, gdt_ref) = rest[:12]
    tm = x_ref.shape[0]
    h = _rms(x_ref[...]) * g1_ref[...]
    hb = (h * (1.0 + sc_ref[0]) + sh_ref[0]).astype(BF16)
    bd_kv = bd_ref[...]

    q3 = _dot(hb, w_ref[:, PK_Q:PK_CMP]).T.reshape(N_HEADS, HEAD_DIM, tm)
    q3 = q3 * lax.rsqrt(jnp.mean(q3 * q3, axis=1, keepdims=True) + EPS)
    qt_ref[...] = (q3.reshape(ATT_WIDTH, tm) * gqc_ref[...]).astype(BF16)

    kv = _dot(hb, w_ref[:, PK_CMP:PK_SEL])
    cmpt_ref[...] = kv.T
    if with_cmp:
        kc_ref = rest[12]
        pooled = (kv.reshape(tm // CMP_BLOCK, CMP_BLOCK, 2 * KV_WIDTH) * wkv_ref[...][None]).sum(axis=1)
        kc_ref[:, 0:KV_WIDTH] = _head_rms(pooled[:, :KV_WIDTH], bd_kv) * gkc_ref[...]
        kc_ref[:, KV_WIDTH:] = pooled[:, KV_WIDTH:]

    def normed_kv(lo, hi, gain_ref, t_ref, k_ref, vt_ref):
        kv = _dot(hb, w_ref[:, lo:hi])
        kn = _head_rms(kv[:, :KV_WIDTH], bd_kv) * gain_ref[...]
        k_ref[...] = kn.astype(BF16)
        kvt = jnp.concatenate([kn, kv[:, KV_WIDTH:]], axis=1).T
        t_ref[...] = kvt
        vt = kvt[KV_WIDTH:, :].astype(BF16)
        for j in range(tm // KEY_CHUNK):
            vt_ref[j] = vt[:, j * KEY_CHUNK:(j + 1) * KEY_CHUNK]

    normed_kv(PK_SEL, PK_WIN, gks_ref, selt_ref, selk_ref, selvt_ref)
    normed_kv(PK_WIN, PK_Z, gkw_ref, wint_ref, wink_ref, winvt_ref)

    z_ref[...] = _dot(hb, w_ref[:, PK_Z:PK_XBC])
    xbc_ref[...] = _dot(hb, w_ref[:, PK_XBC:PK_GD])
    gd = _dot(hb, w_ref[:, PK_GD:PK_WIDTH])
    gd_ref[...] = gd
    gdt_ref[...] = gd.T


def _in_proj(x2d, nbatch, shift3, scale3, mod_index, tm, wts, with_cmp):
    t, d = x2d.shape
    seq = t // nbatch
    per_seq = seq // tm
    rmod = shift3.shape[1]
    full = lambda a: pl.BlockSpec(a.shape, lambda i: (0,) * a.ndim)
    row = lambda w: pl.BlockSpec((tm, w), lambda i: (i, 0))
    fmaj = lambda w: pl.BlockSpec((None, w, tm), lambda i: (i // per_seq, 0, i % per_seq))
    chunks = pl.BlockSpec((tm // KEY_CHUNK, KV_WIDTH, KEY_CHUNK), lambda i: (i, 0, 0))
    mod = pl.BlockSpec((1, rmod, d), mod_index)
    ins = [x2d, shift3, scale3, wts["g1"], wts["w_in"], wts["gq_col"], wts["gks"], wts["gkw"], wts["bd_kv"]]
    in_specs = [row(d), mod, mod] + [full(a) for a in ins[3:]]
    if with_cmp:
        ins += [wts["wkv"], wts["gkc"]]
        in_specs += [full(wts["wkv"]), full(wts["gkc"])]
    sds = jax.ShapeDtypeStruct
    fm = lambda w, dt: sds((nbatch, w, seq), dt)
    ck = sds((t // KEY_CHUNK, KV_WIDTH, KEY_CHUNK), BF16)
    out_shape = [fm(ATT_WIDTH, BF16), fm(2 * KV_WIDTH, F32),
                 fm(2 * KV_WIDTH, F32), sds((t, KV_WIDTH), BF16), ck,
                 fm(2 * KV_WIDTH, F32), sds((t, KV_WIDTH), BF16), ck,
                 sds((t, SSM_WIDTH), F32), sds((t, CONV_DIM), F32), sds((t, LANES), F32), fm(LANES, F32)]
    out_specs = [fmaj(ATT_WIDTH), fmaj(2 * KV_WIDTH),
                 fmaj(2 * KV_WIDTH), row(KV_WIDTH), chunks,
                 fmaj(2 * KV_WIDTH), row(KV_WIDTH), chunks,
                 row(SSM_WIDTH), row(CONV_DIM), row(LANES), fmaj(LANES)]
    if with_cmp:
        out_shape.append(sds((t // CMP_BLOCK, 2 * KV_WIDTH), F32))
        out_specs.append(pl.BlockSpec((tm // CMP_BLOCK, 2 * KV_WIDTH), lambda i: (i, 0)))
    return pl.pallas_call(
        functools.partial(_inproj_body, with_cmp),
        grid=(t // tm,),
        in_specs=in_specs,
        out_specs=out_specs,
        out_shape=out_shape,
        compiler_params=_params("arbitrary"),
        name="in_proj",
    )(*ins)


def _attn_prompt_body(qt_ref, selk_ref, selvt_ref, wink_ref, winvt_ref, kc_ref, vct_ref, gdt_ref, o_ref, selm_scr,
                      *, tq, seq):
    i = pl.program_id(1)
    ncb = seq // CMP_BLOCK
    nblk = seq // SEL_BLOCK
    bpc = KEY_CHUNK // SEL_BLOCK
    wide = Q_PER_KV * tq
    qt = qt_ref[...]
    t_row = i * tq + lax.broadcasted_iota(jnp.int32, (1, tq), 1)
    cur_row = t_row // SEL_BLOCK
    key_i = lax.broadcasted_iota(jnp.int32, (KEY_CHUNK, tq), 0)
    gates = _sigmoid(gdt_ref[GD_GATE0:GD_GATE0 + 3 * N_HEADS, :])
    rep = lambda a: jnp.concatenate([a] * Q_PER_KV, axis=1)

    def flash(k_ref, vt_ref, qgts, c_lo, c_hi, masks_fn):
        last = seq // KEY_CHUNK - 1

        def step(j, carry):
            chunks = []
            for u in range(FLASH_UNROLL):
                c = c_lo + FLASH_UNROLL * j + u
                cc = jnp.minimum(c, last)
                off = pl.multiple_of(cc * KEY_CHUNK, KEY_CHUNK)
                scores = [_dot(k_ref[pl.ds(off, KEY_CHUNK), g * HEAD_DIM:(g + 1) * HEAD_DIM], qgts[g])
                          for g in range(N_KV)]
                chunks.append((c, cc, scores))
            state = list(carry)
            for c, cc, scores in chunks:
                masks = masks_fn(c, cc, c < c_hi)
                for g in range(N_KV):
                    m, l, acc = state[g]
                    s = jnp.where(rep(masks[g]), scores[g], NEG)
                    m_new = jnp.maximum(m, s.max(axis=0, keepdims=True))
                    alpha = jnp.exp(m - m_new)
                    e = jnp.exp(s - m_new)
                    pv = _dot(vt_ref[cc, g * HEAD_DIM:(g + 1) * HEAD_DIM, :], e.astype(BF16))
                    state[g] = (m_new, alpha * l + e.sum(axis=0, keepdims=True), alpha * acc + pv)
            return tuple(state)

        init = (jnp.full((1, wide), NEG, F32), jnp.zeros((1, wide), F32), jnp.zeros((HEAD_DIM, wide), F32))
        trips = (c_hi - c_lo + FLASH_UNROLL - 1) // FLASH_UNROLL
        res = lax.fori_loop(0, trips, step, (init,) * N_KV)
        return [acc / jnp.maximum(l, TINY) for _, l, acc in res]

    qgts, o_cmps = [], []
    for g in range(N_KV):
        qgt = jnp.concatenate(
            [qt[(g * Q_PER_KV + h) * HEAD_DIM:(g * Q_PER_KV + h + 1) * HEAD_DIM, :] for h in range(Q_PER_KV)], axis=1)
        qgts.append(qgt)

        kcg = kc_ref[:, g * HEAD_DIM:(g + 1) * HEAD_DIM]
        vctg = vct_ref[g * HEAD_DIM:(g + 1) * HEAD_DIM, :]
        r = lax.broadcasted_iota(jnp.int32, (ncb, tq), 0)
        cblk = jnp.where(r < nblk, 2 * r, 2 * (r - nblk) + 1)
        cmask = rep((cblk * CMP_BLOCK + CMP_BLOCK - 1) <= t_row)
        s = jnp.where(cmask, _dot(kcg, qgt), NEG)
        e = jnp.where(cmask, jnp.exp(s - s.max(axis=0, keepdims=True)), 0.0)
        p = e / jnp.maximum(e.sum(axis=0, keepdims=True), TINY)
        o_cmps.append(_dot(vctg, p.astype(BF16)))

        imp = sum(p[:, h * tq:(h + 1) * tq] for h in range(Q_PER_KV))
        imp = imp[:nblk] + imp[nblk:]
        jb = lax.broadcasted_iota(jnp.int32, (nblk, tq), 0)
        allowed = jb < cur_row
        forced = (jb == 0) | (jb == cur_row - 1)
        score = jnp.where(allowed, jnp.where(forced, BIG, imp), NEG)
        selm = jnp.where(allowed, _topk_mask_rows(score, min(N_SEL - 1, nblk)), 0.0)
        for c in range(seq // KEY_CHUNK):
            selm_scr[g, c, 0:bpc] = selm[c * bpc:(c + 1) * bpc]

    def key_pos(c, live):
        return c * KEY_CHUNK + key_i + jnp.where(live, 0, DEAD_SHIFT)

    def sel_masks(c, cc, live):
        kp = key_pos(c, live)
        cur_blk = ((kp // SEL_BLOCK) == cur_row) & (kp <= t_row)
        alive = jnp.where(live, 1.0, 0.0)
        out = []
        for g in range(N_KV):
            rows = selm_scr[g, cc] * alive
            picked = jnp.concatenate(
                [jnp.broadcast_to(rows[j:j + 1], (SEL_BLOCK, tq)) for j in range(bpc)], axis=0) > 0.5
            out.append(picked | cur_blk)
        return out

    def win_masks(c, cc, live):
        rel = t_row - key_pos(c, live)
        return [(rel >= 0) & (rel < WINDOW)] * N_KV

    o_sels = flash(selk_ref, selvt_ref, qgts, 0, i + 1, sel_masks)
    o_wins = flash(wink_ref, winvt_ref, qgts, jnp.maximum(i - WINDOW // KEY_CHUNK, 0), i + 1, win_masks)

    heads = []
    for g in range(N_KV):
        for h in range(Q_PER_KV):
            hh = g * Q_PER_KV + h
            sl = slice(h * tq, (h + 1) * tq)
            gate = lambda br: gates[br * N_HEADS + hh:br * N_HEADS + hh + 1, :]
            heads.append(gate(0) * o_cmps[g][:, sl] + gate(1) * o_sels[g][:, sl] + gate(2) * o_wins[g][:, sl])
    o_ref[...] = jnp.concatenate(heads, axis=0).T


def _attn_prompt(qt, selk, selvt, wink, winvt, kc, vct, gdt):
    b, _, s = qt.shape
    tq = KEY_CHUNK
    ncb = s // CMP_BLOCK
    nch = s // KEY_CHUNK
    tile_t = lambda w: pl.BlockSpec((None, w, tq), lambda bi, i: (bi, 0, i))
    whole = lambda shape: pl.BlockSpec((None,) + shape, lambda bi, i: (bi,) + (0,) * len(shape))
    return pl.pallas_call(
        functools.partial(_attn_prompt_body, tq=tq, seq=s),
        grid=(b, s // tq),
        in_specs=[tile_t(ATT_WIDTH), whole((s, KV_WIDTH)), whole((nch, KV_WIDTH, KEY_CHUNK)),
                  whole((s, KV_WIDTH)), whole((nch, KV_WIDTH, KEY_CHUNK)),
                  whole((ncb, KV_WIDTH)), whole((KV_WIDTH, ncb)), tile_t(LANES)],
        out_specs=pl.BlockSpec((None, tq, ATT_WIDTH), lambda bi, i: (bi, i, 0)),
        out_shape=jax.ShapeDtypeStruct((b, s, ATT_WIDTH), F32),
        scratch_shapes=[pltpu.VMEM((N_KV, nch, SUBLANES, tq), F32)],
        compiler_params=_params("arbitrary", "arbitrary"),
        name="attn_prompt",
    )(qt, selk, selvt, wink, winvt, kc, vct, gdt)


def _ssd_body(xbc_ref, z_ref, gd_ref, cprev_ref, h0_ref, cw_ref, cb_ref, dtb_ref, alog_ref, dsk_ref, gout_ref,
              selx_ref, hexp_ref, y_ref, hout_ref, cout_ref, h_scr, xe_scr, yoff_scr, *, cl, l_valid):
    c = pl.program_id(1)
    gn = SSM_GROUPS * SSM_STATE
    hpg = SSM_HEADS // SSM_GROUPS

    @pl.when(c == 0)
    def _():
        h_scr[...] = h0_ref[...]
        xe_scr[0:SUBLANES] = cprev_ref[...]

    xe_scr[SUBLANES:SUBLANES + cl] = xbc_ref[...]
    base = SUBLANES - (CONV_W - 1)
    xc = cb_ref[...]
    for w in range(CONV_W):
        xc = xc + cw_ref[w:w + 1, :] * xe_scr[base + w:base + w + cl, :]
    tail = xe_scr[l_valid:l_valid + SUBLANES, :]
    cout_ref[...] = tail
    xe_scr[0:SUBLANES] = tail
    xc = _silu(xc)
    xs = xc[:, :SSM_WIDTH]
    bmb = xc[:, SSM_WIDTH:SSM_WIDTH + gn].astype(BF16)
    cmb = xc[:, SSM_WIDTH + gn:].astype(BF16)

    dt = _softplus(gd_ref[...] + dtb_ref[...])
    if l_valid < cl:
        dt = jnp.where(lax.broadcasted_iota(jnp.int32, dt.shape, 0) < l_valid, dt, 0.0)
    da = dt * (-jnp.exp(alog_ref[...]))
    ri = lax.broadcasted_iota(jnp.int32, (cl, cl), 0)
    ci = lax.broadcasted_iota(jnp.int32, (cl, cl), 1)
    causal = ri >= ci
    acs = _dot_exact_lhs(jnp.where(causal, 1.0, 0.0).astype(BF16), da)
    if cl == LANES:
        acs_t = acs.T
        acs_row = lambda h: acs_t[h:h + 1, :]
    else:
        acs_rows = sum(_dot_nt(selx_ref[...], p) for p in _split3(acs))
        acs_row = lambda h: acs_rows[h * cl:(h + 1) * cl, :]
    acs_last = acs[cl - 1:cl, :]
    e_last = jnp.exp(acs_last)
    hexp = hexp_ref[...]
    dt_w = _dot_exact_rhs(dt, hexp)
    wend_w = _dot_exact_rhs(jnp.exp(acs_last - acs) * dt, hexp)
    eacs_w = _dot_exact_rhs(jnp.exp(acs), hexp)
    x_dt = (xs * dt_w).astype(BF16)
    x_we = (xs * wend_w).astype(BF16)

    for g in range(SSM_GROUPS):
        bg = bmb[:, g * SSM_STATE:(g + 1) * SSM_STATE]
        cg = cmb[:, g * SSM_STATE:(g + 1) * SSM_STATE]
        cb = _dot_nt(cg, bg)
        for hq in range(hpg):
            h = g * hpg + hq
            hs = slice(h * SSM_HEAD_DIM, (h + 1) * SSM_HEAD_DIM)
            seg = acs[:, h:h + 1] - acs_row(h)
            decay = jnp.where(causal, jnp.exp(jnp.where(causal, seg, 0.0)), 0.0)
            y_ref[:, hs] = _dot((cb * decay).astype(BF16), x_dt[:, hs])
            hst = h_scr[h]
            yoff_scr[:, hs] = _dot_nt(cg, hst.astype(BF16))
            h_scr[h] = e_last[:, h:h + 1] * hst + _dot_tn(x_we[:, hs], bg)

    y = y_ref[...] + yoff_scr[...] * eacs_w + dsk_ref[...] * xs
    yg = y * _silu(z_ref[...])
    y_ref[...] = _rms(yg) * gout_ref[...]
    hout_ref[...] = h_scr[...]


def _ssd(xbc3, z3, gd3, cprev, h0, wts, cl, l_valid):
    b, l, _ = xbc3.shape
    nc = l // cl
    tile = lambda w: pl.BlockSpec((None, cl, w), lambda bi, c: (bi, c, 0))
    full = lambda a: pl.BlockSpec(a.shape, lambda bi, c: (0,) * a.ndim)
    names = ["conv_w", "conv_b", "dt_bias", "a_log", "d_skip", "g_ssm"]
    selx = (np.arange(SSM_HEADS * cl)[:, None] // cl == np.arange(LANES)[None, :]).astype(np.float32)
    selx = jnp.asarray(selx, BF16)
    hexp = (np.arange(LANES)[:, None] == np.arange(SSM_WIDTH)[None, :] // SSM_HEAD_DIM).astype(np.float32)
    consts = [wts[n] for n in names] + [selx, jnp.asarray(hexp, BF16)]
    hshape = (SSM_HEADS, SSM_HEAD_DIM, SSM_STATE)
    return pl.pallas_call(
        functools.partial(_ssd_body, cl=cl, l_valid=l_valid),
        grid=(b, nc),
        in_specs=[tile(CONV_DIM), tile(SSM_WIDTH), tile(LANES),
                  pl.BlockSpec((None, SUBLANES, CONV_DIM), lambda bi, c: (bi, 0, 0)),
                  pl.BlockSpec((None,) + hshape, lambda bi, c: (bi, 0, 0, 0))] + [full(a) for a in consts],
        out_specs=[tile(SSM_WIDTH),
                   pl.BlockSpec((None,) + hshape, lambda bi, c: (bi, 0, 0, 0)),
                   pl.BlockSpec((None, SUBLANES, CONV_DIM), lambda bi, c: (bi, 0, 0))],
        out_shape=[jax.ShapeDtypeStruct((b, l, SSM_WIDTH), F32),
                   jax.ShapeDtypeStruct((b,) + hshape, F32),
                   jax.ShapeDtypeStruct((b, SUBLANES, CONV_DIM), F32)],
        scratch_shapes=[pltpu.VMEM(hshape, F32), pltpu.VMEM((SUBLANES + cl, CONV_DIM), F32),
                        pltpu.VMEM((cl, SSM_WIDTH), F32)],
        compiler_params=_params("arbitrary", "arbitrary"),
        name="ssd",
    )(xbc3, z3, gd3, cprev, h0, *consts)


def _post_body(att_ref, y_ref, x_ref, g1m_ref, sh2_ref, sc2_ref, gatt_ref, wo_ref, gn2_ref, wrh_ref, wrl_ref, br_ref,
               x1_ref, h2_ref, comb_ref):
    an = (_rms(att_ref[...]) * gatt_ref[...]).astype(BF16)
    mixed = _dot(an, wo_ref[0:ATT_WIDTH, :]) + _dot(y_ref[...].astype(BF16), wo_ref[ATT_WIDTH:, :])
    x1 = x_ref[...] + g1m_ref[0] * mixed
    x1_ref[...] = x1
    h2 = _rms(x1) * gn2_ref[...] * (1.0 + sc2_ref[0]) + sh2_ref[0]
    h2_ref[...] = h2.astype(BF16)

    hh, hl = _split2(h2)
    logit = _dot(hh, wrh_ref[...]) + _dot(hl, wrh_ref[...]) + _dot(hh, wrl_ref[...]) + br_ref[...]
    lane = lax.broadcasted_iota(jnp.int32, logit.shape, 1).astype(F32)
    first = lambda cond: jnp.min(jnp.where(cond, lane, 1e6), axis=-1, keepdims=True)
    top = lambda cond: jnp.max(jnp.where(cond, logit, NEG), axis=-1, keepdims=True)
    is_g = lane < N_EGROUPS
    mg = top(is_g)
    gsel = first(is_g & (logit == mg))
    p_top = 1.0 / jnp.sum(jnp.where(is_g, jnp.exp(logit - mg), 0.0), axis=-1, keepdims=True)
    lo = RT_E0 + E_PER_GROUP * gsel
    is_e = (lane >= lo) & (lane < lo + E_PER_GROUP)
    v1 = top(is_e)
    i1 = first(is_e & (logit == v1))
    is_e2 = is_e & (lane != i1)
    v2 = top(is_e2)
    i2 = first(is_e2 & (logit == v2))
    e2 = jnp.exp(v2 - v1)
    comb_ref[...] = (jnp.where(lane == i1, p_top / (1.0 + e2), 0.0)
                     + jnp.where(lane == i2, p_top * e2 / (1.0 + e2), 0.0))


def _post(att2d, y2d, x2d, gate1, shift2, scale2, mod_index, tm, wts):
    t, d = x2d.shape
    rmod = gate1.shape[1]
    row = lambda w: pl.BlockSpec((tm, w), lambda i: (i, 0))
    mod = pl.BlockSpec((1, rmod, d), mod_index)
    full = lambda a: pl.BlockSpec(a.shape, lambda i: (0,) * a.ndim)
    consts = [wts["g_att"], wts["w_out"], wts["g2"], wts["wr_hi"], wts["wr_lo"], wts["br"]]
    return pl.pallas_call(
        _post_body,
        grid=(t // tm,),
        in_specs=[row(ATT_WIDTH), row(SSM_WIDTH), row(d), mod, mod, mod] + [full(a) for a in consts],
        out_specs=[row(d), row(d), row(LANES)],
        out_shape=[jax.ShapeDtypeStruct((t, d), F32), jax.ShapeDtypeStruct((t, d), BF16),
                   jax.ShapeDtypeStruct((t, LANES), F32)],
        compiler_params=_params("arbitrary"),
        name="post",
    )(att2d, y2d, x2d, gate1, shift2, scale2, *consts)


def _moe_body(h_ref, wg_ref, wu_ref, wd_ref, comb_ref, x1_ref, g2m_ref, o_ref, acc_ref):
    e = pl.program_id(1)

    @pl.when(e == 0)
    def _():
        acc_ref[...] = jnp.zeros_like(acc_ref)

    h = h_ref[...]
    he = (_silu(_dot(h, wg_ref[...])) * _dot(h, wu_ref[...])).astype(BF16)
    comb = comb_ref[...]
    lane = lax.broadcasted_iota(jnp.int32, comb.shape, 1)
    cw = jnp.sum(jnp.where(lane == RT_E0 + e, comb, 0.0), axis=-1, keepdims=True)
    acc_ref[...] += cw * _dot(he, wd_ref[...])

    @pl.when(e == N_EXPERTS - 1)
    def _():
        o_ref[...] = x1_ref[...] + g2m_ref[0] * acc_ref[...]


def _moe(h2, comb, x1, gate2, mod_index, tm, wts):
    t, d = x1.shape
    rmod = gate2.shape[1]
    row = lambda w: pl.BlockSpec((tm, w), lambda i, e: (i, 0))
    return pl.pallas_call(
        _moe_body,
        grid=(t // tm, N_EXPERTS),
        in_specs=[row(d),
                  pl.BlockSpec((None, d, EXPERT_FF), lambda i, e: (e, 0, 0)),
                  pl.BlockSpec((None, d, EXPERT_FF), lambda i, e: (e, 0, 0)),
                  pl.BlockSpec((None, EXPERT_FF, d), lambda i, e: (e, 0, 0)),
                  row(LANES), row(d),
                  pl.BlockSpec((1, rmod, d), lambda i, e: mod_index(i))],
        out_specs=row(d),
        out_shape=jax.ShapeDtypeStruct((t, d), F32),
        scratch_shapes=[pltpu.VMEM((tm, d), F32)],
        compiler_params=_params("arbitrary", "arbitrary"),
        name="moe",
    )(h2, wts["w_gate"], wts["w_up"], wts["w_down"], comb, x1, gate2)


def _page_fetch(pt_ref, cache_ref, buf, sem, npages):
    b = pl.program_id(0)
    slot = lax.rem(b, 2)

    def start_all(bb, sl):
        def body(p, carry):
            pltpu.make_async_copy(cache_ref.at[pt_ref[bb, p]], buf.at[sl, p], sem.at[sl]).start()
            return carry
        lax.fori_loop(0, npages, body, 0)

    @pl.when(b == 0)
    def _():
        start_all(0, 0)

    @pl.when(b + 1 < pl.num_programs(0))
    def _():
        start_all(b + 1, 1 - slot)

    def wait_one(p, carry):
        pltpu.make_async_copy(cache_ref.at[0], buf.at[slot, p], sem.at[slot]).wait()
        return carry
    lax.fori_loop(0, npages, wait_one, 0)
    return slot


def _cmp_sample_body(pt_ref, cache_ref, qbd_ref, pw_all_ref, gkc_ref, bd_ref, pair_ref, rsum_ref, rexp_ref,
                     ocmp_ref, mask_ref, buf, sem, kc_scr, *, npages, past_len, tdec):
    slot = _page_fetch(pt_ref, cache_ref, buf, sem, npages)
    nblk = past_len // SEL_BLOCK
    ppi = SUBLANES // (PAGE_SIZE // CMP_BLOCK)
    kw = KV_WIDTH

    tiles = min(POOL_UNROLL, npages // ppi)

    def pool(i, carry):
        for u in range(tiles):
            kc8 = jnp.zeros((SUBLANES, kw), F32)
            vc8 = jnp.zeros((SUBLANES, kw), F32)
            for j in range(ppi):
                r1 = _dot_nt(pw_all_ref[j], buf[slot, (i * tiles + u) * ppi + j].astype(BF16))
                s8 = SUBLANES
                kc8 = kc8 + r1[0:s8, :kw] + r1[s8:2 * s8, :kw]
                vc8 = vc8 + r1[2 * s8:3 * s8, kw:] + r1[3 * s8:, kw:]
            rows = pl.ds(pl.multiple_of((i * tiles + u) * SUBLANES, SUBLANES), SUBLANES)
            kc_scr[rows, 0:kw] = kc8
            kc_scr[rows, kw:] = vc8
        return carry
    lax.fori_loop(0, npages // (ppi * tiles), pool, 0)

    kcn = _head_rms(kc_scr[:, 0:kw], bd_ref[...]) * gkc_ref[...]
    vc = kc_scr[:, kw:]
    s = _dot_nt(qbd_ref[...], kcn.astype(BF16))
    e = jnp.exp(s - s.max(axis=-1, keepdims=True))
    p = e / jnp.maximum(e.sum(axis=-1, keepdims=True), TINY)
    o = _dot(p.astype(BF16), vc.astype(BF16))
    nrow = o.shape[0]
    first_group = lax.broadcasted_iota(jnp.int32, (nrow, HEAD_DIM), 0) < nrow // N_KV
    ocmp_ref[...] = jnp.where(first_group, o[:, :HEAD_DIM], o[:, HEAD_DIM:])

    imp = _dot_exact_rhs(_dot_exact_lhs(rsum_ref[...], p), pair_ref[...])
    lane = lax.broadcasted_iota(jnp.int32, imp.shape, 1)
    tok = lax.rem(lax.broadcasted_iota(jnp.int32, imp.shape, 0), tdec)
    cur = (past_len + tok) // SEL_BLOCK
    allowed = (lane < cur) & (lane < nblk)
    forced = (lane == 0) | (lane == cur - 1)
    score = jnp.where(allowed, jnp.where(forced, BIG, imp), NEG)
    selm = jnp.where(allowed, _topk_mask_lanes(score, nblk, min(N_SEL - 1, nblk)), 0.0)
    mask_ref[...] = _dot(rexp_ref[...], selm.astype(BF16))


def _sel_sample_body(pt_ref, cache_ref, qbd_ref, mask_ref, eexp_ref, selnew_ref, winc_ref, winnew_ref, gt_ref,
                     ocmp_ref, o_ref, buf, sem, *, npages, past_len, tdec, ppc):
    slot = _page_fetch(pt_ref, cache_ref, buf, sem, npages)
    qbd = qbd_ref[...]
    nrow = qbd.shape[0]
    tok = lax.rem(lax.broadcasted_iota(jnp.int32, (nrow, 1), 0), tdec)
    kch = ppc * PAGE_SIZE
    kw = KV_WIDTH

    def attend(pieces):
        ss = [jnp.where(mask, s, NEG) for s, mask, _ in pieces]
        m = functools.reduce(jnp.maximum, [s.max(axis=-1, keepdims=True) for s in ss])
        es = [jnp.where(mask, jnp.exp(s - m), 0.0) for s, (_, mask, _) in zip(ss, pieces)]
        l = sum(e.sum(axis=-1, keepdims=True) for e in es)
        acc = sum(pv(e.astype(BF16)) for e, (_, _, pv) in zip(es, pieces))
        o = acc / jnp.maximum(l, TINY)
        first_group = lax.broadcasted_iota(jnp.int32, (nrow, HEAD_DIM), 0) < nrow // N_KV
        return jnp.where(first_group, o[:, :HEAD_DIM], o[:, HEAD_DIM:])

    def feature_major(kt, vt, mask):
        vtb = vt.astype(BF16)
        return _dot(qbd, kt.astype(BF16)), mask, lambda e: _dot_nt(e, vtb)

    def new_rows(new_ref):
        kv = new_ref[...]
        idx = lax.broadcasted_iota(jnp.int32, (nrow, kv.shape[0]), 1)
        v = kv[:, kw:].astype(BF16)
        return _dot_nt(qbd, kv[:, :kw].astype(BF16)), (idx <= tok) & (idx < tdec), lambda e: _dot(e, v)

    selm = mask_ref[...].astype(BF16)
    pieces = []
    for c in range(npages // ppc):
        pages = [buf[slot, c * ppc + j] for j in range(ppc)]
        kt = jnp.concatenate([pg[:kw] for pg in pages], axis=1)
        vt = jnp.concatenate([pg[kw:] for pg in pages], axis=1)
        pieces.append(feature_major(kt, vt, _dot(selm, eexp_ref[:, c * kch:(c + 1) * kch]) > 0.5))
    o_sel = attend(pieces + [new_rows(selnew_ref)])

    wbuf = winc_ref.shape[1]
    rel = wbuf + tok - lax.broadcasted_iota(jnp.int32, (nrow, wbuf), 1)
    o_win = attend([feature_major(winc_ref[0:kw, :], winc_ref[kw:, :], (rel >= 0) & (rel < WINDOW)),
                    new_rows(winnew_ref)])

    gates = _sigmoid(gt_ref[...])
    o_ref[...] = gates[:, 0:1] * ocmp_ref[...] + gates[:, 1:2] * o_sel + gates[:, 2:3] * o_win


def _attn_sample(qbd, gt, page_table, cache_cmp_t, cache_sel_t, cache_win_t, sel_new, win_new, wts, past_len, tdec):
    nb, nrow, _ = qbd.shape
    npages = page_table.shape[1]
    ncb = past_len // CMP_BLOCK
    per_b = lambda r, w: pl.BlockSpec((None, r, w), lambda b, pt: (b, 0, 0))
    full = lambda a: pl.BlockSpec(a.shape, lambda b, pt: (0,) * a.ndim)
    hbm = pl.BlockSpec(memory_space=pl.ANY)
    page_buf = pltpu.VMEM((2, npages, 2 * KV_WIDTH, PAGE_SIZE), F32)

    r = np.arange(nrow)
    grp = (r // tdec) // Q_PER_KV * tdec + r % tdec
    rsum = (np.arange(N_KV * tdec)[:, None] == grp[None, :]).astype(np.float32)
    pair = np.zeros((ncb, LANES), np.float32)
    pair[np.arange(ncb), np.arange(ncb) // (SEL_BLOCK // CMP_BLOCK)] = 1.0
    eexp = (np.arange(LANES)[:, None] == np.arange(past_len)[None, :] // SEL_BLOCK).astype(np.float32)
    consts1 = [wts["pw_all"], wts["gkc"], wts["bd_kv"], jnp.asarray(pair, BF16),
               jnp.asarray(rsum, BF16), jnp.asarray(rsum.T, BF16)]

    o_cmp, mask = pl.pallas_call(
        functools.partial(_cmp_sample_body, npages=npages, past_len=past_len, tdec=tdec),
        grid_spec=pltpu.PrefetchScalarGridSpec(
            num_scalar_prefetch=1, grid=(nb,),
            in_specs=[hbm, per_b(nrow, KV_WIDTH)] + [full(a) for a in consts1],
            out_specs=[per_b(nrow, HEAD_DIM), per_b(nrow, LANES)],
            scratch_shapes=[page_buf, pltpu.SemaphoreType.DMA((2,)), pltpu.VMEM((ncb, 2 * KV_WIDTH), F32)]),
        out_shape=[jax.ShapeDtypeStruct((nb, nrow, HEAD_DIM), F32), jax.ShapeDtypeStruct((nb, nrow, LANES), F32)],
        compiler_params=_params("arbitrary"),
        name="attn_sample_cmp",
    )(page_table, cache_cmp_t, qbd, *consts1)

    ppc = min(8, npages)
    eexp = jnp.asarray(eexp, BF16)
    return pl.pallas_call(
        functools.partial(_sel_sample_body, npages=npages, past_len=past_len, tdec=tdec, ppc=ppc),
        grid_spec=pltpu.PrefetchScalarGridSpec(
            num_scalar_prefetch=1, grid=(nb,),
            in_specs=[hbm, per_b(nrow, KV_WIDTH), per_b(nrow, LANES), full(eexp),
                      per_b(sel_new.shape[1], 2 * KV_WIDTH), per_b(2 * KV_WIDTH, cache_win_t.shape[2]),
                      per_b(win_new.shape[1], 2 * KV_WIDTH), per_b(nrow, SUBLANES), per_b(nrow, HEAD_DIM)],
            out_specs=per_b(nrow, HEAD_DIM),
            scratch_shapes=[page_buf, pltpu.SemaphoreType.DMA((2,))]),
        out_shape=jax.ShapeDtypeStruct((nb, nrow, HEAD_DIM), F32),
        compiler_params=_params("arbitrary"),
        name="attn_sample_sel",
    )(page_table, cache_sel_t, qbd, mask, eexp, sel_new, cache_win_t, win_new, gt, o_cmp)


def _pool_weights(w_pos_k, w_pos_v):
    bpp = PAGE_SIZE // CMP_BLOCK
    ppi = SUBLANES // bpp
    rows = np.arange(PAGE_SIZE)
    place = np.zeros((ppi, SUBLANES, PAGE_SIZE), np.float32)
    for j in range(ppi):
        place[j, j * bpp + rows // CMP_BLOCK, rows] = 1.0
    place = jnp.asarray(place)

    def hi_lo(w):
        full = place * jnp.tile(w, bpp)[None, None, :]
        hi = full.astype(BF16)
        return hi, (full - hi.astype(F32)).astype(BF16)

    k_hi, k_lo = hi_lo(w_pos_k)
    v_hi, v_lo = hi_lo(w_pos_v)
    return jnp.concatenate([k_hi, k_lo, v_hi, v_lo], axis=1)


def _prep_weights(l, g_norm1, g_norm2, w_in, g_q, g_k_cmp, g_k_sel, g_k_win, w_pos_k, w_pos_v, conv_w, conv_b,
                  dt_bias, a_log, d_skip, g_att_out, g_ssm_out, w_out, w_rg, b_rg, w_re, b_re, w_gate, w_up, w_down):
    w = w_in[l]
    o_gt = ATT_WIDTH + 6 * KV_WIDTH
    o_z = o_gt + 3 * N_HEADS
    o_xbc = o_z + SSM_WIDTH
    o_dt = o_xbc + CONV_DIM
    pad = jnp.zeros((D_MODEL, PK_WIDTH - PK_GD - SSM_HEADS - 3 * N_HEADS), F32)
    w_packed = jnp.concatenate([w[:, :o_gt], w[:, o_z:o_xbc], w[:, o_xbc:o_dt], w[:, o_dt:], w[:, o_gt:o_z], pad],
                               axis=1).astype(BF16)
    seg = np.arange(KV_WIDTH) // HEAD_DIM
    bd_kv = jnp.asarray((seg[:, None] == seg[None, :]).astype(np.float32), BF16)
    lane_pad = lambda v: jnp.pad(v.astype(F32), (0, LANES - v.shape[0])).reshape(1, LANES)
    wr = jnp.concatenate([w_rg[l], jnp.transpose(w_re[l], (1, 0, 2)).reshape(D_MODEL, N_EXPERTS)], axis=1)
    wr = jnp.pad(wr, ((0, 0), (0, LANES - wr.shape[1])))
    wr_hi = wr.astype(BF16)
    wr_lo = (wr - wr_hi.astype(F32)).astype(BF16)
    return {
        "g1": g_norm1[l].reshape(1, D_MODEL), "w_in": w_packed,
        "gq_col": (jnp.tile(g_q[l], N_HEADS) * SCALE).reshape(ATT_WIDTH, 1),
        "gks": jnp.tile(g_k_sel[l], N_KV).reshape(1, KV_WIDTH),
        "gkw": jnp.tile(g_k_win[l], N_KV).reshape(1, KV_WIDTH),
        "gkc": jnp.tile(g_k_cmp[l], N_KV).reshape(1, KV_WIDTH),
        "bd_kv": bd_kv,
        "wkv": jnp.concatenate([jnp.broadcast_to(w_pos_k[l][:, None], (CMP_BLOCK, KV_WIDTH)),
                                jnp.broadcast_to(w_pos_v[l][:, None], (CMP_BLOCK, KV_WIDTH))], axis=1),
        "pw_all": _pool_weights(w_pos_k[l], w_pos_v[l]),
        "conv_w": conv_w[l], "conv_b": conv_b[l].reshape(1, CONV_DIM),
        "dt_bias": lane_pad(dt_bias[l]), "a_log": lane_pad(a_log[l]), "d_skip": jnp.repeat(d_skip[l].astype(F32), SSM_HEAD_DIM).reshape(1, SSM_WIDTH),
        "g_ssm": g_ssm_out[l].reshape(1, SSM_WIDTH),
        "g_att": g_att_out[l].reshape(1, ATT_WIDTH), "w_out": w_out[l].astype(BF16),
        "g2": g_norm2[l].reshape(1, D_MODEL), "wr_hi": wr_hi, "wr_lo": wr_lo,
        "br": lane_pad(jnp.concatenate([b_rg[l], b_re[l].reshape(-1)])),
        "w_gate": w_gate[l].astype(BF16), "w_up": w_up[l].astype(BF16), "w_down": w_down[l].astype(BF16),
    }


def _finish(att2d, y2d, x2d, mods, mod_index, tm_post, tm_moe, moe_index, wts):
    x1, h2, comb = _post(att2d, y2d, x2d, mods[2], mods[3], mods[4], mod_index, tm_post, wts)
    return _moe(h2, comb, x1, mods[5], moe_index, tm_moe, wts)


def _token_major_cache(t):
    b, _, n = t.shape
    return jnp.transpose(t.reshape(b, 2, N_KV, HEAD_DIM, n), (0, 4, 1, 2, 3))[None]


def _feature_major_cache(c):
    n, rows = c.shape[:2]
    return jnp.transpose(c, (0, 2, 3, 4, 1)).reshape(n, 2 * KV_WIDTH, rows)


def kernel(x_prompt, x_sample, cache_cmp, cache_sel, cache_win, state_ssm, state_conv, page_table, c_prompt, c_sample,
           g_norm1, g_norm2, w_ada, b_ada, w_in, g_q, g_k_cmp, g_k_sel, g_k_win, w_pos_k, w_pos_v, conv_w, conv_b,
           dt_bias, a_log, d_skip, g_att_out, g_ssm_out, w_out, w_rg, b_rg, w_re, b_re, w_gate, w_up, w_down):
    depth = w_in.shape[0]
    assert depth == 1
    l = 0
    bp, seq, d = x_prompt.shape
    bs, tdec, _ = x_sample.shape
    npages = page_table.shape[1]
    past_len = npages * PAGE_SIZE
    tp, ts = bp * seq, bs * tdec

    wts = _prep_weights(l, g_norm1, g_norm2, w_in, g_q, g_k_cmp, g_k_sel, g_k_win, w_pos_k, w_pos_v, conv_w, conv_b,
                        dt_bias, a_log, d_skip, g_att_out, g_ssm_out, w_out, w_rg, b_rg, w_re, b_re, w_gate, w_up,
                        w_down)
    m_all = _ada(jnp.concatenate([c_prompt, c_sample], axis=0), w_ada[l], b_ada[l])
    mods_p = [m.reshape(bp, 1, d) for m in jnp.split(m_all[:bp], 6, axis=-1)]
    tm_s = min(512, ts)
    mods_s = [jnp.repeat(m, tdec, axis=0).reshape(ts // tm_s, tm_s, d) for m in jnp.split(m_all[bp:], 6, axis=-1)]

    tm = 512
    per_seq = seq // tm
    xp2 = x_prompt.reshape(tp, d)
    qt, cmpt_p, selt_p, selk, selvt, wint_p, wink, winvt, z, xbc, gd, gdt, kc = _in_proj(
        xp2, bp, mods_p[0], mods_p[1], lambda i: (i // per_seq, 0, 0), tm, wts, True)
    ncb = seq // CMP_BLOCK
    nch = seq // KEY_CHUNK
    r3 = lambda a, n: a.reshape(bp, n, a.shape[-1])
    kc_eo = kc.reshape(bp, ncb // 2, 2, 2 * KV_WIDTH).transpose(0, 2, 1, 3).reshape(bp, ncb, 2 * KV_WIDTH)
    kcmp = kc_eo[:, :, :KV_WIDTH].astype(BF16)
    vcmp_t = jnp.transpose(kc_eo[:, :, KV_WIDTH:], (0, 2, 1)).astype(BF16)
    chunked = lambda a: a.reshape(bp, nch, KV_WIDTH, KEY_CHUNK)
    att = _attn_prompt(qt, r3(selk, seq), chunked(selvt), r3(wink, seq), chunked(winvt), kcmp, vcmp_t, gdt)
    y_ssm, ssm_p, conv_p = _ssd(r3(xbc, seq), r3(z, seq), r3(gd, seq), jnp.zeros((bp, SUBLANES, CONV_DIM), F32),
                                jnp.zeros((bp, SSM_HEADS, SSM_HEAD_DIM, SSM_STATE), F32), wts, SSD_CHUNK, SSD_CHUNK)
    tm_moe = 1024
    y_prompt = _finish(att.reshape(tp, ATT_WIDTH), y_ssm.reshape(tp, SSM_WIDTH), xp2, mods_p,
                       lambda i: (i // per_seq, 0, 0), tm, tm_moe, lambda i: (i // (seq // tm_moe), 0, 0), wts)

    xs2 = x_sample.reshape(ts, d)
    qt_s, cmpt_s, selt_s, _, _, wint_s, _, _, z, xbc, gd, _ = _in_proj(
        xs2, 1, mods_s[0], mods_s[1], lambda i: (i, 0, 0), tm_s, wts, False)
    nrow = N_HEADS * tdec
    qh = qt_s[0].T.reshape(bs, tdec, N_HEADS, HEAD_DIM).transpose(0, 2, 1, 3).reshape(bs, nrow, HEAD_DIM)
    g0 = (jnp.arange(nrow) < nrow // N_KV)[None, :, None]
    zero = jnp.zeros_like(qh)
    qbd = jnp.concatenate([jnp.where(g0, qh, zero), jnp.where(g0, zero, qh)], axis=-1)
    gt = gd[:, GD_GATE0:GD_GATE0 + 3 * N_HEADS].reshape(bs, tdec, 3, N_HEADS).transpose(0, 3, 1, 2)
    gt = jnp.pad(gt.reshape(bs, nrow, 3), ((0, 0), (0, 0), (0, SUBLANES - 3)))
    rows_s = lambda t: t[0].T.reshape(bs, tdec, 2 * KV_WIDTH)
    sel_s, win_s = rows_s(selt_s), rows_s(wint_s)
    new_pad = 2 * SUBLANES - tdec
    padded = lambda a: jnp.pad(a, ((0, 0), (0, new_pad), (0, 0)))
    cwin_t = _feature_major_cache(cache_win[l])
    att_s = _attn_sample(qbd, gt, page_table, _feature_major_cache(cache_cmp[l]), _feature_major_cache(cache_sel[l]),
                         cwin_t, padded(sel_s), padded(win_s), wts, past_len, tdec)
    att_s = att_s.reshape(bs, N_HEADS, tdec, HEAD_DIM).transpose(0, 2, 1, 3).reshape(ts, ATT_WIDTH)

    lpad = SUBLANES - tdec
    pad_rows = lambda a: jnp.pad(a.reshape(bs, tdec, a.shape[-1]), ((0, 0), (0, lpad), (0, 0)))
    cprev = jnp.pad(state_conv[l], ((0, 0), (SUBLANES - (CONV_W - 1), 0), (0, 0)))
    y_s, ssm_s, conv_s = _ssd(pad_rows(xbc), pad_rows(z), pad_rows(gd), cprev, state_ssm[l].astype(F32), wts,
                              SUBLANES, tdec)
    y_s = y_s[:, :tdec].reshape(ts, SSM_WIDTH)
    y_sample = _finish(att_s, y_s, xs2, mods_s, lambda i: (i, 0, 0), tm_s, tm_s, lambda i: (i, 0, 0), wts)

    kv6 = lambda a: a.reshape(1, bs, tdec, 2, N_KV, HEAD_DIM)
    wkeep = min(WINDOW, seq)
    wbuf = cwin_t.shape[2]
    wkeep_s = min(WINDOW, wbuf + tdec)
    win_all_t = jnp.concatenate([cwin_t, jnp.transpose(win_s, (0, 2, 1))], axis=2)[:, :, wbuf + tdec - wkeep_s:]
    tail = lambda a: a[:, SUBLANES - (CONV_W - 1):][None]
    return (y_prompt.reshape(bp, seq, d), y_sample.reshape(bs, tdec, d),
            _token_major_cache(cmpt_p), kv6(rows_s(cmpt_s)), _token_major_cache(selt_p), kv6(sel_s),
            _token_major_cache(wint_p[:, :, seq - wkeep:]), _token_major_cache(win_all_t),
            ssm_p[None], ssm_s[None].astype(state_ssm.dtype), tail(conv_p), tail(conv_s))
```

```python
import functools

import numpy as np
import jax
import jax.numpy as jnp
from jax import lax
from jax.experimental import pallas as pl
from jax.experimental.pallas import tpu as pltpu

F32 = jnp.float32
BF16 = jnp.bfloat16

D_MODEL = 1024
PAGE_SIZE = 128
N_HEADS = 8
N_KV = 2
HEAD_DIM = 64
Q_PER_KV = N_HEADS // N_KV
ATT_WIDTH = N_HEADS * HEAD_DIM
KV_WIDTH = N_KV * HEAD_DIM
CMP_BLOCK = 32
SEL_BLOCK = 64
N_SEL = 16
WINDOW = 512
SCALE = HEAD_DIM ** -0.5
SSM_HEADS = 8
SSM_HEAD_DIM = 64
SSM_WIDTH = SSM_HEADS * SSM_HEAD_DIM
SSM_GROUPS = 2
SSM_STATE = 64
CONV_W = 4
SSD_CHUNK = 128
CONV_DIM = SSM_WIDTH + 2 * SSM_GROUPS * SSM_STATE
N_EGROUPS = 4
E_PER_GROUP = 4
N_EXPERTS = N_EGROUPS * E_PER_GROUP
EXPERT_FF = 256
EPS = 1e-6
NEG = -1e30
BIG = 1e9
TINY = 1e-30
LOG2E = 1.4426950408889634

LANES = 128
SUBLANES = 8
VMEM_LIMIT = 48 * 1024 * 1024
Q_TILE = 128
KEY_CHUNK = 128
FLASH_UNROLL = 2
DEAD_SHIFT = 1 << 24
SSD_SAMPLE_BATCH = 8
POOL_UNROLL = 16

PK_Q = 0
PK_CMP = 512
PK_SEL = 768
PK_WIN = 1024
PK_Z = 1280
PK_XBC = 1792
PK_GD = 2560
PK_WIDTH = 2688
GD_GATE0 = 8
RT_E0 = 4


def _dot(a, b):
    return jnp.dot(a, b, preferred_element_type=F32)


def _dot_nt(a, b):
    return lax.dot_general(a, b, (((1,), (1,)), ((), ())), preferred_element_type=F32)


def _dot_tn(a, b):
    return lax.dot_general(a, b, (((0,), (0,)), ((), ())), preferred_element_type=F32)


def _split2(x):
    hi = x.astype(BF16)
    lo = (x - hi.astype(F32)).astype(BF16)
    return hi, lo


def _split3(x):
    hi = x.astype(BF16)
    r = x - hi.astype(F32)
    mid = r.astype(BF16)
    lo = (r - mid.astype(F32)).astype(BF16)
    return hi, mid, lo


def _dot_exact_lhs(a_bf16, x):
    return sum(_dot(a_bf16, p) for p in _split3(x))


def _dot_exact_rhs(x, b_bf16):
    return sum(_dot(p, b_bf16) for p in _split3(x))


def _sigmoid(x):
    return 1.0 / (1.0 + jnp.exp(-x))


def _silu(x):
    return x * _sigmoid(x)


def _softplus(x):
    return jnp.maximum(x, 0.0) + jnp.log1p(jnp.exp(-jnp.abs(x)))


def _rms(x):
    return x * lax.rsqrt(jnp.mean(x * x, axis=-1, keepdims=True) + EPS)


def _head_rms(x, bd):
    hi, lo = _split2(x * x)
    ms = (_dot(hi, bd) + _dot(lo, bd)) * (1.0 / HEAD_DIM)
    return x * lax.rsqrt(ms + EPS)


def _topk_mask_lanes(score, nblk, k):
    lane = lax.broadcasted_iota(jnp.int32, score.shape, 1)
    rank = jnp.zeros(score.shape, F32)
    for i in range(nblk):
        si = score[:, i:i + 1]
        beats = jnp.where(si > score, 1.0, jnp.where(si == score, jnp.where(lane > i, 1.0, 0.0), 0.0))
        rank = rank + beats
    return jnp.where(rank < k, 1.0, 0.0)


def _topk_mask_rows(score, k):
    row = lax.broadcasted_iota(jnp.int32, score.shape, 0)
    rank = jnp.zeros(score.shape, F32)
    for i in range(score.shape[0]):
        si = score[i:i + 1, :]
        beats = jnp.where(si > score, 1.0, jnp.where(si == score, jnp.where(row > i, 1.0, 0.0), 0.0))
        rank = rank + beats
    return jnp.where(rank < k, 1.0, 0.0)


def _params(*sem):
    return pltpu.CompilerParams(dimension_semantics=sem, vmem_limit_bytes=VMEM_LIMIT)


def _ada_body(c_ref, w_ref, b_ref, o_ref):
    c = _silu(c_ref[...]).astype(BF16)
    o_ref[...] = _dot(c, w_ref[...].astype(BF16)) + b_ref[...]


def _ada(c_all, w_ada, b_ada):
    n, d = c_all.shape
    width = w_ada.shape[1]
    tn = 1536
    return pl.pallas_call(
        _ada_body,
        grid=(width // tn,),
        in_specs=[pl.BlockSpec((n, d), lambda j: (0, 0)),
                  pl.BlockSpec((d, tn), lambda j: (0, j)),
                  pl.BlockSpec((1, tn), lambda j: (0, j))],
        out_specs=pl.BlockSpec((n, tn), lambda j: (0, j)),
        out_shape=jax.ShapeDtypeStruct((n, width), F32),
        compiler_params=_params("arbitrary"),
        name="ada",
    )(c_all, w_ada, b_ada.reshape(1, width))


def _inproj_body(with_cmp, x_ref, sh_ref, sc_ref, g1_ref, w_ref, gqc_ref, gks_ref, gkw_ref, bd_ref, *rest):
    if with_cmp:
        wkv_ref, gkc_ref = rest[:2]
        rest = rest[2:]
    (qt_ref, cmpt_ref, selt_ref, selk_ref, selvt_ref, wint_ref, wink_ref, winvt_ref,
     z_ref, xbc_ref, gd_ref, gdt_ref) = rest[:12]
    tm = x_ref.shape[0]
    h = _rms(x_ref[...]) * g1_ref[...]
    hb = (h * (1.0 + sc_ref[0]) + sh_ref[0]).astype(BF16)
    bd_kv = bd_ref[...]

    q3 = _dot(hb, w_ref[:, PK_Q:PK_CMP]).T.reshape(N_HEADS, HEAD_DIM, tm)
    q3 = q3 * lax.rsqrt(jnp.mean(q3 * q3, axis=1, keepdims=True) + EPS)
    qt_ref[...] = (q3.reshape(ATT_WIDTH, tm) * gqc_ref[...]).astype(BF16)

    kv = _dot(hb, w_ref[:, PK_CMP:PK_SEL])
    cmpt_ref[...] = kv.T
    if with_cmp:
        kc_ref = rest[12]
        pooled = (kv.reshape(tm // CMP_BLOCK, CMP_BLOCK, 2 * KV_WIDTH) * wkv_ref[...][None]).sum(axis=1)
        kc_ref[:, 0:KV_WIDTH] = _head_rms(pooled[:, :KV_WIDTH], bd_kv) * gkc_ref[...]
        kc_ref[:, KV_WIDTH:] = pooled[:, KV_WIDTH:]

    def normed_kv(lo, hi, gain_ref, t_ref, k_ref, vt_ref):
        kv = _dot(hb, w_ref[:, lo:hi])
        kn = _head_rms(kv[:, :KV_WIDTH], bd_kv) * gain_ref[...]
        k_ref[...] = kn.astype(BF16)
        kvt = jnp.concatenate([kn, kv[:, KV_WIDTH:]], axis=1).T
        t_ref[...] = kvt
        vt = kvt[KV_WIDTH:, :].astype(BF16)
        for j in range(tm // KEY_CHUNK):
            vt_ref[j] = vt[:, j * KEY_CHUNK:(j + 1) * KEY_CHUNK]

    normed_kv(PK_SEL, PK_WIN, gks_ref, selt_ref, selk_ref, selvt_ref)
    normed_kv(PK_WIN, PK_Z, gkw_ref, wint_ref, wink_ref, winvt_ref)

    z_ref[...] = _dot(hb, w_ref[:, PK_Z:PK_XBC])
    xbc_ref[...] = _dot(hb, w_ref[:, PK_XBC:PK_GD])
    gd = _dot(hb, w_ref[:, PK_GD:PK_WIDTH])
    gd_ref[...] = gd
    gdt_ref[...] = gd.T


def _in_proj(x2d, nbatch, shift3, scale3, mod_index, tm, wts, with_cmp):
    t, d = x2d.shape
    seq = t // nbatch
    per_seq = seq // tm
    rmod = shift3.shape[1]
    full = lambda a: pl.BlockSpec(a.shape, lambda i: (0,) * a.ndim)
    row = lambda w: pl.BlockSpec((tm, w), lambda i: (i, 0))
    fmaj = lambda w: pl.BlockSpec((None, w, tm), lambda i: (i // per_seq, 0, i % per_seq))
    chunks = pl.BlockSpec((tm // KEY_CHUNK, KV_WIDTH, KEY_CHUNK), lambda i: (i, 0, 0))
    mod = pl.BlockSpec((1, rmod, d), mod_index)
    ins = [x2d, shift3, scale3, wts["g1"], wts["w_in"], wts["gq_col"], wts["gks"], wts["gkw"], wts["bd_kv"]]
    in_specs = [row(d), mod, mod] + [full(a) for a in ins[3:]]
    if with_cmp:
        ins += [wts["wkv"], wts["gkc"]]
        in_specs += [full(wts["wkv"]), full(wts["gkc"])]
    sds = jax.ShapeDtypeStruct
    fm = lambda w, dt: sds((nbatch, w, seq), dt)
    ck = sds((t // KEY_CHUNK, KV_WIDTH, KEY_CHUNK), BF16)
    out_shape = [fm(ATT_WIDTH, BF16), fm(2 * KV_WIDTH, F32),
                 fm(2 * KV_WIDTH, F32), sds((t, KV_WIDTH), BF16), ck,
                 fm(2 * KV_WIDTH, F32), sds((t, KV_WIDTH), BF16), ck,
                 sds((t, SSM_WIDTH), F32), sds((t, CONV_DIM), F32), sds((t, LANES), F32), fm(LANES, F32)]
    out_specs = [fmaj(ATT_WIDTH), fmaj(2 * KV_WIDTH),
                 fmaj(2 * KV_WIDTH), row(KV_WIDTH), chunks,
                 fmaj(2 * KV_WIDTH), row(KV_WIDTH), chunks,
                 row(SSM_WIDTH), row(CONV_DIM), row(LANES), fmaj(LANES)]
    if with_cmp:
        out_shape.append(sds((t // CMP_BLOCK, 2 * KV_WIDTH), F32))
        out_specs.append(pl.BlockSpec((tm // CMP_BLOCK, 2 * KV_WIDTH), lambda i: (i, 0)))
    return pl.pallas_call(
        functools.partial(_inproj_body, with_cmp),
        grid=(t // tm,),
        in_specs=in_specs,
        out_specs=out_specs,
        out_shape=out_shape,
        compiler_params=_params("arbitrary"),
        name="in_proj",
    )(*ins)


def _attn_prompt_body(qt_ref, selk_ref, selvt_ref, wink_ref, winvt_ref, kc_ref, vct_ref, gdt_ref, o_ref, selm_scr,
                      *, tq, seq):
    i = pl.program_id(1)
    ncb = seq // CMP_BLOCK
    nblk = seq // SEL_BLOCK
    bpc = KEY_CHUNK // SEL_BLOCK
    wide = Q_PER_KV * tq
    qt = qt_ref[...]
    t_row = i * tq + lax.broadcasted_iota(jnp.int32, (1, tq), 1)
    cur_row = t_row // SEL_BLOCK
    key_i = lax.broadcasted_iota(jnp.int32, (KEY_CHUNK, tq), 0)
    gates = _sigmoid(gdt_ref[GD_GATE0:GD_GATE0 + 3 * N_HEADS, :])
    rep = lambda a: jnp.concatenate([a] * Q_PER_KV, axis=1)

    def flash(k_ref, vt_ref, qgts, c_lo, c_hi, masks_fn, groups):
        last = seq // KEY_CHUNK - 1

        def step(j, carry):
            chunks = []
            for u in range(FLASH_UNROLL):
                c = c_lo + FLASH_UNROLL * j + u
                cc = jnp.minimum(c, last)
                off = pl.multiple_of(cc * KEY_CHUNK, KEY_CHUNK)
                scores = {g: _dot(k_ref[pl.ds(off, KEY_CHUNK), g * HEAD_DIM:(g + 1) * HEAD_DIM], qgts[g])
                          for g in groups}
                chunks.append((c, cc, scores))
            state = dict(zip(groups, carry))
            for c, cc, scores in chunks:
                masks = masks_fn(c, cc, c < c_hi)
                for g in groups:
                    m, l, acc = state[g]
                    s = jnp.where(rep(masks[g]), scores[g], NEG)
                    m_new = jnp.maximum(m, s.max(axis=0, keepdims=True))
                    alpha = jnp.exp2(m - m_new)
                    e = jnp.exp2(s - m_new)
                    pv = _dot(vt_ref[cc, g * HEAD_DIM:(g + 1) * HEAD_DIM, :], e.astype(BF16))
                    state[g] = (m_new, alpha * l + e.sum(axis=0, keepdims=True), alpha * acc + pv)
            return tuple(state[g] for g in groups)

        init = (jnp.full((1, wide), NEG, F32), jnp.zeros((1, wide), F32), jnp.zeros((HEAD_DIM, wide), F32))
        trips = (c_hi - c_lo + FLASH_UNROLL - 1) // FLASH_UNROLL
        res = lax.fori_loop(0, trips, step, (init,) * len(groups))
        return [acc / jnp.maximum(l, TINY) for _, l, acc in res]

    qgts, o_cmps = [], []
    for g in range(N_KV):
        qgt = jnp.concatenate(
            [qt[(g * Q_PER_KV + h) * HEAD_DIM:(g * Q_PER_KV + h + 1) * HEAD_DIM, :] for h in range(Q_PER_KV)], axis=1)
        qgts.append(qgt)

        kcg = kc_ref[:, g * HEAD_DIM:(g + 1) * HEAD_DIM]
        vctg = vct_ref[g * HEAD_DIM:(g + 1) * HEAD_DIM, :]
        r = lax.broadcasted_iota(jnp.int32, (ncb, tq), 0)
        cblk = jnp.where(r < nblk, 2 * r, 2 * (r - nblk) + 1)
        cmask = rep((cblk * CMP_BLOCK + CMP_BLOCK - 1) <= t_row)
        s = jnp.where(cmask, _dot(kcg, qgt), NEG)
        e = jnp.where(cmask, jnp.exp2(s - s.max(axis=0, keepdims=True)), 0.0)
        p = e / jnp.maximum(e.sum(axis=0, keepdims=True), TINY)
        o_cmps.append(_dot(vctg, p.astype(BF16)))

        imp = sum(p[:, h * tq:(h + 1) * tq] for h in range(Q_PER_KV))
        imp = imp[:nblk] + imp[nblk:]
        jb = lax.broadcasted_iota(jnp.int32, (nblk, tq), 0)
        allowed = jb < cur_row
        forced = (jb == 0) | (jb == cur_row - 1)
        score = jnp.where(allowed, jnp.where(forced, BIG, imp), NEG)
        selm = jnp.where(allowed, _topk_mask_rows(score, min(N_SEL - 1, nblk)), 0.0)
        for c in range(seq // KEY_CHUNK):
            selm_scr[g, c, 0:bpc] = selm[c * bpc:(c + 1) * bpc]

    def key_pos(c, live):
        return c * KEY_CHUNK + key_i + jnp.where(live, 0, DEAD_SHIFT)

    def sel_masks(c, cc, live):
        kp = key_pos(c, live)
        cur_blk = ((kp // SEL_BLOCK) == cur_row) & (kp <= t_row)
        alive = jnp.where(live, 1.0, 0.0)
        out = []
        for g in range(N_KV):
            rows = selm_scr[g, cc] * alive
            picked = jnp.concatenate(
                [jnp.broadcast_to(rows[j:j + 1], (SEL_BLOCK, tq)) for j in range(bpc)], axis=0) > 0.5
            out.append(picked | cur_blk)
        return out

    def win_masks(c, cc, live):
        rel = t_row - key_pos(c, live)
        return [(rel >= 0) & (rel < WINDOW)] * N_KV

    c_hi = ((i + 1) * tq + KEY_CHUNK - 1) // KEY_CHUNK
    c_win = jnp.maximum(i * tq - WINDOW + 1, 0) // KEY_CHUNK
    both = tuple(range(N_KV))
    o_sels = flash(selk_ref, selvt_ref, qgts, 0, c_hi, sel_masks, both)
    o_wins = flash(wink_ref, winvt_ref, qgts, c_win, c_hi, win_masks, both)

    heads = []
    for g in range(N_KV):
        for h in range(Q_PER_KV):
            hh = g * Q_PER_KV + h
            sl = slice(h * tq, (h + 1) * tq)
            gate = lambda br: gates[br * N_HEADS + hh:br * N_HEADS + hh + 1, :]
            heads.append(gate(0) * o_cmps[g][:, sl] + gate(1) * o_sels[g][:, sl] + gate(2) * o_wins[g][:, sl])
    o_ref[...] = jnp.concatenate(heads, axis=0).T


def _attn_prompt(qt, selk, selvt, wink, winvt, kc, vct, gdt):
    b, _, s = qt.shape
    tq = Q_TILE
    ncb = s // CMP_BLOCK
    nch = s // KEY_CHUNK
    tile_t = lambda w: pl.BlockSpec((None, w, tq), lambda bi, i: (bi, 0, i))
    whole = lambda shape: pl.BlockSpec((None,) + shape, lambda bi, i: (bi,) + (0,) * len(shape))
    return pl.pallas_call(
        functools.partial(_attn_prompt_body, tq=tq, seq=s),
        grid=(b, s // tq),
        in_specs=[tile_t(ATT_WIDTH), whole((s, KV_WIDTH)), whole((nch, KV_WIDTH, KEY_CHUNK)),
                  whole((s, KV_WIDTH)), whole((nch, KV_WIDTH, KEY_CHUNK)),
                  whole((ncb, KV_WIDTH)), whole((KV_WIDTH, ncb)), tile_t(LANES)],
        out_specs=pl.BlockSpec((None, tq, ATT_WIDTH), lambda bi, i: (bi, i, 0)),
        out_shape=jax.ShapeDtypeStruct((b, s, ATT_WIDTH), F32),
        scratch_shapes=[pltpu.VMEM((N_KV, nch, SUBLANES, tq), F32)],
        compiler_params=_params("arbitrary", "arbitrary"),
        name="attn_prompt",
    )(qt, selk, selvt, wink, winvt, kc, vct, gdt)


def _ssd_body(xbc_ref, z_ref, gd_ref, cprev_ref, h0_ref, *rest, cl, l_valid):
    consts, outs_scratch = rest[:8], rest[8:]
    for bi in range(xbc_ref.shape[0]):
        per_b = [r.at[bi] for r in (xbc_ref, z_ref, gd_ref, cprev_ref, h0_ref)]
        _ssd_one(*per_b, *consts, *[r.at[bi] for r in outs_scratch], cl=cl, l_valid=l_valid)


def _ssd_one(xbc_ref, z_ref, gd_ref, cprev_ref, h0_ref, cw_ref, cb_ref, dtb_ref, alog_ref, dsk_ref, gout_ref,
             selx_ref, hexp_ref, y_ref, hout_ref, cout_ref, h_scr, xe_scr, yoff_scr, *, cl, l_valid):
    c = pl.program_id(1)
    gn = SSM_GROUPS * SSM_STATE
    hpg = SSM_HEADS // SSM_GROUPS

    @pl.when(c == 0)
    def _():
        h_scr[...] = h0_ref[...]
        xe_scr[0:SUBLANES] = cprev_ref[...]

    xe_scr[SUBLANES:SUBLANES + cl] = xbc_ref[...]
    base = SUBLANES - (CONV_W - 1)
    xc = cb_ref[...]
    for w in range(CONV_W):
        xc = xc + cw_ref[w:w + 1, :] * xe_scr[base + w:base + w + cl, :]
    tail = xe_scr[l_valid:l_valid + SUBLANES, :]
    cout_ref[...] = tail
    xe_scr[0:SUBLANES] = tail
    xc = _silu(xc)
    xs = xc[:, :SSM_WIDTH]
    bmb = xc[:, SSM_WIDTH:SSM_WIDTH + gn].astype(BF16)
    cmb = xc[:, SSM_WIDTH + gn:].astype(BF16)

    dt = _softplus(gd_ref[...] + dtb_ref[...])
    if l_valid < cl:
        dt = jnp.where(lax.broadcasted_iota(jnp.int32, dt.shape, 0) < l_valid, dt, 0.0)
    da = dt * (-jnp.exp(alog_ref[...]))
    ri = lax.broadcasted_iota(jnp.int32, (cl, cl), 0)
    ci = lax.broadcasted_iota(jnp.int32, (cl, cl), 1)
    causal = ri >= ci
    acs = _dot_exact_lhs(jnp.where(causal, 1.0, 0.0).astype(BF16), da)
    if cl == LANES:
        acs_t = acs.T
        acs_row = lambda h: acs_t[h:h + 1, :]
    else:
        acs_rows = sum(_dot_nt(selx_ref[...], p) for p in _split3(acs))
        acs_row = lambda h: acs_rows[h * cl:(h + 1) * cl, :]
    acs_last = acs[cl - 1:cl, :]
    e_last = jnp.exp(acs_last)
    hexp = hexp_ref[...]
    dt_w = _dot_exact_rhs(dt, hexp)
    wend_w = _dot_exact_rhs(jnp.exp(acs_last - acs) * dt, hexp)
    eacs_w = _dot_exact_rhs(jnp.exp(acs), hexp)
    x_dt = (xs * dt_w).astype(BF16)
    x_we = (xs * wend_w).astype(BF16)

    for g in range(SSM_GROUPS):
        bg = bmb[:, g * SSM_STATE:(g + 1) * SSM_STATE]
        cg = cmb[:, g * SSM_STATE:(g + 1) * SSM_STATE]
        cb = _dot_nt(cg, bg)
        for hq in range(hpg):
            h = g * hpg + hq
            hs = slice(h * SSM_HEAD_DIM, (h + 1) * SSM_HEAD_DIM)
            seg = acs[:, h:h + 1] - acs_row(h)
            decay = jnp.where(causal, jnp.exp(jnp.where(causal, seg, 0.0)), 0.0)
            y_ref[:, hs] = _dot((cb * decay).astype(BF16), x_dt[:, hs])
            hst = h_scr[h]
            yoff_scr[:, hs] = _dot_nt(cg, hst.astype(BF16))
            h_scr[h] = e_last[:, h:h + 1] * hst + _dot_tn(x_we[:, hs], bg)

    y = y_ref[...] + yoff_scr[...] * eacs_w + dsk_ref[...] * xs
    yg = y * _silu(z_ref[...])
    y_ref[...] = _rms(yg) * gout_ref[...]
    hout_ref[...] = h_scr[...]


def _ssd(xbc3, z3, gd3, cprev, h0, wts, cl, l_valid, nbs):
    b, l, _ = xbc3.shape
    nc = l // cl
    tile = lambda w: pl.BlockSpec((nbs, cl, w), lambda bi, c: (bi, c, 0))
    full = lambda a: pl.BlockSpec(a.shape, lambda bi, c: (0,) * a.ndim)
    names = ["conv_w", "conv_b", "dt_bias", "a_log", "d_skip", "g_ssm"]
    selx = (np.arange(SSM_HEADS * cl)[:, None] // cl == np.arange(LANES)[None, :]).astype(np.float32)
    selx = jnp.asarray(selx, BF16)
    hexp = (np.arange(LANES)[:, None] == np.arange(SSM_WIDTH)[None, :] // SSM_HEAD_DIM).astype(np.float32)
    consts = [wts[n] for n in names] + [selx, jnp.asarray(hexp, BF16)]
    hshape = (SSM_HEADS, SSM_HEAD_DIM, SSM_STATE)
    return pl.pallas_call(
        functools.partial(_ssd_body, cl=cl, l_valid=l_valid),
        grid=(b // nbs, nc),
        in_specs=[tile(CONV_DIM), tile(SSM_WIDTH), tile(LANES),
                  pl.BlockSpec((nbs, SUBLANES, CONV_DIM), lambda bi, c: (bi, 0, 0)),
                  pl.BlockSpec((nbs,) + hshape, lambda bi, c: (bi, 0, 0, 0))] + [full(a) for a in consts],
        out_specs=[tile(SSM_WIDTH),
                   pl.BlockSpec((nbs,) + hshape, lambda bi, c: (bi, 0, 0, 0)),
                   pl.BlockSpec((nbs, SUBLANES, CONV_DIM), lambda bi, c: (bi, 0, 0))],
        out_shape=[jax.ShapeDtypeStruct((b, l, SSM_WIDTH), F32),
                   jax.ShapeDtypeStruct((b,) + hshape, F32),
                   jax.ShapeDtypeStruct((b, SUBLANES, CONV_DIM), F32)],
        scratch_shapes=[pltpu.VMEM((nbs,) + hshape, F32), pltpu.VMEM((nbs, SUBLANES + cl, CONV_DIM), F32),
                        pltpu.VMEM((nbs, cl, SSM_WIDTH), F32)],
        compiler_params=_params("arbitrary", "arbitrary"),
        name="ssd",
    )(xbc3, z3, gd3, cprev, h0, *consts)


def _post_body(att_ref, y_ref, x_ref, g1m_ref, sh2_ref, sc2_ref, gatt_ref, wo_ref, gn2_ref, wrh_ref, wrl_ref, br_ref,
               x1_ref, h2_ref, comb_ref):
    an = (_rms(att_ref[...]) * gatt_ref[...]).astype(BF16)
    mixed = _dot(an, wo_ref[0:ATT_WIDTH, :]) + _dot(y_ref[...].astype(BF16), wo_ref[ATT_WIDTH:, :])
    x1 = x_ref[...] + g1m_ref[0] * mixed
    x1_ref[...] = x1
    h2 = _rms(x1) * gn2_ref[...] * (1.0 + sc2_ref[0]) + sh2_ref[0]
    h2_ref[...] = h2.astype(BF16)

    hh, hl = _split2(h2)
    logit = _dot(hh, wrh_ref[...]) + _dot(hl, wrh_ref[...]) + _dot(hh, wrl_ref[...]) + br_ref[...]
    lane = lax.broadcasted_iota(jnp.int32, logit.shape, 1).astype(F32)
    first = lambda cond: jnp.min(jnp.where(cond, lane, 1e6), axis=-1, keepdims=True)
    top = lambda cond: jnp.max(jnp.where(cond, logit, NEG), axis=-1, keepdims=True)
    is_g = lane < N_EGROUPS
    mg = top(is_g)
    gsel = first(is_g & (logit == mg))
    p_top = 1.0 / jnp.sum(jnp.where(is_g, jnp.exp(logit - mg), 0.0), axis=-1, keepdims=True)
    lo = RT_E0 + E_PER_GROUP * gsel
    is_e = (lane >= lo) & (lane < lo + E_PER_GROUP)
    v1 = top(is_e)
    i1 = first(is_e & (logit == v1))
    is_e2 = is_e & (lane != i1)
    v2 = top(is_e2)
    i2 = first(is_e2 & (logit == v2))
    e2 = jnp.exp(v2 - v1)
    comb_ref[...] = (jnp.where(lane == i1, p_top / (1.0 + e2), 0.0)
                     + jnp.where(lane == i2, p_top * e2 / (1.0 + e2), 0.0))


def _post(att2d, y2d, x2d, gate1, shift2, scale2, mod_index, tm, wts):
    t, d = x2d.shape
    rmod = gate1.shape[1]
    row = lambda w: pl.BlockSpec((tm, w), lambda i: (i, 0))
    mod = pl.BlockSpec((1, rmod, d), mod_index)
    full = lambda a: pl.BlockSpec(a.shape, lambda i: (0,) * a.ndim)
    consts = [wts["g_att"], wts["w_out"], wts["g2"], wts["wr_hi"], wts["wr_lo"], wts["br"]]
    return pl.pallas_call(
        _post_body,
        grid=(t // tm,),
        in_specs=[row(ATT_WIDTH), row(SSM_WIDTH), row(d), mod, mod, mod] + [full(a) for a in consts],
        out_specs=[row(d), row(d), row(LANES)],
        out_shape=[jax.ShapeDtypeStruct((t, d), F32), jax.ShapeDtypeStruct((t, d), BF16),
                   jax.ShapeDtypeStruct((t, LANES), F32)],
        compiler_params=_params("arbitrary"),
        name="post",
    )(att2d, y2d, x2d, gate1, shift2, scale2, *consts)


def _moe_body(h_ref, wg_ref, wu_ref, wd_ref, comb_ref, x1_ref, g2m_ref, o_ref, acc_ref):
    g = pl.program_id(1)

    @pl.when(g == 0)
    def _():
        acc_ref[...] = jnp.zeros_like(acc_ref)

    h = h_ref[...]
    tm = h.shape[0]
    comb = comb_ref[...]
    lane = lax.broadcasted_iota(jnp.int32, comb.shape, 1)
    first = RT_E0 + E_PER_GROUP * g
    cw = jnp.concatenate(
        [jnp.broadcast_to(jnp.sum(jnp.where(lane == first + j, comb, 0.0), axis=-1, keepdims=True), (tm, EXPERT_FF))
         for j in range(E_PER_GROUP)], axis=1)
    he = (_silu(_dot(h, wg_ref[...])) * _dot(h, wu_ref[...]) * cw).astype(BF16)
    acc_ref[...] += _dot(he, wd_ref[...])

    @pl.when(g == N_EGROUPS - 1)
    def _():
        o_ref[...] = x1_ref[...] + g2m_ref[0] * acc_ref[...]


def _moe(h2, comb, x1, gate2, mod_index, tm, wts):
    t, d = x1.shape
    rmod = gate2.shape[1]
    gff = E_PER_GROUP * EXPERT_FF
    row = lambda w: pl.BlockSpec((tm, w), lambda i, g: (i, 0))
    return pl.pallas_call(
        _moe_body,
        grid=(t // tm, N_EGROUPS),
        in_specs=[row(d),
                  pl.BlockSpec((None, d, gff), lambda i, g: (g, 0, 0)),
                  pl.BlockSpec((None, d, gff), lambda i, g: (g, 0, 0)),
                  pl.BlockSpec((None, gff, d), lambda i, g: (g, 0, 0)),
                  row(LANES), row(d),
                  pl.BlockSpec((1, rmod, d), lambda i, g: mod_index(i))],
        out_specs=row(d),
        out_shape=jax.ShapeDtypeStruct((t, d), F32),
        scratch_shapes=[pltpu.VMEM((tm, d), F32)],
        compiler_params=_params("arbitrary", "arbitrary"),
        name="moe",
    )(h2, wts["w_gate"], wts["w_up"], wts["w_down"], comb, x1, gate2)


def _page_fetch(pt_ref, cache_ref, buf, sem, npages):
    b = pl.program_id(0)
    slot = lax.rem(b, 2)

    def start_all(bb, sl):
        def body(p, carry):
            pltpu.make_async_copy(cache_ref.at[pt_ref[bb, p]], buf.at[sl, p], sem.at[sl]).start()
            return carry
        lax.fori_loop(0, npages, body, 0)

    @pl.when(b == 0)
    def _():
        start_all(0, 0)

    @pl.when(b + 1 < pl.num_programs(0))
    def _():
        start_all(b + 1, 1 - slot)

    def wait_one(p, carry):
        pltpu.make_async_copy(cache_ref.at[0], buf.at[slot, p], sem.at[slot]).wait()
        return carry
    lax.fori_loop(0, npages, wait_one, 0)
    return slot


def _cmp_sample_body(pt_ref, cache_ref, qbd_ref, pw_all_ref, gkc_ref, bd_ref, pair_ref, rsum_ref, rexp_ref,
                     ocmp_ref, mask_ref, buf, sem, kc_scr, *, npages, past_len, tdec):
    slot = _page_fetch(pt_ref, cache_ref, buf, sem, npages)
    nblk = past_len // SEL_BLOCK
    ppi = SUBLANES // (PAGE_SIZE // CMP_BLOCK)
    kw = KV_WIDTH

    tiles = min(POOL_UNROLL, npages // ppi)

    def pool(i, carry):
        for u in range(tiles):
            kc8 = jnp.zeros((SUBLANES, kw), F32)
            vc8 = jnp.zeros((SUBLANES, kw), F32)
            for j in range(ppi):
                r1 = _dot_nt(pw_all_ref[j], buf[slot, (i * tiles + u) * ppi + j].astype(BF16))
                s8 = SUBLANES
                kc8 = kc8 + r1[0:s8, :kw] + r1[s8:2 * s8, :kw]
                vc8 = vc8 + r1[2 * s8:3 * s8, kw:] + r1[3 * s8:, kw:]
            rows = pl.ds(pl.multiple_of((i * tiles + u) * SUBLANES, SUBLANES), SUBLANES)
            kc_scr[rows, 0:kw] = kc8
            kc_scr[rows, kw:] = vc8
        return carry
    lax.fori_loop(0, npages // (ppi * tiles), pool, 0)

    kcn = _head_rms(kc_scr[:, 0:kw], bd_ref[...]) * gkc_ref[...]
    vc = kc_scr[:, kw:]
    s = _dot_nt(qbd_ref[...], kcn.astype(BF16))
    e = jnp.exp2(s - s.max(axis=-1, keepdims=True))
    p = e / jnp.maximum(e.sum(axis=-1, keepdims=True), TINY)
    o = _dot(p.astype(BF16), vc.astype(BF16))
    nrow = o.shape[0]
    first_group = lax.broadcasted_iota(jnp.int32, (nrow, HEAD_DIM), 0) < nrow // N_KV
    ocmp_ref[...] = jnp.where(first_group, o[:, :HEAD_DIM], o[:, HEAD_DIM:])

    imp = _dot_exact_rhs(_dot_exact_lhs(rsum_ref[...], p), pair_ref[...])
    lane = lax.broadcasted_iota(jnp.int32, imp.shape, 1)
    tok = lax.rem(lax.broadcasted_iota(jnp.int32, imp.shape, 0), tdec)
    cur = (past_len + tok) // SEL_BLOCK
    allowed = (lane < cur) & (lane < nblk)
    forced = (lane == 0) | (lane == cur - 1)
    score = jnp.where(allowed, jnp.where(forced, BIG, imp), NEG)
    selm = jnp.where(allowed, _topk_mask_lanes(score, nblk, min(N_SEL - 1, nblk)), 0.0)
    mask_ref[...] = _dot(rexp_ref[...], selm.astype(BF16))


def _sel_sample_body(pt_ref, cache_ref, qbd_ref, mask_ref, eexp_ref, selnew_ref, winc_ref, winnew_ref, gt_ref,
                     ocmp_ref, o_ref, buf, sem, *, npages, past_len, tdec, ppc):
    slot = _page_fetch(pt_ref, cache_ref, buf, sem, npages)
    qbd = qbd_ref[...]
    nrow = qbd.shape[0]
    tok = lax.rem(lax.broadcasted_iota(jnp.int32, (nrow, 1), 0), tdec)
    kch = ppc * PAGE_SIZE
    kw = KV_WIDTH

    def attend(pieces):
        ss = [jnp.where(mask, s, NEG) for s, mask, _ in pieces]
        m = functools.reduce(jnp.maximum, [s.max(axis=-1, keepdims=True) for s in ss])
        es = [jnp.where(mask, jnp.exp2(s - m), 0.0) for s, (_, mask, _) in zip(ss, pieces)]
        l = sum(e.sum(axis=-1, keepdims=True) for e in es)
        acc = sum(pv(e.astype(BF16)) for e, (_, _, pv) in zip(es, pieces))
        o = acc / jnp.maximum(l, TINY)
        first_group = lax.broadcasted_iota(jnp.int32, (nrow, HEAD_DIM), 0) < nrow // N_KV
        return jnp.where(first_group, o[:, :HEAD_DIM], o[:, HEAD_DIM:])

    def feature_major(kt, vt, mask):
        vtb = vt.astype(BF16)
        return _dot(qbd, kt.astype(BF16)), mask, lambda e: _dot_nt(e, vtb)

    def new_rows(new_ref):
        kv = new_ref[...]
        idx = lax.broadcasted_iota(jnp.int32, (nrow, kv.shape[0]), 1)
        v = kv[:, kw:].astype(BF16)
        return _dot_nt(qbd, kv[:, :kw].astype(BF16)), (idx <= tok) & (idx < tdec), lambda e: _dot(e, v)

    selm = mask_ref[...].astype(BF16)
    pieces = []
    for c in range(npages // ppc):
        pages = [buf[slot, c * ppc + j] for j in range(ppc)]
        kt = jnp.concatenate([pg[:kw] for pg in pages], axis=1)
        vt = jnp.concatenate([pg[kw:] for pg in pages], axis=1)
        pieces.append(feature_major(kt, vt, _dot(selm, eexp_ref[:, c * kch:(c + 1) * kch]) > 0.5))
    o_sel = attend(pieces + [new_rows(selnew_ref)])

    wbuf = winc_ref.shape[1]
    rel = wbuf + tok - lax.broadcasted_iota(jnp.int32, (nrow, wbuf), 1)
    o_win = attend([feature_major(winc_ref[0:kw, :], winc_ref[kw:, :], (rel >= 0) & (rel < WINDOW)),
                    new_rows(winnew_ref)])

    gates = _sigmoid(gt_ref[...])
    o_ref[...] = gates[:, 0:1] * ocmp_ref[...] + gates[:, 1:2] * o_sel + gates[:, 2:3] * o_win


def _attn_sample(qbd, gt, page_table, cache_cmp_t, cache_sel_t, cache_win_t, sel_new, win_new, wts, past_len, tdec):
    nb, nrow, _ = qbd.shape
    npages = page_table.shape[1]
    ncb = past_len // CMP_BLOCK
    per_b = lambda r, w: pl.BlockSpec((None, r, w), lambda b, pt: (b, 0, 0))
    full = lambda a: pl.BlockSpec(a.shape, lambda b, pt: (0,) * a.ndim)
    hbm = pl.BlockSpec(memory_space=pl.ANY)
    page_buf = pltpu.VMEM((2, npages, 2 * KV_WIDTH, PAGE_SIZE), F32)

    r = np.arange(nrow)
    grp = (r // tdec) // Q_PER_KV * tdec + r % tdec
    rsum = (np.arange(N_KV * tdec)[:, None] == grp[None, :]).astype(np.float32)
    pair = np.zeros((ncb, LANES), np.float32)
    pair[np.arange(ncb), np.arange(ncb) // (SEL_BLOCK // CMP_BLOCK)] = 1.0
    eexp = (np.arange(LANES)[:, None] == np.arange(past_len)[None, :] // SEL_BLOCK).astype(np.float32)
    consts1 = [wts["pw_all"], wts["gkc"], wts["bd_kv"], jnp.asarray(pair, BF16),
               jnp.asarray(rsum, BF16), jnp.asarray(rsum.T, BF16)]

    o_cmp, mask = pl.pallas_call(
        functools.partial(_cmp_sample_body, npages=npages, past_len=past_len, tdec=tdec),
        grid_spec=pltpu.PrefetchScalarGridSpec(
            num_scalar_prefetch=1, grid=(nb,),
            in_specs=[hbm, per_b(nrow, KV_WIDTH)] + [full(a) for a in consts1],
            out_specs=[per_b(nrow, HEAD_DIM), per_b(nrow, LANES)],
            scratch_shapes=[page_buf, pltpu.SemaphoreType.DMA((2,)), pltpu.VMEM((ncb, 2 * KV_WIDTH), F32)]),
        out_shape=[jax.ShapeDtypeStruct((nb, nrow, HEAD_DIM), F32), jax.ShapeDtypeStruct((nb, nrow, LANES), F32)],
        compiler_params=_params("arbitrary"),
        name="attn_sample_cmp",
    )(page_table, cache_cmp_t, qbd, *consts1)

    ppc = min(8, npages)
    eexp = jnp.asarray(eexp, BF16)
    return pl.pallas_call(
        functools.partial(_sel_sample_body, npages=npages, past_len=past_len, tdec=tdec, ppc=ppc),
        grid_spec=pltpu.PrefetchScalarGridSpec(
            num_scalar_prefetch=1, grid=(nb,),
            in_specs=[hbm, per_b(nrow, KV_WIDTH), per_b(nrow, LANES), full(eexp),
                      per_b(sel_new.shape[1], 2 * KV_WIDTH), per_b(2 * KV_WIDTH, cache_win_t.shape[2]),
                      per_b(win_new.shape[1], 2 * KV_WIDTH), per_b(nrow, SUBLANES), per_b(nrow, HEAD_DIM)],
            out_specs=per_b(nrow, HEAD_DIM),
            scratch_shapes=[page_buf, pltpu.SemaphoreType.DMA((2,))]),
        out_shape=jax.ShapeDtypeStruct((nb, nrow, HEAD_DIM), F32),
        compiler_params=_params("arbitrary"),
        name="attn_sample_sel",
    )(page_table, cache_sel_t, qbd, mask, eexp, sel_new, cache_win_t, win_new, gt, o_cmp)


def _pool_weights(w_pos_k, w_pos_v):
    bpp = PAGE_SIZE // CMP_BLOCK
    ppi = SUBLANES // bpp
    rows = np.arange(PAGE_SIZE)
    place = np.zeros((ppi, SUBLANES, PAGE_SIZE), np.float32)
    for j in range(ppi):
        place[j, j * bpp + rows // CMP_BLOCK, rows] = 1.0
    place = jnp.asarray(place)

    def hi_lo(w):
        full = place * jnp.tile(w, bpp)[None, None, :]
        hi = full.astype(BF16)
        return hi, (full - hi.astype(F32)).astype(BF16)

    k_hi, k_lo = hi_lo(w_pos_k)
    v_hi, v_lo = hi_lo(w_pos_v)
    return jnp.concatenate([k_hi, k_lo, v_hi, v_lo], axis=1)


def _prep_weights(l, g_norm1, g_norm2, w_in, g_q, g_k_cmp, g_k_sel, g_k_win, w_pos_k, w_pos_v, conv_w, conv_b,
                  dt_bias, a_log, d_skip, g_att_out, g_ssm_out, w_out, w_rg, b_rg, w_re, b_re, w_gate, w_up, w_down):
    w = w_in[l]
    o_gt = ATT_WIDTH + 6 * KV_WIDTH
    o_z = o_gt + 3 * N_HEADS
    o_xbc = o_z + SSM_WIDTH
    o_dt = o_xbc + CONV_DIM
    pad = jnp.zeros((D_MODEL, PK_WIDTH - PK_GD - SSM_HEADS - 3 * N_HEADS), F32)
    w_packed = jnp.concatenate([w[:, :o_gt], w[:, o_z:o_xbc], w[:, o_xbc:o_dt], w[:, o_dt:], w[:, o_gt:o_z], pad],
                               axis=1).astype(BF16)
    seg = np.arange(KV_WIDTH) // HEAD_DIM
    bd_kv = jnp.asarray((seg[:, None] == seg[None, :]).astype(np.float32), BF16)
    lane_pad = lambda v: jnp.pad(v.astype(F32), (0, LANES - v.shape[0])).reshape(1, LANES)
    wr = jnp.concatenate([w_rg[l], jnp.transpose(w_re[l], (1, 0, 2)).reshape(D_MODEL, N_EXPERTS)], axis=1)
    wr = jnp.pad(wr, ((0, 0), (0, LANES - wr.shape[1])))
    wr_hi = wr.astype(BF16)
    wr_lo = (wr - wr_hi.astype(F32)).astype(BF16)
    by_group = lambda w: jnp.transpose(w.astype(BF16).reshape(N_EGROUPS, E_PER_GROUP, D_MODEL, EXPERT_FF),
                                       (0, 2, 1, 3)).reshape(N_EGROUPS, D_MODEL, E_PER_GROUP * EXPERT_FF)
    return {
        "g1": g_norm1[l].reshape(1, D_MODEL), "w_in": w_packed,
        "gq_col": (jnp.tile(g_q[l], N_HEADS) * (SCALE * LOG2E)).reshape(ATT_WIDTH, 1),
        "gks": jnp.tile(g_k_sel[l], N_KV).reshape(1, KV_WIDTH),
        "gkw": jnp.tile(g_k_win[l], N_KV).reshape(1, KV_WIDTH),
        "gkc": jnp.tile(g_k_cmp[l], N_KV).reshape(1, KV_WIDTH),
        "bd_kv": bd_kv,
        "wkv": jnp.concatenate([jnp.broadcast_to(w_pos_k[l][:, None], (CMP_BLOCK, KV_WIDTH)),
                                jnp.broadcast_to(w_pos_v[l][:, None], (CMP_BLOCK, KV_WIDTH))], axis=1),
        "pw_all": _pool_weights(w_pos_k[l], w_pos_v[l]),
        "conv_w": conv_w[l], "conv_b": conv_b[l].reshape(1, CONV_DIM),
        "dt_bias": lane_pad(dt_bias[l]), "a_log": lane_pad(a_log[l]), "d_skip": jnp.repeat(d_skip[l].astype(F32), SSM_HEAD_DIM).reshape(1, SSM_WIDTH),
        "g_ssm": g_ssm_out[l].reshape(1, SSM_WIDTH),
        "g_att": g_att_out[l].reshape(1, ATT_WIDTH), "w_out": w_out[l].astype(BF16),
        "g2": g_norm2[l].reshape(1, D_MODEL), "wr_hi": wr_hi, "wr_lo": wr_lo,
        "br": lane_pad(jnp.concatenate([b_rg[l], b_re[l].reshape(-1)])),
        "w_gate": by_group(w_gate[l]), "w_up": by_group(w_up[l]),
        "w_down": w_down[l].astype(BF16).reshape(N_EGROUPS, E_PER_GROUP * EXPERT_FF, D_MODEL),
    }


def _finish(att2d, y2d, x2d, mods, mod_index, tm_post, tm_moe, moe_index, wts):
    x1, h2, comb = _post(att2d, y2d, x2d, mods[2], mods[3], mods[4], mod_index, tm_post, wts)
    return _moe(h2, comb, x1, mods[5], moe_index, tm_moe, wts)


def _token_major_cache(t):
    b, _, n = t.shape
    return jnp.transpose(t.reshape(b, 2, N_KV, HEAD_DIM, n), (0, 4, 1, 2, 3))[None]


def _feature_major_cache(c):
    n, rows = c.shape[:2]
    return jnp.transpose(c, (0, 2, 3, 4, 1)).reshape(n, 2 * KV_WIDTH, rows)


def kernel(x_prompt, x_sample, cache_cmp, cache_sel, cache_win, state_ssm, state_conv, page_table, c_prompt, c_sample,
           g_norm1, g_norm2, w_ada, b_ada, w_in, g_q, g_k_cmp, g_k_sel, g_k_win, w_pos_k, w_pos_v, conv_w, conv_b,
           dt_bias, a_log, d_skip, g_att_out, g_ssm_out, w_out, w_rg, b_rg, w_re, b_re, w_gate, w_up, w_down):
    depth = w_in.shape[0]
    assert depth == 1
    l = 0
    bp, seq, d = x_prompt.shape
    bs, tdec, _ = x_sample.shape
    npages = page_table.shape[1]
    past_len = npages * PAGE_SIZE
    tp, ts = bp * seq, bs * tdec

    wts = _prep_weights(l, g_norm1, g_norm2, w_in, g_q, g_k_cmp, g_k_sel, g_k_win, w_pos_k, w_pos_v, conv_w, conv_b,
                        dt_bias, a_log, d_skip, g_att_out, g_ssm_out, w_out, w_rg, b_rg, w_re, b_re, w_gate, w_up,
                        w_down)
    m_all = _ada(jnp.concatenate([c_prompt, c_sample], axis=0), w_ada[l], b_ada[l])
    mods_p = [m.reshape(bp, 1, d) for m in jnp.split(m_all[:bp], 6, axis=-1)]
    tm_s = min(512, ts)
    mods_s = [jnp.repeat(m, tdec, axis=0).reshape(ts // tm_s, tm_s, d) for m in jnp.split(m_all[bp:], 6, axis=-1)]

    tm = 512
    per_seq = seq // tm
    xp2 = x_prompt.reshape(tp, d)
    qt, cmpt_p, selt_p, selk, selvt, wint_p, wink, winvt, z, xbc, gd, gdt, kc = _in_proj(
        xp2, bp, mods_p[0], mods_p[1], lambda i: (i // per_seq, 0, 0), tm, wts, True)
    ncb = seq // CMP_BLOCK
    nch = seq // KEY_CHUNK
    r3 = lambda a, n: a.reshape(bp, n, a.shape[-1])
    kc_eo = kc.reshape(bp, ncb // 2, 2, 2 * KV_WIDTH).transpose(0, 2, 1, 3).reshape(bp, ncb, 2 * KV_WIDTH)
    kcmp = kc_eo[:, :, :KV_WIDTH].astype(BF16)
    vcmp_t = jnp.transpose(kc_eo[:, :, KV_WIDTH:], (0, 2, 1)).astype(BF16)
    chunked = lambda a: a.reshape(bp, nch, KV_WIDTH, KEY_CHUNK)
    att = _attn_prompt(qt, r3(selk, seq), chunked(selvt), r3(wink, seq), chunked(winvt), kcmp, vcmp_t, gdt)
    y_ssm, ssm_p, conv_p = _ssd(r3(xbc, seq), r3(z, seq), r3(gd, seq), jnp.zeros((bp, SUBLANES, CONV_DIM), F32),
                                jnp.zeros((bp, SSM_HEADS, SSM_HEAD_DIM, SSM_STATE), F32), wts, SSD_CHUNK, SSD_CHUNK,
                                1)
    tm_moe = 512
    y_prompt = _finish(att.reshape(tp, ATT_WIDTH), y_ssm.reshape(tp, SSM_WIDTH), xp2, mods_p,
                       lambda i: (i // per_seq, 0, 0), tm, tm_moe, lambda i: (i // (seq // tm_moe), 0, 0), wts)

    xs2 = x_sample.reshape(ts, d)
    qt_s, cmpt_s, selt_s, _, _, wint_s, _, _, z, xbc, gd, _ = _in_proj(
        xs2, 1, mods_s[0], mods_s[1], lambda i: (i, 0, 0), tm_s, wts, False)
    nrow = N_HEADS * tdec
    qh = qt_s[0].T.reshape(bs, tdec, N_HEADS, HEAD_DIM).transpose(0, 2, 1, 3).reshape(bs, nrow, HEAD_DIM)
    g0 = (jnp.arange(nrow) < nrow // N_KV)[None, :, None]
    zero = jnp.zeros_like(qh)
    qbd = jnp.concatenate([jnp.where(g0, qh, zero), jnp.where(g0, zero, qh)], axis=-1)
    gt = gd[:, GD_GATE0:GD_GATE0 + 3 * N_HEADS].reshape(bs, tdec, 3, N_HEADS).transpose(0, 3, 1, 2)
    gt = jnp.pad(gt.reshape(bs, nrow, 3), ((0, 0), (0, 0), (0, SUBLANES - 3)))
    rows_s = lambda t: t[0].T.reshape(bs, tdec, 2 * KV_WIDTH)
    sel_s, win_s = rows_s(selt_s), rows_s(wint_s)
    new_pad = 2 * SUBLANES - tdec
    padded = lambda a: jnp.pad(a, ((0, 0), (0, new_pad), (0, 0)))
    cwin_t = _feature_major_cache(cache_win[l])
    att_s = _attn_sample(qbd, gt, page_table, _feature_major_cache(cache_cmp[l]), _feature_major_cache(cache_sel[l]),
                         cwin_t, padded(sel_s), padded(win_s), wts, past_len, tdec)
    att_s = att_s.reshape(bs, N_HEADS, tdec, HEAD_DIM).transpose(0, 2, 1, 3).reshape(ts, ATT_WIDTH)

    lpad = SUBLANES - tdec
    pad_rows = lambda a: jnp.pad(a.reshape(bs, tdec, a.shape[-1]), ((0, 0), (0, lpad), (0, 0)))
    cprev = jnp.pad(state_conv[l], ((0, 0), (SUBLANES - (CONV_W - 1), 0), (0, 0)))
    y_s, ssm_s, conv_s = _ssd(pad_rows(xbc), pad_rows(z), pad_rows(gd), cprev, state_ssm[l].astype(F32), wts,
                              SUBLANES, tdec, SSD_SAMPLE_BATCH)
    y_s = y_s[:, :tdec].reshape(ts, SSM_WIDTH)
    y_sample = _finish(att_s, y_s, xs2, mods_s, lambda i: (i, 0, 0), tm_s, tm_s, lambda i: (i, 0, 0), wts)

    kv6 = lambda a: a.reshape(1, bs, tdec, 2, N_KV, HEAD_DIM)
    wkeep = min(WINDOW, seq)
    wbuf = cwin_t.shape[2]
    wkeep_s = min(WINDOW, wbuf + tdec)
    win_all_t = jnp.concatenate([cwin_t, jnp.transpose(win_s, (0, 2, 1))], axis=2)[:, :, wbuf + tdec - wkeep_s:]
    tail = lambda a: a[:, SUBLANES - (CONV_W - 1):][None]
    return (y_prompt.reshape(bp, seq, d), y_sample.reshape(bs, tdec, d),
            _token_major_cache(cmpt_p), kv6(rows_s(cmpt_s)), _token_major_cache(selt_p), kv6(sel_s),
            _token_major_cache(wint_p[:, :, seq - wkeep:]), _token_major_cache(win_all_t),
            ssm_p[None], ssm_s[None].astype(state_ssm.dtype), tail(conv_p), tail(conv_s))
```

```python
import functools

import numpy as np
import jax
import jax.numpy as jnp
from jax import lax
from jax.experimental import pallas as pl
from jax.experimental.pallas import tpu as pltpu

F32 = jnp.float32
BF16 = jnp.bfloat16

D_MODEL = 1024
PAGE_SIZE = 128
N_HEADS = 8
N_KV = 2
HEAD_DIM = 64
Q_PER_KV = N_HEADS // N_KV
ATT_WIDTH = N_HEADS * HEAD_DIM
KV_WIDTH = N_KV * HEAD_DIM
CMP_BLOCK = 32
SEL_BLOCK = 64
N_SEL = 16
WINDOW = 512
SCALE = HEAD_DIM ** -0.5
SSM_HEADS = 8
SSM_HEAD_DIM = 64
SSM_WIDTH = SSM_HEADS * SSM_HEAD_DIM
SSM_GROUPS = 2
SSM_STATE = 64
CONV_W = 4
SSD_CHUNK = 128
CONV_DIM = SSM_WIDTH + 2 * SSM_GROUPS * SSM_STATE
N_EGROUPS = 4
E_PER_GROUP = 4
N_EXPERTS = N_EGROUPS * E_PER_GROUP
EXPERT_FF = 256
EPS = 1e-6
NEG = -1e30
BIG = 1e9
TINY = 1e-30
LOG2E = 1.4426950408889634

LANES = 128
SUBLANES = 8
VMEM_LIMIT = 48 * 1024 * 1024
Q_TILE = 128
KEY_CHUNK = 128
FLASH_UNROLL = 2
DEAD_SHIFT = 1 << 24
PAGE_SLOTS = 3
SSD_SAMPLE_BATCH = 8
POOL_UNROLL = 16

PK_Q = 0
PK_CMP = 512
PK_SEL = 768
PK_WIN = 1024
PK_Z = 1280
PK_XBC = 1792
PK_GD = 2560
PK_WIDTH = 2688
GD_GATE0 = 8
RT_E0 = 4


def _dot(a, b):
    return jnp.dot(a, b, preferred_element_type=F32)


def _dot_nt(a, b):
    return lax.dot_general(a, b, (((1,), (1,)), ((), ())), preferred_element_type=F32)


def _dot_tn(a, b):
    return lax.dot_general(a, b, (((0,), (0,)), ((), ())), preferred_element_type=F32)


def _split2(x):
    hi = x.astype(BF16)
    lo = (x - hi.astype(F32)).astype(BF16)
    return hi, lo


def _split3(x):
    hi = x.astype(BF16)
    r = x - hi.astype(F32)
    mid = r.astype(BF16)
    lo = (r - mid.astype(F32)).astype(BF16)
    return hi, mid, lo


def _dot_exact_lhs(a_bf16, x):
    return sum(_dot(a_bf16, p) for p in _split3(x))


def _dot_exact_rhs(x, b_bf16):
    return sum(_dot(p, b_bf16) for p in _split3(x))


def _sigmoid(x):
    return 1.0 / (1.0 + jnp.exp(-x))


def _silu(x):
    return x * _sigmoid(x)


def _softplus(x):
    return jnp.maximum(x, 0.0) + jnp.log1p(jnp.exp(-jnp.abs(x)))


def _rms(x):
    return x * lax.rsqrt(jnp.mean(x * x, axis=-1, keepdims=True) + EPS)


def _head_rms(x, bd):
    hi, lo = _split2(x * x)
    ms = (_dot(hi, bd) + _dot(lo, bd)) * (1.0 / HEAD_DIM)
    return x * lax.rsqrt(ms + EPS)


def _topk_mask_lanes(score, nblk, k):
    lane = lax.broadcasted_iota(jnp.int32, score.shape, 1)
    rank = jnp.zeros(score.shape, F32)
    for i in range(nblk):
        si = score[:, i:i + 1]
        beats = jnp.where(si > score, 1.0, jnp.where(si == score, jnp.where(lane > i, 1.0, 0.0), 0.0))
        rank = rank + beats
    return jnp.where(rank < k, 1.0, 0.0)


def _topk_mask_rows(score, k):
    row = lax.broadcasted_iota(jnp.int32, score.shape, 0)
    rank = jnp.zeros(score.shape, F32)
    for i in range(score.shape[0]):
        si = score[i:i + 1, :]
        beats = jnp.where(si > score, 1.0, jnp.where(si == score, jnp.where(row > i, 1.0, 0.0), 0.0))
        rank = rank + beats
    return jnp.where(rank < k, 1.0, 0.0)


def _params(*sem):
    return pltpu.CompilerParams(dimension_semantics=sem, vmem_limit_bytes=VMEM_LIMIT)


def _ada_body(c_ref, w_ref, b_ref, o_ref):
    c = _silu(c_ref[...]).astype(BF16)
    o_ref[...] = _dot(c, w_ref[...].astype(BF16)) + b_ref[...]


def _ada(c_all, w_ada, b_ada):
    n, d = c_all.shape
    width = w_ada.shape[1]
    tn = 1536
    return pl.pallas_call(
        _ada_body,
        grid=(width // tn,),
        in_specs=[pl.BlockSpec((n, d), lambda j: (0, 0)),
                  pl.BlockSpec((d, tn), lambda j: (0, j)),
                  pl.BlockSpec((1, tn), lambda j: (0, j))],
        out_specs=pl.BlockSpec((n, tn), lambda j: (0, j)),
        out_shape=jax.ShapeDtypeStruct((n, width), F32),
        compiler_params=_params("arbitrary"),
        name="ada",
    )(c_all, w_ada, b_ada.reshape(1, width))


def _inproj_body(with_cmp, x_ref, sh_ref, sc_ref, g1_ref, w_ref, gqc_ref, gks_ref, gkw_ref, bd_ref, *rest):
    if with_cmp:
        wkv_ref, gkc_ref = rest[:2]
        rest = rest[2:]
    (qt_ref, cmpt_ref, selt_ref, selk_ref, selvt_ref, wint_ref, wink_ref, winvt_ref,
     z_ref, xbc_ref, gd_ref, gdt_ref) = rest[:12]
    tm = x_ref.shape[0]
    h = _rms(x_ref[...]) * g1_ref[...]
    hb = (h * (1.0 + sc_ref[0]) + sh_ref[0]).astype(BF16)
    bd_kv = bd_ref[...]

    q3 = _dot(hb, w_ref[:, PK_Q:PK_CMP]).T.reshape(N_HEADS, HEAD_DIM, tm)
    q3 = q3 * lax.rsqrt(jnp.mean(q3 * q3, axis=1, keepdims=True) + EPS)
    qt_ref[...] = (q3.reshape(ATT_WIDTH, tm) * gqc_ref[...]).astype(BF16)

    kv = _dot(hb, w_ref[:, PK_CMP:PK_SEL])
    cmpt_ref[...] = kv.T
    if with_cmp:
        kc_ref = rest[12]
        pooled = (kv.reshape(tm // CMP_BLOCK, CMP_BLOCK, 2 * KV_WIDTH) * wkv_ref[...][None]).sum(axis=1)
        kc_ref[:, 0:KV_WIDTH] = _head_rms(pooled[:, :KV_WIDTH], bd_kv) * gkc_ref[...]
        kc_ref[:, KV_WIDTH:] = pooled[:, KV_WIDTH:]

    def normed_kv(lo, hi, gain_ref, t_ref, k_ref, vt_ref):
        kv = _dot(hb, w_ref[:, lo:hi])
        kn = _head_rms(kv[:, :KV_WIDTH], bd_kv) * gain_ref[...]
        k_ref[...] = kn.astype(BF16)
        kvt = jnp.concatenate([kn, kv[:, KV_WIDTH:]], axis=1).T
        t_ref[...] = kvt
        vt = kvt[KV_WIDTH:, :].astype(BF16)
        for j in range(tm // KEY_CHUNK):
            vt_ref[j] = vt[:, j * KEY_CHUNK:(j + 1) * KEY_CHUNK]

    normed_kv(PK_SEL, PK_WIN, gks_ref, selt_ref, selk_ref, selvt_ref)
    normed_kv(PK_WIN, PK_Z, gkw_ref, wint_ref, wink_ref, winvt_ref)

    z_ref[...] = _dot(hb, w_ref[:, PK_Z:PK_XBC])
    xbc_ref[...] = _dot(hb, w_ref[:, PK_XBC:PK_GD])
    gd = _dot(hb, w_ref[:, PK_GD:PK_WIDTH])
    gd_ref[...] = gd
    gdt_ref[...] = gd.T


def _in_proj(x2d, nbatch, shift3, scale3, mod_index, tm, wts, with_cmp):
    t, d = x2d.shape
    seq = t // nbatch
    per_seq = seq // tm
    rmod = shift3.shape[1]
    full = lambda a: pl.BlockSpec(a.shape, lambda i: (0,) * a.ndim)
    row = lambda w: pl.BlockSpec((tm, w), lambda i: (i, 0))
    fmaj = lambda w: pl.BlockSpec((None, w, tm), lambda i: (i // per_seq, 0, i % per_seq))
    chunks = pl.BlockSpec((tm // KEY_CHUNK, KV_WIDTH, KEY_CHUNK), lambda i: (i, 0, 0))
    mod = pl.BlockSpec((1, rmod, d), mod_index)
    ins = [x2d, shift3, scale3, wts["g1"], wts["w_in"], wts["gq_col"], wts["gks"], wts["gkw"], wts["bd_kv"]]
    in_specs = [row(d), mod, mod] + [full(a) for a in ins[3:]]
    if with_cmp:
        ins += [wts["wkv"], wts["gkc"]]
        in_specs += [full(wts["wkv"]), full(wts["gkc"])]
    sds = jax.ShapeDtypeStruct
    fm = lambda w, dt: sds((nbatch, w, seq), dt)
    ck = sds((t // KEY_CHUNK, KV_WIDTH, KEY_CHUNK), BF16)
    out_shape = [fm(ATT_WIDTH, BF16), fm(2 * KV_WIDTH, F32),
                 fm(2 * KV_WIDTH, F32), sds((t, KV_WIDTH), BF16), ck,
                 fm(2 * KV_WIDTH, F32), sds((t, KV_WIDTH), BF16), ck,
                 sds((t, SSM_WIDTH), F32), sds((t, CONV_DIM), F32), sds((t, LANES), F32), fm(LANES, F32)]
    out_specs = [fmaj(ATT_WIDTH), fmaj(2 * KV_WIDTH),
                 fmaj(2 * KV_WIDTH), row(KV_WIDTH), chunks,
                 fmaj(2 * KV_WIDTH), row(KV_WIDTH), chunks,
                 row(SSM_WIDTH), row(CONV_DIM), row(LANES), fmaj(LANES)]
    if with_cmp:
        out_shape.append(sds((t // CMP_BLOCK, 2 * KV_WIDTH), F32))
        out_specs.append(pl.BlockSpec((tm // CMP_BLOCK, 2 * KV_WIDTH), lambda i: (i, 0)))
    return pl.pallas_call(
        functools.partial(_inproj_body, with_cmp),
        grid=(t // tm,),
        in_specs=in_specs,
        out_specs=out_specs,
        out_shape=out_shape,
        compiler_params=_params("arbitrary"),
        name="in_proj",
    )(*ins)


def _attn_prompt_body(qt_ref, selk_ref, selvt_ref, wink_ref, winvt_ref, kc_ref, vct_ref, gdt_ref, o_ref, selm_scr,
                      *, tq, seq):
    i = pl.program_id(1)
    ncb = seq // CMP_BLOCK
    nblk = seq // SEL_BLOCK
    bpc = KEY_CHUNK // SEL_BLOCK
    wide = Q_PER_KV * tq
    qt = qt_ref[...]
    t_row = i * tq + lax.broadcasted_iota(jnp.int32, (1, tq), 1)
    cur_row = t_row // SEL_BLOCK
    key_i = lax.broadcasted_iota(jnp.int32, (KEY_CHUNK, tq), 0)
    gates = _sigmoid(gdt_ref[GD_GATE0:GD_GATE0 + 3 * N_HEADS, :])
    rep = lambda a: jnp.concatenate([a] * Q_PER_KV, axis=1)

    def flash(k_ref, vt_ref, qgts, c_lo, c_hi, masks_fn, groups):
        last = seq // KEY_CHUNK - 1

        def step(j, carry):
            chunks = []
            for u in range(FLASH_UNROLL):
                c = c_lo + FLASH_UNROLL * j + u
                cc = jnp.minimum(c, last)
                off = pl.multiple_of(cc * KEY_CHUNK, KEY_CHUNK)
                scores = {g: _dot(k_ref[pl.ds(off, KEY_CHUNK), g * HEAD_DIM:(g + 1) * HEAD_DIM], qgts[g])
                          for g in groups}
                chunks.append((c, cc, scores))
            state = dict(zip(groups, carry))
            for c, cc, scores in chunks:
                masks = masks_fn(c, cc, c < c_hi)
                for g in groups:
                    m, l, acc = state[g]
                    s = jnp.where(rep(masks[g]), scores[g], NEG)
                    m_new = jnp.maximum(m, s.max(axis=0, keepdims=True))
                    alpha = jnp.exp2(m - m_new)
                    e = jnp.exp2(s - m_new)
                    pv = _dot(vt_ref[cc, g * HEAD_DIM:(g + 1) * HEAD_DIM, :], e.astype(BF16))
                    state[g] = (m_new, alpha * l + e.sum(axis=0, keepdims=True), alpha * acc + pv)
            return tuple(state[g] for g in groups)

        init = (jnp.full((1, wide), NEG, F32), jnp.zeros((1, wide), F32), jnp.zeros((HEAD_DIM, wide), F32))
        trips = (c_hi - c_lo + FLASH_UNROLL - 1) // FLASH_UNROLL
        res = lax.fori_loop(0, trips, step, (init,) * len(groups))
        return [acc / jnp.maximum(l, TINY) for _, l, acc in res]

    qgts, o_cmps = [], []
    for g in range(N_KV):
        qgt = jnp.concatenate(
            [qt[(g * Q_PER_KV + h) * HEAD_DIM:(g * Q_PER_KV + h + 1) * HEAD_DIM, :] for h in range(Q_PER_KV)], axis=1)
        qgts.append(qgt)

        kcg = kc_ref[:, g * HEAD_DIM:(g + 1) * HEAD_DIM]
        vctg = vct_ref[g * HEAD_DIM:(g + 1) * HEAD_DIM, :]
        r = lax.broadcasted_iota(jnp.int32, (ncb, tq), 0)
        cblk = jnp.where(r < nblk, 2 * r, 2 * (r - nblk) + 1)
        cmask = rep((cblk * CMP_BLOCK + CMP_BLOCK - 1) <= t_row)
        s = jnp.where(cmask, _dot(kcg, qgt), NEG)
        e = jnp.where(cmask, jnp.exp2(s - s.max(axis=0, keepdims=True)), 0.0)
        p = e / jnp.maximum(e.sum(axis=0, keepdims=True), TINY)
        o_cmps.append(_dot(vctg, p.astype(BF16)))

        imp = sum(p[:, h * tq:(h + 1) * tq] for h in range(Q_PER_KV))
        imp = imp[:nblk] + imp[nblk:]
        jb = lax.broadcasted_iota(jnp.int32, (nblk, tq), 0)
        allowed = jb < cur_row
        forced = (jb == 0) | (jb == cur_row - 1)
        score = jnp.where(allowed, jnp.where(forced, BIG, imp), NEG)
        selm = jnp.where(allowed, _topk_mask_rows(score, min(N_SEL - 1, nblk)), 0.0)
        for c in range(seq // KEY_CHUNK):
            selm_scr[g, c, 0:bpc] = selm[c * bpc:(c + 1) * bpc]

    def key_pos(c, live):
        return c * KEY_CHUNK + key_i + jnp.where(live, 0, DEAD_SHIFT)

    def sel_masks(c, cc, live):
        kp = key_pos(c, live)
        cur_blk = ((kp // SEL_BLOCK) == cur_row) & (kp <= t_row)
        alive = jnp.where(live, 1.0, 0.0)
        out = []
        for g in range(N_KV):
            rows = selm_scr[g, cc] * alive
            picked = jnp.concatenate(
                [jnp.broadcast_to(rows[j:j + 1], (SEL_BLOCK, tq)) for j in range(bpc)], axis=0) > 0.5
            out.append(picked | cur_blk)
        return out

    def win_masks(c, cc, live):
        rel = t_row - key_pos(c, live)
        return [(rel >= 0) & (rel < WINDOW)] * N_KV

    c_hi = ((i + 1) * tq + KEY_CHUNK - 1) // KEY_CHUNK
    c_win = jnp.maximum(i * tq - WINDOW + 1, 0) // KEY_CHUNK
    both = tuple(range(N_KV))
    o_sels = flash(selk_ref, selvt_ref, qgts, 0, c_hi, sel_masks, both)
    o_wins = flash(wink_ref, winvt_ref, qgts, c_win, c_hi, win_masks, both)

    heads = []
    for g in range(N_KV):
        for h in range(Q_PER_KV):
            hh = g * Q_PER_KV + h
            sl = slice(h * tq, (h + 1) * tq)
            gate = lambda br: gates[br * N_HEADS + hh:br * N_HEADS + hh + 1, :]
            heads.append(gate(0) * o_cmps[g][:, sl] + gate(1) * o_sels[g][:, sl] + gate(2) * o_wins[g][:, sl])
    o_ref[...] = jnp.concatenate(heads, axis=0).T


def _attn_prompt(qt, selk, selvt, wink, winvt, kc, vct, gdt):
    b, _, s = qt.shape
    tq = Q_TILE
    ncb = s // CMP_BLOCK
    nch = s // KEY_CHUNK
    tile_t = lambda w: pl.BlockSpec((None, w, tq), lambda bi, i: (bi, 0, i))
    whole = lambda shape: pl.BlockSpec((None,) + shape, lambda bi, i: (bi,) + (0,) * len(shape))
    return pl.pallas_call(
        functools.partial(_attn_prompt_body, tq=tq, seq=s),
        grid=(b, s // tq),
        in_specs=[tile_t(ATT_WIDTH), whole((s, KV_WIDTH)), whole((nch, KV_WIDTH, KEY_CHUNK)),
                  whole((s, KV_WIDTH)), whole((nch, KV_WIDTH, KEY_CHUNK)),
                  whole((ncb, KV_WIDTH)), whole((KV_WIDTH, ncb)), tile_t(LANES)],
        out_specs=pl.BlockSpec((None, tq, ATT_WIDTH), lambda bi, i: (bi, i, 0)),
        out_shape=jax.ShapeDtypeStruct((b, s, ATT_WIDTH), F32),
        scratch_shapes=[pltpu.VMEM((N_KV, nch, SUBLANES, tq), F32)],
        compiler_params=_params("arbitrary", "arbitrary"),
        name="attn_prompt",
    )(qt, selk, selvt, wink, winvt, kc, vct, gdt)


def _ssd_body(xbc_ref, z_ref, gd_ref, cprev_ref, h0_ref, *rest, cl, l_valid):
    consts, outs_scratch = rest[:8], rest[8:]
    for bi in range(xbc_ref.shape[0]):
        per_b = [r.at[bi] for r in (xbc_ref, z_ref, gd_ref, cprev_ref, h0_ref)]
        _ssd_one(*per_b, *consts, *[r.at[bi] for r in outs_scratch], cl=cl, l_valid=l_valid)


def _ssd_one(xbc_ref, z_ref, gd_ref, cprev_ref, h0_ref, cw_ref, cb_ref, dtb_ref, alog_ref, dsk_ref, gout_ref,
             selx_ref, hexp_ref, y_ref, hout_ref, cout_ref, h_scr, xe_scr, yoff_scr, *, cl, l_valid):
    c = pl.program_id(1)
    gn = SSM_GROUPS * SSM_STATE
    hpg = SSM_HEADS // SSM_GROUPS

    @pl.when(c == 0)
    def _():
        h_scr[...] = h0_ref[...]
        xe_scr[0:SUBLANES] = cprev_ref[...]

    xe_scr[SUBLANES:SUBLANES + cl] = xbc_ref[...]
    base = SUBLANES - (CONV_W - 1)
    xc = cb_ref[...]
    for w in range(CONV_W):
        xc = xc + cw_ref[w:w + 1, :] * xe_scr[base + w:base + w + cl, :]
    tail = xe_scr[l_valid:l_valid + SUBLANES, :]
    cout_ref[...] = tail
    xe_scr[0:SUBLANES] = tail
    xc = _silu(xc)
    xs = xc[:, :SSM_WIDTH]
    bmb = xc[:, SSM_WIDTH:SSM_WIDTH + gn].astype(BF16)
    cmb = xc[:, SSM_WIDTH + gn:].astype(BF16)

    dt = _softplus(gd_ref[...] + dtb_ref[...])
    if l_valid < cl:
        dt = jnp.where(lax.broadcasted_iota(jnp.int32, dt.shape, 0) < l_valid, dt, 0.0)
    da = dt * (-jnp.exp(alog_ref[...]))
    ri = lax.broadcasted_iota(jnp.int32, (cl, cl), 0)
    ci = lax.broadcasted_iota(jnp.int32, (cl, cl), 1)
    causal = ri >= ci
    acs = _dot_exact_lhs(jnp.where(causal, 1.0, 0.0).astype(BF16), da)
    if cl == LANES:
        acs_t = acs.T
        acs_row = lambda h: acs_t[h:h + 1, :]
    else:
        acs_rows = sum(_dot_nt(selx_ref[...], p) for p in _split3(acs))
        acs_row = lambda h: acs_rows[h * cl:(h + 1) * cl, :]
    acs_last = acs[cl - 1:cl, :]
    e_last = jnp.exp(acs_last)
    hexp = hexp_ref[...]
    dt_w = _dot_exact_rhs(dt, hexp)
    wend_w = _dot_exact_rhs(jnp.exp(acs_last - acs) * dt, hexp)
    eacs_w = _dot_exact_rhs(jnp.exp(acs), hexp)
    x_dt = (xs * dt_w).astype(BF16)
    x_we = (xs * wend_w).astype(BF16)

    for g in range(SSM_GROUPS):
        bg = bmb[:, g * SSM_STATE:(g + 1) * SSM_STATE]
        cg = cmb[:, g * SSM_STATE:(g + 1) * SSM_STATE]
        cb = _dot_nt(cg, bg)
        for hq in range(hpg):
            h = g * hpg + hq
            hs = slice(h * SSM_HEAD_DIM, (h + 1) * SSM_HEAD_DIM)
            seg = acs[:, h:h + 1] - acs_row(h)
            decay = jnp.where(causal, jnp.exp(jnp.where(causal, seg, 0.0)), 0.0)
            y_ref[:, hs] = _dot((cb * decay).astype(BF16), x_dt[:, hs])
            hst = h_scr[h]
            yoff_scr[:, hs] = _dot_nt(cg, hst.astype(BF16))
            h_scr[h] = e_last[:, h:h + 1] * hst + _dot_tn(x_we[:, hs], bg)

    y = y_ref[...] + yoff_scr[...] * eacs_w + dsk_ref[...] * xs
    yg = y * _silu(z_ref[...])
    y_ref[...] = _rms(yg) * gout_ref[...]
    hout_ref[...] = h_scr[...]


def _ssd(xbc3, z3, gd3, cprev, h0, wts, cl, l_valid, nbs):
    b, l, _ = xbc3.shape
    nc = l // cl
    tile = lambda w: pl.BlockSpec((nbs, cl, w), lambda bi, c: (bi, c, 0))
    full = lambda a: pl.BlockSpec(a.shape, lambda bi, c: (0,) * a.ndim)
    names = ["conv_w", "conv_b", "dt_bias", "a_log", "d_skip", "g_ssm"]
    selx = (np.arange(SSM_HEADS * cl)[:, None] // cl == np.arange(LANES)[None, :]).astype(np.float32)
    selx = jnp.asarray(selx, BF16)
    hexp = (np.arange(LANES)[:, None] == np.arange(SSM_WIDTH)[None, :] // SSM_HEAD_DIM).astype(np.float32)
    consts = [wts[n] for n in names] + [selx, jnp.asarray(hexp, BF16)]
    hshape = (SSM_HEADS, SSM_HEAD_DIM, SSM_STATE)
    return pl.pallas_call(
        functools.partial(_ssd_body, cl=cl, l_valid=l_valid),
        grid=(b // nbs, nc),
        in_specs=[tile(CONV_DIM), tile(SSM_WIDTH), tile(LANES),
                  pl.BlockSpec((nbs, SUBLANES, CONV_DIM), lambda bi, c: (bi, 0, 0)),
                  pl.BlockSpec((nbs,) + hshape, lambda bi, c: (bi, 0, 0, 0))] + [full(a) for a in consts],
        out_specs=[tile(SSM_WIDTH),
                   pl.BlockSpec((nbs,) + hshape, lambda bi, c: (bi, 0, 0, 0)),
                   pl.BlockSpec((nbs, SUBLANES, CONV_DIM), lambda bi, c: (bi, 0, 0))],
        out_shape=[jax.ShapeDtypeStruct((b, l, SSM_WIDTH), F32),
                   jax.ShapeDtypeStruct((b,) + hshape, F32),
                   jax.ShapeDtypeStruct((b, SUBLANES, CONV_DIM), F32)],
        scratch_shapes=[pltpu.VMEM((nbs,) + hshape, F32), pltpu.VMEM((nbs, SUBLANES + cl, CONV_DIM), F32),
                        pltpu.VMEM((nbs, cl, SSM_WIDTH), F32)],
        compiler_params=_params("arbitrary", "arbitrary"),
        name="ssd",
    )(xbc3, z3, gd3, cprev, h0, *consts)


def _post_body(att_ref, y_ref, x_ref, g1m_ref, sh2_ref, sc2_ref, gatt_ref, wo_ref, gn2_ref, wrh_ref, wrl_ref, br_ref,
               x1_ref, h2_ref, comb_ref):
    an = (_rms(att_ref[...]) * gatt_ref[...]).astype(BF16)
    mixed = _dot(an, wo_ref[0:ATT_WIDTH, :]) + _dot(y_ref[...].astype(BF16), wo_ref[ATT_WIDTH:, :])
    x1 = x_ref[...] + g1m_ref[0] * mixed
    x1_ref[...] = x1
    h2 = _rms(x1) * gn2_ref[...] * (1.0 + sc2_ref[0]) + sh2_ref[0]
    h2_ref[...] = h2.astype(BF16)

    hh, hl = _split2(h2)
    logit = _dot(hh, wrh_ref[...]) + _dot(hl, wrh_ref[...]) + _dot(hh, wrl_ref[...]) + br_ref[...]
    lane = lax.broadcasted_iota(jnp.int32, logit.shape, 1).astype(F32)
    first = lambda cond: jnp.min(jnp.where(cond, lane, 1e6), axis=-1, keepdims=True)
    top = lambda cond: jnp.max(jnp.where(cond, logit, NEG), axis=-1, keepdims=True)
    is_g = lane < N_EGROUPS
    mg = top(is_g)
    gsel = first(is_g & (logit == mg))
    p_top = 1.0 / jnp.sum(jnp.where(is_g, jnp.exp(logit - mg), 0.0), axis=-1, keepdims=True)
    lo = RT_E0 + E_PER_GROUP * gsel
    is_e = (lane >= lo) & (lane < lo + E_PER_GROUP)
    v1 = top(is_e)
    i1 = first(is_e & (logit == v1))
    is_e2 = is_e & (lane != i1)
    v2 = top(is_e2)
    i2 = first(is_e2 & (logit == v2))
    e2 = jnp.exp(v2 - v1)
    comb_ref[...] = (jnp.where(lane == i1, p_top / (1.0 + e2), 0.0)
                     + jnp.where(lane == i2, p_top * e2 / (1.0 + e2), 0.0))


def _post(att2d, y2d, x2d, gate1, shift2, scale2, mod_index, tm, wts):
    t, d = x2d.shape
    rmod = gate1.shape[1]
    row = lambda w: pl.BlockSpec((tm, w), lambda i: (i, 0))
    mod = pl.BlockSpec((1, rmod, d), mod_index)
    full = lambda a: pl.BlockSpec(a.shape, lambda i: (0,) * a.ndim)
    consts = [wts["g_att"], wts["w_out"], wts["g2"], wts["wr_hi"], wts["wr_lo"], wts["br"]]
    return pl.pallas_call(
        _post_body,
        grid=(t // tm,),
        in_specs=[row(ATT_WIDTH), row(SSM_WIDTH), row(d), mod, mod, mod] + [full(a) for a in consts],
        out_specs=[row(d), row(d), row(LANES)],
        out_shape=[jax.ShapeDtypeStruct((t, d), F32), jax.ShapeDtypeStruct((t, d), BF16),
                   jax.ShapeDtypeStruct((t, LANES), F32)],
        compiler_params=_params("arbitrary"),
        name="post",
    )(att2d, y2d, x2d, gate1, shift2, scale2, *consts)


def _moe_body(h_ref, wg_ref, wu_ref, wd_ref, comb_ref, x1_ref, g2m_ref, o_ref, acc_ref):
    g = pl.program_id(1)

    @pl.when(g == 0)
    def _():
        acc_ref[...] = jnp.zeros_like(acc_ref)

    h = h_ref[...]
    comb = comb_ref[...]
    lane = lax.broadcasted_iota(jnp.int32, comb.shape, 1)
    first = RT_E0 + E_PER_GROUP * g
    he = []
    for j in range(E_PER_GROUP):
        cw = jnp.sum(jnp.where(lane == first + j, comb, 0.0), axis=-1, keepdims=True)
        he.append((_silu(_dot(h, wg_ref[j])) * _dot(h, wu_ref[j]) * cw).astype(BF16))
    acc_ref[...] += _dot(jnp.concatenate(he, axis=1), wd_ref[...])

    @pl.when(g == N_EGROUPS - 1)
    def _():
        o_ref[...] = x1_ref[...] + g2m_ref[0] * acc_ref[...]


def _moe(h2, comb, x1, gate2, mod_index, tm, wts):
    t, d = x1.shape
    rmod = gate2.shape[1]
    gff = E_PER_GROUP * EXPERT_FF
    row = lambda w: pl.BlockSpec((tm, w), lambda i, g: (i, 0))
    return pl.pallas_call(
        _moe_body,
        grid=(t // tm, N_EGROUPS),
        in_specs=[row(d),
                  pl.BlockSpec((E_PER_GROUP, d, EXPERT_FF), lambda i, g: (g, 0, 0)),
                  pl.BlockSpec((E_PER_GROUP, d, EXPERT_FF), lambda i, g: (g, 0, 0)),
                  pl.BlockSpec((None, gff, d), lambda i, g: (g, 0, 0)),
                  row(LANES), row(d),
                  pl.BlockSpec((1, rmod, d), lambda i, g: mod_index(i))],
        out_specs=row(d),
        out_shape=jax.ShapeDtypeStruct((t, d), F32),
        scratch_shapes=[pltpu.VMEM((tm, d), F32)],
        compiler_params=_params("arbitrary", "arbitrary"),
        name="moe",
    )(h2, wts["w_gate"], wts["w_up"], wts["w_down"], comb, x1, gate2)


def _page_fetch(pt_ref, cache_ref, buf, sem, npages):
    b = pl.program_id(0)
    nb = pl.num_programs(0)
    slot = lax.rem(b, PAGE_SLOTS)
    ahead = PAGE_SLOTS - 1

    def start_all(bb):
        sl = lax.rem(bb, PAGE_SLOTS)

        def body(p, carry):
            pltpu.make_async_copy(cache_ref.at[pt_ref[bb, p]], buf.at[sl, p], sem.at[sl]).start()
            return carry
        lax.fori_loop(0, npages, body, 0)

    for first in range(ahead):
        @pl.when((b == 0) & (first < nb))
        def _():
            start_all(first)

    @pl.when(b + ahead < nb)
    def _():
        start_all(b + ahead)

    def wait_one(p, carry):
        pltpu.make_async_copy(cache_ref.at[0], buf.at[slot, p], sem.at[slot]).wait()
        return carry
    lax.fori_loop(0, npages, wait_one, 0)
    return slot


def _cmp_sample_body(pt_ref, cache_ref, qbd_ref, pw_all_ref, gkc_ref, bd_ref, pair_ref, rsum_ref, rexp_ref,
                     ocmp_ref, mask_ref, buf, sem, kc_scr, *, npages, past_len, tdec):
    slot = _page_fetch(pt_ref, cache_ref, buf, sem, npages)
    nblk = past_len // SEL_BLOCK
    ppi = SUBLANES // (PAGE_SIZE // CMP_BLOCK)
    kw = KV_WIDTH

    tiles = min(POOL_UNROLL, npages // ppi)

    def pool(i, carry):
        for u in range(tiles):
            kc8 = jnp.zeros((SUBLANES, kw), F32)
            vc8 = jnp.zeros((SUBLANES, kw), F32)
            for j in range(ppi):
                r1 = _dot_nt(pw_all_ref[j], buf[slot, (i * tiles + u) * ppi + j].astype(BF16))
                s8 = SUBLANES
                kc8 = kc8 + r1[0:s8, :kw] + r1[s8:2 * s8, :kw]
                vc8 = vc8 + r1[2 * s8:3 * s8, kw:] + r1[3 * s8:, kw:]
            rows = pl.ds(pl.multiple_of((i * tiles + u) * SUBLANES, SUBLANES), SUBLANES)
            kc_scr[rows, 0:kw] = kc8
            kc_scr[rows, kw:] = vc8
        return carry
    lax.fori_loop(0, npages // (ppi * tiles), pool, 0)

    kcn = _head_rms(kc_scr[:, 0:kw], bd_ref[...]) * gkc_ref[...]
    vc = kc_scr[:, kw:]
    s = _dot_nt(qbd_ref[...], kcn.astype(BF16))
    e = jnp.exp2(s - s.max(axis=-1, keepdims=True))
    p = e / jnp.maximum(e.sum(axis=-1, keepdims=True), TINY)
    o = _dot(p.astype(BF16), vc.astype(BF16))
    nrow = o.shape[0]
    first_group = lax.broadcasted_iota(jnp.int32, (nrow, HEAD_DIM), 0) < nrow // N_KV
    ocmp_ref[...] = jnp.where(first_group, o[:, :HEAD_DIM], o[:, HEAD_DIM:])

    imp = _dot_exact_rhs(_dot_exact_lhs(rsum_ref[...], p), pair_ref[...])
    lane = lax.broadcasted_iota(jnp.int32, imp.shape, 1)
    tok = lax.rem(lax.broadcasted_iota(jnp.int32, imp.shape, 0), tdec)
    cur = (past_len + tok) // SEL_BLOCK
    allowed = (lane < cur) & (lane < nblk)
    forced = (lane == 0) | (lane == cur - 1)
    score = jnp.where(allowed, jnp.where(forced, BIG, imp), NEG)
    selm = jnp.where(allowed, _topk_mask_lanes(score, nblk, min(N_SEL - 1, nblk)), 0.0)
    mask_ref[...] = _dot(rexp_ref[...], selm.astype(BF16))


def _sel_sample_body(pt_ref, cache_ref, qbd_ref, mask_ref, eexp_ref, selnew_ref, winc_ref, winnew_ref, gt_ref,
                     ocmp_ref, o_ref, buf, sem, *, npages, past_len, tdec, ppc):
    slot = _page_fetch(pt_ref, cache_ref, buf, sem, npages)
    qbd = qbd_ref[...]
    nrow = qbd.shape[0]
    tok = lax.rem(lax.broadcasted_iota(jnp.int32, (nrow, 1), 0), tdec)
    kch = ppc * PAGE_SIZE
    kw = KV_WIDTH

    def attend(pieces):
        ss = [jnp.where(mask, s, NEG) for s, mask, _ in pieces]
        m = functools.reduce(jnp.maximum, [s.max(axis=-1, keepdims=True) for s in ss])
        es = [jnp.where(mask, jnp.exp2(s - m), 0.0) for s, (_, mask, _) in zip(ss, pieces)]
        l = sum(e.sum(axis=-1, keepdims=True) for e in es)
        acc = sum(pv(e.astype(BF16)) for e, (_, _, pv) in zip(es, pieces))
        o = acc / jnp.maximum(l, TINY)
        first_group = lax.broadcasted_iota(jnp.int32, (nrow, HEAD_DIM), 0) < nrow // N_KV
        return jnp.where(first_group, o[:, :HEAD_DIM], o[:, HEAD_DIM:])

    def feature_major(kt, vt, mask):
        vtb = vt.astype(BF16)
        return _dot(qbd, kt.astype(BF16)), mask, lambda e: _dot_nt(e, vtb)

    def new_rows(new_ref):
        kv = new_ref[...]
        idx = lax.broadcasted_iota(jnp.int32, (nrow, kv.shape[0]), 1)
        v = kv[:, kw:].astype(BF16)
        return _dot_nt(qbd, kv[:, :kw].astype(BF16)), (idx <= tok) & (idx < tdec), lambda e: _dot(e, v)

    selm = mask_ref[...].astype(BF16)
    pieces = []
    for c in range(npages // ppc):
        pages = [buf[slot, c * ppc + j] for j in range(ppc)]
        kt = jnp.concatenate([pg[:kw] for pg in pages], axis=1)
        vt = jnp.concatenate([pg[kw:] for pg in pages], axis=1)
        pieces.append(feature_major(kt, vt, _dot(selm, eexp_ref[:, c * kch:(c + 1) * kch]) > 0.5))
    o_sel = attend(pieces + [new_rows(selnew_ref)])

    wbuf = winc_ref.shape[1]
    rel = wbuf + tok - lax.broadcasted_iota(jnp.int32, (nrow, wbuf), 1)
    o_win = attend([feature_major(winc_ref[0:kw, :], winc_ref[kw:, :], (rel >= 0) & (rel < WINDOW)),
                    new_rows(winnew_ref)])

    gates = _sigmoid(gt_ref[...])
    o_ref[...] = gates[:, 0:1] * ocmp_ref[...] + gates[:, 1:2] * o_sel + gates[:, 2:3] * o_win


def _attn_sample(qbd, gt, page_table, cache_cmp_t, cache_sel_t, cache_win_t, sel_new, win_new, wts, past_len, tdec):
    nb, nrow, _ = qbd.shape
    npages = page_table.shape[1]
    ncb = past_len // CMP_BLOCK
    per_b = lambda r, w: pl.BlockSpec((None, r, w), lambda b, pt: (b, 0, 0))
    full = lambda a: pl.BlockSpec(a.shape, lambda b, pt: (0,) * a.ndim)
    hbm = pl.BlockSpec(memory_space=pl.ANY)
    page_buf = pltpu.VMEM((PAGE_SLOTS, npages, 2 * KV_WIDTH, PAGE_SIZE), F32)
    page_sem = pltpu.SemaphoreType.DMA((PAGE_SLOTS,))

    r = np.arange(nrow)
    grp = (r // tdec) // Q_PER_KV * tdec + r % tdec
    rsum = (np.arange(N_KV * tdec)[:, None] == grp[None, :]).astype(np.float32)
    pair = np.zeros((ncb, LANES), np.float32)
    pair[np.arange(ncb), np.arange(ncb) // (SEL_BLOCK // CMP_BLOCK)] = 1.0
    eexp = (np.arange(LANES)[:, None] == np.arange(past_len)[None, :] // SEL_BLOCK).astype(np.float32)
    consts1 = [wts["pw_all"], wts["gkc"], wts["bd_kv"], jnp.asarray(pair, BF16),
               jnp.asarray(rsum, BF16), jnp.asarray(rsum.T, BF16)]

    o_cmp, mask = pl.pallas_call(
        functools.partial(_cmp_sample_body, npages=npages, past_len=past_len, tdec=tdec),
        grid_spec=pltpu.PrefetchScalarGridSpec(
            num_scalar_prefetch=1, grid=(nb,),
            in_specs=[hbm, per_b(nrow, KV_WIDTH)] + [full(a) for a in consts1],
            out_specs=[per_b(nrow, HEAD_DIM), per_b(nrow, LANES)],
            scratch_shapes=[page_buf, page_sem, pltpu.VMEM((ncb, 2 * KV_WIDTH), F32)]),
        out_shape=[jax.ShapeDtypeStruct((nb, nrow, HEAD_DIM), F32), jax.ShapeDtypeStruct((nb, nrow, LANES), F32)],
        compiler_params=_params("arbitrary"),
        name="attn_sample_cmp",
    )(page_table, cache_cmp_t, qbd, *consts1)

    ppc = min(8, npages)
    eexp = jnp.asarray(eexp, BF16)
    return pl.pallas_call(
        functools.partial(_sel_sample_body, npages=npages, past_len=past_len, tdec=tdec, ppc=ppc),
        grid_spec=pltpu.PrefetchScalarGridSpec(
            num_scalar_prefetch=1, grid=(nb,),
            in_specs=[hbm, per_b(nrow, KV_WIDTH), per_b(nrow, LANES), full(eexp),
                      per_b(sel_new.shape[1], 2 * KV_WIDTH), per_b(2 * KV_WIDTH, cache_win_t.shape[2]),
                      per_b(win_new.shape[1], 2 * KV_WIDTH), per_b(nrow, SUBLANES), per_b(nrow, HEAD_DIM)],
            out_specs=per_b(nrow, HEAD_DIM),
            scratch_shapes=[page_buf, page_sem]),
        out_shape=jax.ShapeDtypeStruct((nb, nrow, HEAD_DIM), F32),
        compiler_params=_params("arbitrary"),
        name="attn_sample_sel",
    )(page_table, cache_sel_t, qbd, mask, eexp, sel_new, cache_win_t, win_new, gt, o_cmp)


def _pool_weights(w_pos_k, w_pos_v):
    bpp = PAGE_SIZE // CMP_BLOCK
    ppi = SUBLANES // bpp
    rows = np.arange(PAGE_SIZE)
    place = np.zeros((ppi, SUBLANES, PAGE_SIZE), np.float32)
    for j in range(ppi):
        place[j, j * bpp + rows // CMP_BLOCK, rows] = 1.0
    place = jnp.asarray(place)

    def hi_lo(w):
        full = place * jnp.tile(w, bpp)[None, None, :]
        hi = full.astype(BF16)
        return hi, (full - hi.astype(F32)).astype(BF16)

    k_hi, k_lo = hi_lo(w_pos_k)
    v_hi, v_lo = hi_lo(w_pos_v)
    return jnp.concatenate([k_hi, k_lo, v_hi, v_lo], axis=1)


def _prep_weights(l, g_norm1, g_norm2, w_in, g_q, g_k_cmp, g_k_sel, g_k_win, w_pos_k, w_pos_v, conv_w, conv_b,
                  dt_bias, a_log, d_skip, g_att_out, g_ssm_out, w_out, w_rg, b_rg, w_re, b_re, w_gate, w_up, w_down):
    w = w_in[l]
    o_gt = ATT_WIDTH + 6 * KV_WIDTH
    o_z = o_gt + 3 * N_HEADS
    o_xbc = o_z + SSM_WIDTH
    o_dt = o_xbc + CONV_DIM
    pad = jnp.zeros((D_MODEL, PK_WIDTH - PK_GD - SSM_HEADS - 3 * N_HEADS), F32)
    w_packed = jnp.concatenate([w[:, :o_gt], w[:, o_z:o_xbc], w[:, o_xbc:o_dt], w[:, o_dt:], w[:, o_gt:o_z], pad],
                               axis=1).astype(BF16)
    seg = np.arange(KV_WIDTH) // HEAD_DIM
    bd_kv = jnp.asarray((seg[:, None] == seg[None, :]).astype(np.float32), BF16)
    lane_pad = lambda v: jnp.pad(v.astype(F32), (0, LANES - v.shape[0])).reshape(1, LANES)
    wr = jnp.concatenate([w_rg[l], jnp.transpose(w_re[l], (1, 0, 2)).reshape(D_MODEL, N_EXPERTS)], axis=1)
    wr = jnp.pad(wr, ((0, 0), (0, LANES - wr.shape[1])))
    wr_hi = wr.astype(BF16)
    wr_lo = (wr - wr_hi.astype(F32)).astype(BF16)
    return {
        "g1": g_norm1[l].reshape(1, D_MODEL), "w_in": w_packed,
        "gq_col": (jnp.tile(g_q[l], N_HEADS) * (SCALE * LOG2E)).reshape(ATT_WIDTH, 1),
        "gks": jnp.tile(g_k_sel[l], N_KV).reshape(1, KV_WIDTH),
        "gkw": jnp.tile(g_k_win[l], N_KV).reshape(1, KV_WIDTH),
        "gkc": jnp.tile(g_k_cmp[l], N_KV).reshape(1, KV_WIDTH),
        "bd_kv": bd_kv,
        "wkv": jnp.concatenate([jnp.broadcast_to(w_pos_k[l][:, None], (CMP_BLOCK, KV_WIDTH)),
                                jnp.broadcast_to(w_pos_v[l][:, None], (CMP_BLOCK, KV_WIDTH))], axis=1),
        "pw_all": _pool_weights(w_pos_k[l], w_pos_v[l]),
        "conv_w": conv_w[l], "conv_b": conv_b[l].reshape(1, CONV_DIM),
        "dt_bias": lane_pad(dt_bias[l]), "a_log": lane_pad(a_log[l]), "d_skip": jnp.repeat(d_skip[l].astype(F32), SSM_HEAD_DIM).reshape(1, SSM_WIDTH),
        "g_ssm": g_ssm_out[l].reshape(1, SSM_WIDTH),
        "g_att": g_att_out[l].reshape(1, ATT_WIDTH), "w_out": w_out[l].astype(BF16),
        "g2": g_norm2[l].reshape(1, D_MODEL), "wr_hi": wr_hi, "wr_lo": wr_lo,
        "br": lane_pad(jnp.concatenate([b_rg[l], b_re[l].reshape(-1)])),
        "w_gate": w_gate[l].astype(BF16), "w_up": w_up[l].astype(BF16),
        "w_down": w_down[l].astype(BF16).reshape(N_EGROUPS, E_PER_GROUP * EXPERT_FF, D_MODEL),
    }


def _finish(att2d, y2d, x2d, mods, mod_index, tm_post, tm_moe, moe_index, wts):
    x1, h2, comb = _post(att2d, y2d, x2d, mods[2], mods[3], mods[4], mod_index, tm_post, wts)
    return _moe(h2, comb, x1, mods[5], moe_index, tm_moe, wts)


def _token_major_cache(t):
    b, _, n = t.shape
    return jnp.transpose(t.reshape(b, 2, N_KV, HEAD_DIM, n), (0, 4, 1, 2, 3))[None]


def _feature_major_cache(c):
    n, rows = c.shape[:2]
    return jnp.transpose(c, (0, 2, 3, 4, 1)).reshape(n, 2 * KV_WIDTH, rows)


def kernel(x_prompt, x_sample, cache_cmp, cache_sel, cache_win, state_ssm, state_conv, page_table, c_prompt, c_sample,
           g_norm1, g_norm2, w_ada, b_ada, w_in, g_q, g_k_cmp, g_k_sel, g_k_win, w_pos_k, w_pos_v, conv_w, conv_b,
           dt_bias, a_log, d_skip, g_att_out, g_ssm_out, w_out, w_rg, b_rg, w_re, b_re, w_gate, w_up, w_down):
    depth = w_in.shape[0]
    assert depth == 1
    l = 0
    bp, seq, d = x_prompt.shape
    bs, tdec, _ = x_sample.shape
    npages = page_table.shape[1]
    past_len = npages * PAGE_SIZE
    tp, ts = bp * seq, bs * tdec

    wts = _prep_weights(l, g_norm1, g_norm2, w_in, g_q, g_k_cmp, g_k_sel, g_k_win, w_pos_k, w_pos_v, conv_w, conv_b,
                        dt_bias, a_log, d_skip, g_att_out, g_ssm_out, w_out, w_rg, b_rg, w_re, b_re, w_gate, w_up,
                        w_down)
    m_all = _ada(jnp.concatenate([c_prompt, c_sample], axis=0), w_ada[l], b_ada[l])
    mods_p = [m.reshape(bp, 1, d) for m in jnp.split(m_all[:bp], 6, axis=-1)]
    tm_s = min(512, ts)
    mods_s = [jnp.repeat(m, tdec, axis=0).reshape(ts // tm_s, tm_s, d) for m in jnp.split(m_all[bp:], 6, axis=-1)]

    tm = 512
    per_seq = seq // tm
    xp2 = x_prompt.reshape(tp, d)
    qt, cmpt_p, selt_p, selk, selvt, wint_p, wink, winvt, z, xbc, gd, gdt, kc = _in_proj(
        xp2, bp, mods_p[0], mods_p[1], lambda i: (i // per_seq, 0, 0), tm, wts, True)
    ncb = seq // CMP_BLOCK
    nch = seq // KEY_CHUNK
    r3 = lambda a, n: a.reshape(bp, n, a.shape[-1])
    kc_eo = kc.reshape(bp, ncb // 2, 2, 2 * KV_WIDTH).transpose(0, 2, 1, 3).reshape(bp, ncb, 2 * KV_WIDTH)
    kcmp = kc_eo[:, :, :KV_WIDTH].astype(BF16)
    vcmp_t = jnp.transpose(kc_eo[:, :, KV_WIDTH:], (0, 2, 1)).astype(BF16)
    chunked = lambda a: a.reshape(bp, nch, KV_WIDTH, KEY_CHUNK)
    att = _attn_prompt(qt, r3(selk, seq), chunked(selvt), r3(wink, seq), chunked(winvt), kcmp, vcmp_t, gdt)
    y_ssm, ssm_p, conv_p = _ssd(r3(xbc, seq), r3(z, seq), r3(gd, seq), jnp.zeros((bp, SUBLANES, CONV_DIM), F32),
                                jnp.zeros((bp, SSM_HEADS, SSM_HEAD_DIM, SSM_STATE), F32), wts, SSD_CHUNK, SSD_CHUNK,
                                1)
    tm_moe = 512
    y_prompt = _finish(att.reshape(tp, ATT_WIDTH), y_ssm.reshape(tp, SSM_WIDTH), xp2, mods_p,
                       lambda i: (i // per_seq, 0, 0), tm, tm_moe, lambda i: (i // (seq // tm_moe), 0, 0), wts)

    xs2 = x_sample.reshape(ts, d)
    qt_s, cmpt_s, selt_s, _, _, wint_s, _, _, z, xbc, gd, _ = _in_proj(
        xs2, 1, mods_s[0], mods_s[1], lambda i: (i, 0, 0), tm_s, wts, False)
    nrow = N_HEADS * tdec
    qh = qt_s[0].T.reshape(bs, tdec, N_HEADS, HEAD_DIM).transpose(0, 2, 1, 3).reshape(bs, nrow, HEAD_DIM)
    g0 = (jnp.arange(nrow) < nrow // N_KV)[None, :, None]
    zero = jnp.zeros_like(qh)
    qbd = jnp.concatenate([jnp.where(g0, qh, zero), jnp.where(g0, zero, qh)], axis=-1)
    gt = gd[:, GD_GATE0:GD_GATE0 + 3 * N_HEADS].reshape(bs, tdec, 3, N_HEADS).transpose(0, 3, 1, 2)
    gt = jnp.pad(gt.reshape(bs, nrow, 3), ((0, 0), (0, 0), (0, SUBLANES - 3)))
    rows_s = lambda t: t[0].T.reshape(bs, tdec, 2 * KV_WIDTH)
    sel_s, win_s = rows_s(selt_s), rows_s(wint_s)
    new_pad = 2 * SUBLANES - tdec
    padded = lambda a: jnp.pad(a, ((0, 0), (0, new_pad), (0, 0)))
    cwin_t = _feature_major_cache(cache_win[l])
    att_s = _attn_sample(qbd, gt, page_table, _feature_major_cache(cache_cmp[l]), _feature_major_cache(cache_sel[l]),
                         cwin_t, padded(sel_s), padded(win_s), wts, past_len, tdec)
    att_s = att_s.reshape(bs, N_HEADS, tdec, HEAD_DIM).transpose(0, 2, 1, 3).reshape(ts, ATT_WIDTH)

    lpad = SUBLANES - tdec
    pad_rows = lambda a: jnp.pad(a.reshape(bs, tdec, a.shape[-1]), ((0, 0), (0, lpad), (0, 0)))
    cprev = jnp.pad(state_conv[l], ((0, 0), (SUBLANES - (CONV_W - 1), 0), (0, 0)))
    y_s, ssm_s, conv_s = _ssd(pad_rows(xbc), pad_rows(z), pad_rows(gd), cprev, state_ssm[l].astype(F32), wts,
                              SUBLANES, tdec, SSD_SAMPLE_BATCH)
    y_s = y_s[:, :tdec].reshape(ts, SSM_WIDTH)
    y_sample = _finish(att_s, y_s, xs2, mods_s, lambda i: (i, 0, 0), tm_s, tm_s, lambda i: (i, 0, 0), wts)

    kv6 = lambda a: a.reshape(1, bs, tdec, 2, N_KV, HEAD_DIM)
    wkeep = min(WINDOW, seq)
    wbuf = cwin_t.shape[2]
    wkeep_s = min(WINDOW, wbuf + tdec)
    win_all_t = jnp.concatenate([cwin_t, jnp.transpose(win_s, (0, 2, 1))], axis=2)[:, :, wbuf + tdec - wkeep_s:]
    tail = lambda a: a[:, SUBLANES - (CONV_W - 1):][None]
    return (y_prompt.reshape(bp, seq, d), y_sample.reshape(bs, tdec, d),
            _token_major_cache(cmpt_p), kv6(rows_s(cmpt_s)), _token_major_cache(selt_p), kv6(sel_s),
            _token_major_cache(wint_p[:, :, seq - wkeep:]), _token_major_cache(win_all_t),
            ssm_p[None], ssm_s[None].astype(state_ssm.dtype), tail(conv_p), tail(conv_s))
```

```python
import functools

import numpy as np
import jax
import jax.numpy as jnp
from jax import lax
from jax.experimental import pallas as pl
from jax.experimental.pallas import tpu as pltpu

F32 = jnp.float32
BF16 = jnp.bfloat16

D_MODEL = 1024
PAGE_SIZE = 128
N_HEADS = 8
N_KV = 2
HEAD_DIM = 64
Q_PER_KV = N_HEADS // N_KV
ATT_WIDTH = N_HEADS * HEAD_DIM
KV_WIDTH = N_KV * HEAD_DIM
CMP_BLOCK = 32
SEL_BLOCK = 64
N_SEL = 16
WINDOW = 512
SCALE = HEAD_DIM ** -0.5
SSM_HEADS = 8
SSM_HEAD_DIM = 64
SSM_WIDTH = SSM_HEADS * SSM_HEAD_DIM
SSM_GROUPS = 2
SSM_STATE = 64
CONV_W = 4
SSD_CHUNK = 128
CONV_DIM = SSM_WIDTH + 2 * SSM_GROUPS * SSM_STATE
N_EGROUPS = 4
E_PER_GROUP = 4
N_EXPERTS = N_EGROUPS * E_PER_GROUP
EXPERT_FF = 256
EPS = 1e-6
NEG = -1e30
BIG = 1e9
TINY = 1e-30
LOG2E = 1.4426950408889634

LANES = 128
SUBLANES = 8
VMEM_LIMIT = 48 * 1024 * 1024
Q_TILE = 128
KEY_CHUNK = 128
FLASH_UNROLL = 2
DEAD_SHIFT = 1 << 24
PAGE_SLOTS = 3
POOL_UNROLL = 16

PK_Q = 0
PK_CMP = 512
PK_SEL = 768
PK_WIN = 1024
PK_Z = 1280
PK_XBC = 1792
PK_GD = 2560
PK_WIDTH = 2688
GD_GATE0 = 8
RT_E0 = 4


def _dot(a, b):
    return jnp.dot(a, b, preferred_element_type=F32)


def _dot_nt(a, b):
    return lax.dot_general(a, b, (((1,), (1,)), ((), ())), preferred_element_type=F32)


def _dot_tn(a, b):
    return lax.dot_general(a, b, (((0,), (0,)), ((), ())), preferred_element_type=F32)


def _split2(x):
    hi = x.astype(BF16)
    lo = (x - hi.astype(F32)).astype(BF16)
    return hi, lo


def _split3(x):
    hi = x.astype(BF16)
    r = x - hi.astype(F32)
    mid = r.astype(BF16)
    lo = (r - mid.astype(F32)).astype(BF16)
    return hi, mid, lo


def _dot_exact_lhs(a_bf16, x):
    return sum(_dot(a_bf16, p) for p in _split3(x))


def _dot_exact_rhs(x, b_bf16):
    return sum(_dot(p, b_bf16) for p in _split3(x))


def _sigmoid(x):
    return 1.0 / (1.0 + jnp.exp(-x))


def _silu(x):
    return x * _sigmoid(x)


def _softplus(x):
    return jnp.maximum(x, 0.0) + jnp.log1p(jnp.exp(-jnp.abs(x)))


def _rms(x):
    return x * lax.rsqrt(jnp.mean(x * x, axis=-1, keepdims=True) + EPS)


def _head_rms(x, bd):
    hi, lo = _split2(x * x)
    ms = (_dot(hi, bd) + _dot(lo, bd)) * (1.0 / HEAD_DIM)
    return x * lax.rsqrt(ms + EPS)


def _topk_mask_lanes(score, nblk, k):
    lane = lax.broadcasted_iota(jnp.int32, score.shape, 1)
    rank = jnp.zeros(score.shape, F32)
    for i in range(nblk):
        si = score[:, i:i + 1]
        beats = jnp.where(si > score, 1.0, jnp.where(si == score, jnp.where(lane > i, 1.0, 0.0), 0.0))
        rank = rank + beats
    return jnp.where(rank < k, 1.0, 0.0)


def _topk_mask_rows(score, k):
    row = lax.broadcasted_iota(jnp.int32, score.shape, 0)
    rank = jnp.zeros(score.shape, F32)
    for i in range(score.shape[0]):
        si = score[i:i + 1, :]
        beats = jnp.where(si > score, 1.0, jnp.where(si == score, jnp.where(row > i, 1.0, 0.0), 0.0))
        rank = rank + beats
    return jnp.where(rank < k, 1.0, 0.0)


def _params(*sem):
    return pltpu.CompilerParams(dimension_semantics=sem, vmem_limit_bytes=VMEM_LIMIT)


def _ada_body(c_ref, w_ref, b_ref, o_ref):
    c = _silu(c_ref[...]).astype(BF16)
    o_ref[...] = _dot(c, w_ref[...].astype(BF16)) + b_ref[...]


def _ada(c_all, w_ada, b_ada):
    n, d = c_all.shape
    width = w_ada.shape[1]
    tn = 1536
    return pl.pallas_call(
        _ada_body,
        grid=(width // tn,),
        in_specs=[pl.BlockSpec((n, d), lambda j: (0, 0)),
                  pl.BlockSpec((d, tn), lambda j: (0, j)),
                  pl.BlockSpec((1, tn), lambda j: (0, j))],
        out_specs=pl.BlockSpec((n, tn), lambda j: (0, j)),
        out_shape=jax.ShapeDtypeStruct((n, width), F32),
        compiler_params=_params("arbitrary"),
        name="ada",
    )(c_all, w_ada, b_ada.reshape(1, width))


def _inproj_body(with_cmp, x_ref, sh_ref, sc_ref, g1_ref, w_ref, gqc_ref, gks_ref, gkw_ref, bd_ref, *rest):
    if with_cmp:
        wkv_ref, gkc_ref = rest[:2]
        rest = rest[2:]
    (qt_ref, cmpt_ref, selt_ref, selk_ref, selvt_ref, wint_ref, wink_ref, winvt_ref,
     z_ref, xbc_ref, gd_ref, gdt_ref) = rest[:12]
    tm = x_ref.shape[0]
    h = _rms(x_ref[...]) * g1_ref[...]
    hb = (h * (1.0 + sc_ref[0]) + sh_ref[0]).astype(BF16)
    bd_kv = bd_ref[...]

    q3 = _dot(hb, w_ref[:, PK_Q:PK_CMP]).T.reshape(N_HEADS, HEAD_DIM, tm)
    q3 = q3 * lax.rsqrt(jnp.mean(q3 * q3, axis=1, keepdims=True) + EPS)
    qt_ref[...] = (q3.reshape(ATT_WIDTH, tm) * gqc_ref[...]).astype(BF16)

    kv = _dot(hb, w_ref[:, PK_CMP:PK_SEL])
    cmpt_ref[...] = kv.T
    if with_cmp:
        kc_ref = rest[12]
        pooled = (kv.reshape(tm // CMP_BLOCK, CMP_BLOCK, 2 * KV_WIDTH) * wkv_ref[...][None]).sum(axis=1)
        kc_ref[:, 0:KV_WIDTH] = _head_rms(pooled[:, :KV_WIDTH], bd_kv) * gkc_ref[...]
        kc_ref[:, KV_WIDTH:] = pooled[:, KV_WIDTH:]

    def normed_kv(lo, hi, gain_ref, t_ref, k_ref, vt_ref):
        kv = _dot(hb, w_ref[:, lo:hi])
        kn = _head_rms(kv[:, :KV_WIDTH], bd_kv) * gain_ref[...]
        k_ref[...] = kn.astype(BF16)
        kvt = jnp.concatenate([kn, kv[:, KV_WIDTH:]], axis=1).T
        t_ref[...] = kvt
        vt = kvt[KV_WIDTH:, :].astype(BF16)
        for j in range(tm // KEY_CHUNK):
            vt_ref[j] = vt[:, j * KEY_CHUNK:(j + 1) * KEY_CHUNK]

    normed_kv(PK_SEL, PK_WIN, gks_ref, selt_ref, selk_ref, selvt_ref)
    normed_kv(PK_WIN, PK_Z, gkw_ref, wint_ref, wink_ref, winvt_ref)

    z_ref[...] = _dot(hb, w_ref[:, PK_Z:PK_XBC])
    xbc_ref[...] = _dot(hb, w_ref[:, PK_XBC:PK_GD])
    gd = _dot(hb, w_ref[:, PK_GD:PK_WIDTH])
    gd_ref[...] = gd
    gdt_ref[...] = gd.T


def _in_proj(x2d, nbatch, shift3, scale3, mod_index, tm, wts, with_cmp):
    t, d = x2d.shape
    seq = t // nbatch
    per_seq = seq // tm
    rmod = shift3.shape[1]
    full = lambda a: pl.BlockSpec(a.shape, lambda i: (0,) * a.ndim)
    row = lambda w: pl.BlockSpec((tm, w), lambda i: (i, 0))
    fmaj = lambda w: pl.BlockSpec((None, w, tm), lambda i: (i // per_seq, 0, i % per_seq))
    chunks = pl.BlockSpec((tm // KEY_CHUNK, KV_WIDTH, KEY_CHUNK), lambda i: (i, 0, 0))
    mod = pl.BlockSpec((1, rmod, d), mod_index)
    ins = [x2d, shift3, scale3, wts["g1"], wts["w_in"], wts["gq_col"], wts["gks"], wts["gkw"], wts["bd_kv"]]
    in_specs = [row(d), mod, mod] + [full(a) for a in ins[3:]]
    if with_cmp:
        ins += [wts["wkv"], wts["gkc"]]
        in_specs += [full(wts["wkv"]), full(wts["gkc"])]
    sds = jax.ShapeDtypeStruct
    fm = lambda w, dt: sds((nbatch, w, seq), dt)
    ck = sds((t // KEY_CHUNK, KV_WIDTH, KEY_CHUNK), BF16)
    out_shape = [fm(ATT_WIDTH, BF16), fm(2 * KV_WIDTH, F32),
                 fm(2 * KV_WIDTH, F32), sds((t, KV_WIDTH), BF16), ck,
                 fm(2 * KV_WIDTH, F32), sds((t, KV_WIDTH), BF16), ck,
                 sds((t, SSM_WIDTH), F32), sds((t, CONV_DIM), F32), sds((t, LANES), F32), fm(LANES, F32)]
    out_specs = [fmaj(ATT_WIDTH), fmaj(2 * KV_WIDTH),
                 fmaj(2 * KV_WIDTH), row(KV_WIDTH), chunks,
                 fmaj(2 * KV_WIDTH), row(KV_WIDTH), chunks,
                 row(SSM_WIDTH), row(CONV_DIM), row(LANES), fmaj(LANES)]
    if with_cmp:
        out_shape.append(sds((t // CMP_BLOCK, 2 * KV_WIDTH), F32))
        out_specs.append(pl.BlockSpec((tm // CMP_BLOCK, 2 * KV_WIDTH), lambda i: (i, 0)))
    return pl.pallas_call(
        functools.partial(_inproj_body, with_cmp),
        grid=(t // tm,),
        in_specs=in_specs,
        out_specs=out_specs,
        out_shape=out_shape,
        compiler_params=_params("arbitrary"),
        name="in_proj",
    )(*ins)


def _attn_prompt_body(qt_ref, selk_ref, selvt_ref, wink_ref, winvt_ref, kc_ref, vct_ref, gdt_ref, o_ref, selm_scr,
                      *, tq, seq):
    i = pl.program_id(1)
    ncb = seq // CMP_BLOCK
    nblk = seq // SEL_BLOCK
    bpc = KEY_CHUNK // SEL_BLOCK
    wide = Q_PER_KV * tq
    qt = qt_ref[...]
    t_row = i * tq + lax.broadcasted_iota(jnp.int32, (1, tq), 1)
    cur_row = t_row // SEL_BLOCK
    key_i = lax.broadcasted_iota(jnp.int32, (KEY_CHUNK, tq), 0)
    gates = _sigmoid(gdt_ref[GD_GATE0:GD_GATE0 + 3 * N_HEADS, :])
    rep = lambda a: jnp.concatenate([a] * Q_PER_KV, axis=1)

    def flash(k_ref, vt_ref, qgts, c_lo, c_hi, masks_fn, groups):
        last = seq // KEY_CHUNK - 1

        def step(j, carry):
            chunks = []
            for u in range(FLASH_UNROLL):
                c = c_lo + FLASH_UNROLL * j + u
                cc = jnp.minimum(c, last)
                off = pl.multiple_of(cc * KEY_CHUNK, KEY_CHUNK)
                scores = {g: _dot(k_ref[pl.ds(off, KEY_CHUNK), g * HEAD_DIM:(g + 1) * HEAD_DIM], qgts[g])
                          for g in groups}
                chunks.append((c, cc, scores))
            state = dict(zip(groups, carry))
            for c, cc, scores in chunks:
                masks = masks_fn(c, cc, c < c_hi)
                for g in groups:
                    m, l, acc = state[g]
                    s = jnp.where(rep(masks[g]), scores[g], NEG)
                    m_new = jnp.maximum(m, s.max(axis=0, keepdims=True))
                    alpha = jnp.exp2(m - m_new)
                    e = jnp.exp2(s - m_new)
                    pv = _dot(vt_ref[cc, g * HEAD_DIM:(g + 1) * HEAD_DIM, :], e.astype(BF16))
                    state[g] = (m_new, alpha * l + e.sum(axis=0, keepdims=True), alpha * acc + pv)
            return tuple(state[g] for g in groups)

        init = (jnp.full((1, wide), NEG, F32), jnp.zeros((1, wide), F32), jnp.zeros((HEAD_DIM, wide), F32))
        trips = (c_hi - c_lo + FLASH_UNROLL - 1) // FLASH_UNROLL
        res = lax.fori_loop(0, trips, step, (init,) * len(groups))
        return [acc / jnp.maximum(l, TINY) for _, l, acc in res]

    qgts, o_cmps = [], []
    for g in range(N_KV):
        qgt = jnp.concatenate(
            [qt[(g * Q_PER_KV + h) * HEAD_DIM:(g * Q_PER_KV + h + 1) * HEAD_DIM, :] for h in range(Q_PER_KV)], axis=1)
        qgts.append(qgt)

        kcg = kc_ref[:, g * HEAD_DIM:(g + 1) * HEAD_DIM]
        vctg = vct_ref[g * HEAD_DIM:(g + 1) * HEAD_DIM, :]
        r = lax.broadcasted_iota(jnp.int32, (ncb, tq), 0)
        cblk = jnp.where(r < nblk, 2 * r, 2 * (r - nblk) + 1)
        cmask = rep((cblk * CMP_BLOCK + CMP_BLOCK - 1) <= t_row)
        s = jnp.where(cmask, _dot(kcg, qgt), NEG)
        e = jnp.where(cmask, jnp.exp2(s - s.max(axis=0, keepdims=True)), 0.0)
        p = e / jnp.maximum(e.sum(axis=0, keepdims=True), TINY)
        o_cmps.append(_dot(vctg, p.astype(BF16)))

        imp = sum(p[:, h * tq:(h + 1) * tq] for h in range(Q_PER_KV))
        imp = imp[:nblk] + imp[nblk:]
        jb = lax.broadcasted_iota(jnp.int32, (nblk, tq), 0)
        allowed = jb < cur_row
        forced = (jb == 0) | (jb == cur_row - 1)
        score = jnp.where(allowed, jnp.where(forced, BIG, imp), NEG)
        selm = jnp.where(allowed, _topk_mask_rows(score, min(N_SEL - 1, nblk)), 0.0)
        for c in range(seq // KEY_CHUNK):
            selm_scr[g, c, 0:bpc] = selm[c * bpc:(c + 1) * bpc]

    def key_pos(c, live):
        return c * KEY_CHUNK + key_i + jnp.where(live, 0, DEAD_SHIFT)

    def sel_masks(c, cc, live):
        kp = key_pos(c, live)
        cur_blk = ((kp // SEL_BLOCK) == cur_row) & (kp <= t_row)
        alive = jnp.where(live, 1.0, 0.0)
        out = []
        for g in range(N_KV):
            rows = selm_scr[g, cc] * alive
            picked = jnp.concatenate(
                [jnp.broadcast_to(rows[j:j + 1], (SEL_BLOCK, tq)) for j in range(bpc)], axis=0) > 0.5
            out.append(picked | cur_blk)
        return out

    def win_masks(c, cc, live):
        rel = t_row - key_pos(c, live)
        return [(rel >= 0) & (rel < WINDOW)] * N_KV

    c_hi = ((i + 1) * tq + KEY_CHUNK - 1) // KEY_CHUNK
    c_win = jnp.maximum(i * tq - WINDOW + 1, 0) // KEY_CHUNK
    both = tuple(range(N_KV))
    o_sels = flash(selk_ref, selvt_ref, qgts, 0, c_hi, sel_masks, both)
    o_wins = flash(wink_ref, winvt_ref, qgts, c_win, c_hi, win_masks, both)

    heads = []
    for g in range(N_KV):
        for h in range(Q_PER_KV):
            hh = g * Q_PER_KV + h
            sl = slice(h * tq, (h + 1) * tq)
            gate = lambda br: gates[br * N_HEADS + hh:br * N_HEADS + hh + 1, :]
            heads.append(gate(0) * o_cmps[g][:, sl] + gate(1) * o_sels[g][:, sl] + gate(2) * o_wins[g][:, sl])
    o_ref[...] = jnp.concatenate(heads, axis=0).T


def _attn_prompt(qt, selk, selvt, wink, winvt, kc, vct, gdt):
    b, _, s = qt.shape
    tq = Q_TILE
    ncb = s // CMP_BLOCK
    nch = s // KEY_CHUNK
    tile_t = lambda w: pl.BlockSpec((None, w, tq), lambda bi, i: (bi, 0, i))
    whole = lambda shape: pl.BlockSpec((None,) + shape, lambda bi, i: (bi,) + (0,) * len(shape))
    return pl.pallas_call(
        functools.partial(_attn_prompt_body, tq=tq, seq=s),
        grid=(b, s // tq),
        in_specs=[tile_t(ATT_WIDTH), whole((s, KV_WIDTH)), whole((nch, KV_WIDTH, KEY_CHUNK)),
                  whole((s, KV_WIDTH)), whole((nch, KV_WIDTH, KEY_CHUNK)),
                  whole((ncb, KV_WIDTH)), whole((KV_WIDTH, ncb)), tile_t(LANES)],
        out_specs=pl.BlockSpec((None, tq, ATT_WIDTH), lambda bi, i: (bi, i, 0)),
        out_shape=jax.ShapeDtypeStruct((b, s, ATT_WIDTH), F32),
        scratch_shapes=[pltpu.VMEM((N_KV, nch, SUBLANES, tq), F32)],
        compiler_params=_params("arbitrary", "arbitrary"),
        name="attn_prompt",
    )(qt, selk, selvt, wink, winvt, kc, vct, gdt)


def _ssd_body(xbc_ref, z_ref, gd_ref, cprev_ref, h0_ref, cw_ref, cb_ref, dtb_ref, alog_ref, dsk_ref, gout_ref,
              hexp_ref, y_ref, hout_ref, cout_ref, h_scr, xe_scr, yoff_scr, *, cl):
    c = pl.program_id(1)
    gn = SSM_GROUPS * SSM_STATE
    hpg = SSM_HEADS // SSM_GROUPS

    @pl.when(c == 0)
    def _():
        h_scr[...] = h0_ref[...]
        xe_scr[0:SUBLANES] = cprev_ref[...]

    xe_scr[SUBLANES:SUBLANES + cl] = xbc_ref[...]
    base = SUBLANES - (CONV_W - 1)
    xc = cb_ref[...]
    for w in range(CONV_W):
        xc = xc + cw_ref[w:w + 1, :] * xe_scr[base + w:base + w + cl, :]
    tail = xe_scr[cl:cl + SUBLANES, :]
    cout_ref[...] = tail
    xe_scr[0:SUBLANES] = tail
    xc = _silu(xc)
    xs = xc[:, :SSM_WIDTH]
    bmb = xc[:, SSM_WIDTH:SSM_WIDTH + gn].astype(BF16)
    cmb = xc[:, SSM_WIDTH + gn:].astype(BF16)

    dt = _softplus(gd_ref[...] + dtb_ref[...])
    da = dt * (-jnp.exp(alog_ref[...]))
    ri = lax.broadcasted_iota(jnp.int32, (cl, cl), 0)
    ci = lax.broadcasted_iota(jnp.int32, (cl, cl), 1)
    causal = ri >= ci
    acs = _dot_exact_lhs(jnp.where(causal, 1.0, 0.0).astype(BF16), da)
    acs_t = acs.T
    acs_last = acs[cl - 1:cl, :]
    e_last = jnp.exp(acs_last)
    hexp = hexp_ref[...]
    dt_w = _dot_exact_rhs(dt, hexp)
    wend_w = _dot_exact_rhs(jnp.exp(acs_last - acs) * dt, hexp)
    eacs_w = _dot_exact_rhs(jnp.exp(acs), hexp)
    x_dt = (xs * dt_w).astype(BF16)
    x_we = (xs * wend_w).astype(BF16)

    for g in range(SSM_GROUPS):
        bg = bmb[:, g * SSM_STATE:(g + 1) * SSM_STATE]
        cg = cmb[:, g * SSM_STATE:(g + 1) * SSM_STATE]
        cb = _dot_nt(cg, bg)
        for hq in range(hpg):
            h = g * hpg + hq
            hs = slice(h * SSM_HEAD_DIM, (h + 1) * SSM_HEAD_DIM)
            seg = acs[:, h:h + 1] - acs_t[h:h + 1, :]
            decay = jnp.where(causal, jnp.exp(jnp.where(causal, seg, 0.0)), 0.0)
            y_ref[:, hs] = _dot((cb * decay).astype(BF16), x_dt[:, hs])
            hst = h_scr[h]
            yoff_scr[:, hs] = _dot_nt(cg, hst.astype(BF16))
            h_scr[h] = e_last[:, h:h + 1] * hst + _dot_tn(x_we[:, hs], bg)

    y = y_ref[...] + yoff_scr[...] * eacs_w + dsk_ref[...] * xs
    yg = y * _silu(z_ref[...])
    y_ref[...] = _rms(yg) * gout_ref[...]
    hout_ref[...] = h_scr[...]


def _ssd(xbc3, z3, gd3, cprev, h0, wts):
    b, l, _ = xbc3.shape
    cl = SSD_CHUNK
    assert cl == LANES and l % cl == 0
    nc = l // cl
    tile = lambda w: pl.BlockSpec((None, cl, w), lambda bi, c: (bi, c, 0))
    full = lambda a: pl.BlockSpec(a.shape, lambda bi, c: (0,) * a.ndim)
    names = ["conv_w", "conv_b", "dt_bias", "a_log", "d_skip", "g_ssm"]
    hexp = (np.arange(LANES)[:, None] == np.arange(SSM_WIDTH)[None, :] // SSM_HEAD_DIM).astype(np.float32)
    consts = [wts[n] for n in names] + [jnp.asarray(hexp, BF16)]
    hshape = (SSM_HEADS, SSM_HEAD_DIM, SSM_STATE)
    return pl.pallas_call(
        functools.partial(_ssd_body, cl=cl),
        grid=(b, nc),
        in_specs=[tile(CONV_DIM), tile(SSM_WIDTH), tile(LANES),
                  pl.BlockSpec((None, SUBLANES, CONV_DIM), lambda bi, c: (bi, 0, 0)),
                  pl.BlockSpec((None,) + hshape, lambda bi, c: (bi, 0, 0, 0))] + [full(a) for a in consts],
        out_specs=[tile(SSM_WIDTH),
                   pl.BlockSpec((None,) + hshape, lambda bi, c: (bi, 0, 0, 0)),
                   pl.BlockSpec((None, SUBLANES, CONV_DIM), lambda bi, c: (bi, 0, 0))],
        out_shape=[jax.ShapeDtypeStruct((b, l, SSM_WIDTH), F32),
                   jax.ShapeDtypeStruct((b,) + hshape, F32),
                   jax.ShapeDtypeStruct((b, SUBLANES, CONV_DIM), F32)],
        scratch_shapes=[pltpu.VMEM(hshape, F32), pltpu.VMEM((SUBLANES + cl, CONV_DIM), F32),
                        pltpu.VMEM((cl, SSM_WIDTH), F32)],
        compiler_params=_params("arbitrary", "arbitrary"),
        name="ssd",
    )(xbc3, z3, gd3, cprev, h0, *consts)


def _ssd_decode_body(xbc_ref, cprev_ref, z_ref, dt_ref, h_ref, cw_ref, cb_ref, dtb_ref, alog_ref, dsk_ref, gout_ref,
                     y_ref, hout_ref, xc_scr, dtv_scr, dec_scr, *, steps):
    hd = pl.program_id(0)
    gn = SSM_GROUPS * SSM_STATE
    hpg = SSM_HEADS // SSM_GROUPS

    @pl.when(hd == 0)
    def _():
        rows = [cprev_ref[j] for j in range(CONV_W - 1)] + [xbc_ref[t] for t in range(steps)]
        for t in range(steps):
            xc = cb_ref[...]
            for w in range(CONV_W):
                xc = xc + cw_ref[w] * rows[t + w]
            xc_scr[t] = _silu(xc)
        dt = _softplus(dt_ref[...] + dtb_ref[...][None])
        dtv_scr[...] = dt
        dec_scr[...] = jnp.exp(dt * (-jnp.exp(alog_ref[...]))[None])

    x_rows = pl.ds(pl.multiple_of(hd * SSM_HEAD_DIM, SSM_HEAD_DIM), SSM_HEAD_DIM)
    grp = hd // hpg
    b_rows = pl.ds(pl.multiple_of(SSM_WIDTH + grp * SSM_STATE, SSM_STATE), SSM_STATE)
    c_rows = pl.ds(pl.multiple_of(SSM_WIDTH + gn + grp * SSM_STATE, SSM_STATE), SSM_STATE)
    xs = [xc_scr[t, x_rows, :] for t in range(steps)]
    bs_ = [xc_scr[t, b_rows, :] for t in range(steps)]
    cs = [xc_scr[t, c_rows, :] for t in range(steps)]
    dec = [dec_scr[t, pl.ds(hd, 1), :] for t in range(steps)]
    xdt = [xs[t] * dtv_scr[t, pl.ds(hd, 1), :] for t in range(steps)]
    dsk = dsk_ref[x_rows, :]
    for p in range(SSM_HEAD_DIM):
        hp = h_ref[p]
        for t in range(steps):
            hp = dec[t] * hp + xdt[t][p:p + 1, :] * bs_[t]
            y_ref[t, pl.ds(hd * SSM_HEAD_DIM + p, 1), :] = jnp.sum(hp * cs[t], axis=0, keepdims=True)
        hout_ref[p] = hp
    for t in range(steps):
        y_ref[t, x_rows, :] = y_ref[t, x_rows, :] + dsk * xs[t]

    @pl.when(hd == SSM_HEADS - 1)
    def _():
        yg = y_ref[...] * _silu(z_ref[...])
        ms = jnp.mean(yg * yg, axis=1, keepdims=True)
        y_ref[...] = yg * lax.rsqrt(ms + EPS) * gout_ref[...][None]


def _ssd_decode(xbc_t, cprev_t, z_t, dt_t, h_t, wts):
    steps, _, nb = xbc_t.shape
    lanes = lambda v: jnp.broadcast_to(v.astype(F32)[..., None], v.shape + (nb,))
    consts = [lanes(wts["conv_w"]), lanes(wts["conv_b"][0]), lanes(wts["dt_bias"][0, :SSM_HEADS]),
              lanes(wts["a_log"][0, :SSM_HEADS]), lanes(wts["d_skip"][0]), lanes(wts["g_ssm"][0])]
    full = lambda a: pl.BlockSpec(a.shape, lambda hd: (0,) * a.ndim)
    per_head = pl.BlockSpec((None, SSM_HEAD_DIM, SSM_STATE, nb), lambda hd: (hd, 0, 0, 0))
    ins = [xbc_t, cprev_t, z_t, dt_t]
    return pl.pallas_call(
        functools.partial(_ssd_decode_body, steps=steps),
        grid=(SSM_HEADS,),
        in_specs=[full(a) for a in ins] + [per_head] + [full(a) for a in consts],
        out_specs=[full(z_t), per_head],
        out_shape=[jax.ShapeDtypeStruct(z_t.shape, F32), jax.ShapeDtypeStruct(h_t.shape, F32)],
        scratch_shapes=[pltpu.VMEM(xbc_t.shape, F32), pltpu.VMEM(dt_t.shape, F32), pltpu.VMEM(dt_t.shape, F32)],
        compiler_params=_params("arbitrary"),
        name="ssd_decode",
    )(*ins, h_t, *consts)


def _post_body(att_ref, y_ref, x_ref, g1m_ref, sh2_ref, sc2_ref, gatt_ref, wo_ref, gn2_ref, wrh_ref, wrl_ref, br_ref,
               x1_ref, h2_ref, comb_ref):
    an = (_rms(att_ref[...]) * gatt_ref[...]).astype(BF16)
    mixed = _dot(an, wo_ref[0:ATT_WIDTH, :]) + _dot(y_ref[...].astype(BF16), wo_ref[ATT_WIDTH:, :])
    x1 = x_ref[...] + g1m_ref[0] * mixed
    x1_ref[...] = x1
    h2 = _rms(x1) * gn2_ref[...] * (1.0 + sc2_ref[0]) + sh2_ref[0]
    h2_ref[...] = h2.astype(BF16)

    hh, hl = _split2(h2)
    logit = _dot(hh, wrh_ref[...]) + _dot(hl, wrh_ref[...]) + _dot(hh, wrl_ref[...]) + br_ref[...]
    lane = lax.broadcasted_iota(jnp.int32, logit.shape, 1).astype(F32)
    first = lambda cond: jnp.min(jnp.where(cond, lane, 1e6), axis=-1, keepdims=True)
    top = lambda cond: jnp.max(jnp.where(cond, logit, NEG), axis=-1, keepdims=True)
    is_g = lane < N_EGROUPS
    mg = top(is_g)
    gsel = first(is_g & (logit == mg))
    p_top = 1.0 / jnp.sum(jnp.where(is_g, jnp.exp(logit - mg), 0.0), axis=-1, keepdims=True)
    lo = RT_E0 + E_PER_GROUP * gsel
    is_e = (lane >= lo) & (lane < lo + E_PER_GROUP)
    v1 = top(is_e)
    i1 = first(is_e & (logit == v1))
    is_e2 = is_e & (lane != i1)
    v2 = top(is_e2)
    i2 = first(is_e2 & (logit == v2))
    e2 = jnp.exp(v2 - v1)
    comb_ref[...] = (jnp.where(lane == i1, p_top / (1.0 + e2), 0.0)
                     + jnp.where(lane == i2, p_top * e2 / (1.0 + e2), 0.0))


def _post(att2d, y2d, x2d, gate1, shift2, scale2, mod_index, tm, wts):
    t, d = x2d.shape
    rmod = gate1.shape[1]
    row = lambda w: pl.BlockSpec((tm, w), lambda i: (i, 0))
    mod = pl.BlockSpec((1, rmod, d), mod_index)
    full = lambda a: pl.BlockSpec(a.shape, lambda i: (0,) * a.ndim)
    consts = [wts["g_att"], wts["w_out"], wts["g2"], wts["wr_hi"], wts["wr_lo"], wts["br"]]
    return pl.pallas_call(
        _post_body,
        grid=(t // tm,),
        in_specs=[row(ATT_WIDTH), row(SSM_WIDTH), row(d), mod, mod, mod] + [full(a) for a in consts],
        out_specs=[row(d), row(d), row(LANES)],
        out_shape=[jax.ShapeDtypeStruct((t, d), F32), jax.ShapeDtypeStruct((t, d), BF16),
                   jax.ShapeDtypeStruct((t, LANES), F32)],
        compiler_params=_params("arbitrary"),
        name="post",
    )(att2d, y2d, x2d, gate1, shift2, scale2, *consts)


def _moe_body(h_ref, wg_ref, wu_ref, wd_ref, comb_ref, x1_ref, g2m_ref, o_ref, acc_ref):
    g = pl.program_id(1)

    @pl.when(g == 0)
    def _():
        acc_ref[...] = jnp.zeros_like(acc_ref)

    h = h_ref[...]
    comb = comb_ref[...]
    lane = lax.broadcasted_iota(jnp.int32, comb.shape, 1)
    first = RT_E0 + E_PER_GROUP * g
    he = []
    for j in range(E_PER_GROUP):
        cw = jnp.sum(jnp.where(lane == first + j, comb, 0.0), axis=-1, keepdims=True)
        he.append((_silu(_dot(h, wg_ref[j])) * _dot(h, wu_ref[j]) * cw).astype(BF16))
    acc_ref[...] += _dot(jnp.concatenate(he, axis=1), wd_ref[...])

    @pl.when(g == N_EGROUPS - 1)
    def _():
        o_ref[...] = x1_ref[...] + g2m_ref[0] * acc_ref[...]


def _moe(h2, comb, x1, gate2, mod_index, tm, wts):
    t, d = x1.shape
    rmod = gate2.shape[1]
    gff = E_PER_GROUP * EXPERT_FF
    row = lambda w: pl.BlockSpec((tm, w), lambda i, g: (i, 0))
    return pl.pallas_call(
        _moe_body,
        grid=(t // tm, N_EGROUPS),
        in_specs=[row(d),
                  pl.BlockSpec((E_PER_GROUP, d, EXPERT_FF), lambda i, g: (g, 0, 0)),
                  pl.BlockSpec((E_PER_GROUP, d, EXPERT_FF), lambda i, g: (g, 0, 0)),
                  pl.BlockSpec((None, gff, d), lambda i, g: (g, 0, 0)),
                  row(LANES), row(d),
                  pl.BlockSpec((1, rmod, d), lambda i, g: mod_index(i))],
        out_specs=row(d),
        out_shape=jax.ShapeDtypeStruct((t, d), F32),
        scratch_shapes=[pltpu.VMEM((tm, d), F32)],
        compiler_params=_params("arbitrary", "arbitrary"),
        name="moe",
    )(h2, wts["w_gate"], wts["w_up"], wts["w_down"], comb, x1, gate2)


def _page_fetch(pt_ref, cache_ref, buf, sem, npages):
    b = pl.program_id(0)
    nb = pl.num_programs(0)
    slot = lax.rem(b, PAGE_SLOTS)
    ahead = PAGE_SLOTS - 1

    def start_all(bb, unrolled):
        sl = lax.rem(bb, PAGE_SLOTS)

        def start(p):
            pltpu.make_async_copy(cache_ref.at[pt_ref[bb, p]], buf.at[sl, p], sem.at[sl]).start()

        if unrolled:
            for p in range(npages):
                start(p)
        else:
            lax.fori_loop(0, npages, lambda p, carry: (start(p), carry)[1], 0)

    for first in range(ahead):
        @pl.when((b == 0) & (first < nb))
        def _():
            start_all(first, False)

    @pl.when(b + ahead < nb)
    def _():
        start_all(b + ahead, True)

    for p in range(npages):
        pltpu.make_async_copy(cache_ref.at[0], buf.at[slot, p], sem.at[slot]).wait()
    return slot


def _cmp_sample_body(pt_ref, cache_ref, qbd_ref, pw_all_ref, gkc_ref, bd_ref, pair_ref, rsum_ref, rexp_ref,
                     ocmp_ref, mask_ref, buf, sem, kc_scr, *, npages, past_len, tdec):
    slot = _page_fetch(pt_ref, cache_ref, buf, sem, npages)
    nblk = past_len // SEL_BLOCK
    ppi = SUBLANES // (PAGE_SIZE // CMP_BLOCK)
    kw = KV_WIDTH

    tiles = min(POOL_UNROLL, npages // ppi)

    def pool(i, carry):
        for u in range(tiles):
            kc8 = jnp.zeros((SUBLANES, kw), F32)
            vc8 = jnp.zeros((SUBLANES, kw), F32)
            for j in range(ppi):
                r1 = _dot_nt(pw_all_ref[j], buf[slot, (i * tiles + u) * ppi + j].astype(BF16))
                s8 = SUBLANES
                kc8 = kc8 + r1[0:s8, :kw] + r1[s8:2 * s8, :kw]
                vc8 = vc8 + r1[2 * s8:3 * s8, kw:] + r1[3 * s8:, kw:]
            rows = pl.ds(pl.multiple_of((i * tiles + u) * SUBLANES, SUBLANES), SUBLANES)
            kc_scr[rows, 0:kw] = kc8
            kc_scr[rows, kw:] = vc8
        return carry
    lax.fori_loop(0, npages // (ppi * tiles), pool, 0)

    kcn = _head_rms(kc_scr[:, 0:kw], bd_ref[...]) * gkc_ref[...]
    vc = kc_scr[:, kw:]
    s = _dot_nt(qbd_ref[...], kcn.astype(BF16))
    e = jnp.exp2(s - s.max(axis=-1, keepdims=True))
    p = e / jnp.maximum(e.sum(axis=-1, keepdims=True), TINY)
    o = _dot(p.astype(BF16), vc.astype(BF16))
    nrow = o.shape[0]
    first_group = lax.broadcasted_iota(jnp.int32, (nrow, HEAD_DIM), 0) < nrow // N_KV
    ocmp_ref[...] = jnp.where(first_group, o[:, :HEAD_DIM], o[:, HEAD_DIM:])

    imp = _dot_exact_rhs(_dot_exact_lhs(rsum_ref[...], p), pair_ref[...])
    lane = lax.broadcasted_iota(jnp.int32, imp.shape, 1)
    tok = lax.rem(lax.broadcasted_iota(jnp.int32, imp.shape, 0), tdec)
    cur = (past_len + tok) // SEL_BLOCK
    allowed = (lane < cur) & (lane < nblk)
    forced = (lane == 0) | (lane == cur - 1)
    score = jnp.where(allowed, jnp.where(forced, BIG, imp), NEG)
    selm = jnp.where(allowed, _topk_mask_lanes(score, nblk, min(N_SEL - 1, nblk)), 0.0)
    mask_ref[...] = _dot(rexp_ref[...], selm.astype(BF16))


def _sel_sample_body(pt_ref, cache_ref, qbd_ref, mask_ref, eexp_ref, selnew_ref, winc_ref, winnew_ref, gt_ref,
                     ocmp_ref, o_ref, buf, sem, *, npages, past_len, tdec, ppc):
    slot = _page_fetch(pt_ref, cache_ref, buf, sem, npages)
    qbd = qbd_ref[...]
    nrow = qbd.shape[0]
    tok = lax.rem(lax.broadcasted_iota(jnp.int32, (nrow, 1), 0), tdec)
    kch = ppc * PAGE_SIZE
    kw = KV_WIDTH

    def attend(pieces):
        ss = [jnp.where(mask, s, NEG) for s, mask, _ in pieces]
        m = functools.reduce(jnp.maximum, [s.max(axis=-1, keepdims=True) for s in ss])
        es = [jnp.where(mask, jnp.exp2(s - m), 0.0) for s, (_, mask, _) in zip(ss, pieces)]
        l = sum(e.sum(axis=-1, keepdims=True) for e in es)
        acc = sum(pv(e.astype(BF16)) for e, (_, _, pv) in zip(es, pieces))
        o = acc / jnp.maximum(l, TINY)
        first_group = lax.broadcasted_iota(jnp.int32, (nrow, HEAD_DIM), 0) < nrow // N_KV
        return jnp.where(first_group, o[:, :HEAD_DIM], o[:, HEAD_DIM:])

    def feature_major(kt, vt, mask):
        vtb = vt.astype(BF16)
        return _dot(qbd, kt.astype(BF16)), mask, lambda e: _dot_nt(e, vtb)

    def new_rows(new_ref):
        kv = new_ref[...]
        idx = lax.broadcasted_iota(jnp.int32, (nrow, kv.shape[0]), 1)
        v = kv[:, kw:].astype(BF16)
        return _dot_nt(qbd, kv[:, :kw].astype(BF16)), (idx <= tok) & (idx < tdec), lambda e: _dot(e, v)

    selm = mask_ref[...].astype(BF16)
    pieces = []
    for c in range(npages // ppc):
        pages = [buf[slot, c * ppc + j] for j in range(ppc)]
        kt = jnp.concatenate([pg[:kw] for pg in pages], axis=1)
        vt = jnp.concatenate([pg[kw:] for pg in pages], axis=1)
        pieces.append(feature_major(kt, vt, _dot(selm, eexp_ref[:, c * kch:(c + 1) * kch]) > 0.5))
    o_sel = attend(pieces + [new_rows(selnew_ref)])

    wbuf = winc_ref.shape[1]
    rel = wbuf + tok - lax.broadcasted_iota(jnp.int32, (nrow, wbuf), 1)
    o_win = attend([feature_major(winc_ref[0:kw, :], winc_ref[kw:, :], (rel >= 0) & (rel < WINDOW)),
                    new_rows(winnew_ref)])

    gates = _sigmoid(gt_ref[...])
    o_ref[...] = gates[:, 0:1] * ocmp_ref[...] + gates[:, 1:2] * o_sel + gates[:, 2:3] * o_win


def _attn_sample(qbd, gt, page_table, cache_cmp_t, cache_sel_t, cache_win_t, sel_new, win_new, wts, past_len, tdec):
    nb, nrow, _ = qbd.shape
    npages = page_table.shape[1]
    ncb = past_len // CMP_BLOCK
    per_b = lambda r, w: pl.BlockSpec((None, r, w), lambda b, pt: (b, 0, 0))
    full = lambda a: pl.BlockSpec(a.shape, lambda b, pt: (0,) * a.ndim)
    hbm = pl.BlockSpec(memory_space=pl.ANY)
    page_buf = pltpu.VMEM((PAGE_SLOTS, npages, 2 * KV_WIDTH, PAGE_SIZE), F32)
    page_sem = pltpu.SemaphoreType.DMA((PAGE_SLOTS,))

    r = np.arange(nrow)
    grp = (r // tdec) // Q_PER_KV * tdec + r % tdec
    rsum = (np.arange(N_KV * tdec)[:, None] == grp[None, :]).astype(np.float32)
    pair = np.zeros((ncb, LANES), np.float32)
    pair[np.arange(ncb), np.arange(ncb) // (SEL_BLOCK // CMP_BLOCK)] = 1.0
    eexp = (np.arange(LANES)[:, None] == np.arange(past_len)[None, :] // SEL_BLOCK).astype(np.float32)
    consts1 = [wts["pw_all"], wts["gkc"], wts["bd_kv"], jnp.asarray(pair, BF16),
               jnp.asarray(rsum, BF16), jnp.asarray(rsum.T, BF16)]

    o_cmp, mask = pl.pallas_call(
        functools.partial(_cmp_sample_body, npages=npages, past_len=past_len, tdec=tdec),
        grid_spec=pltpu.PrefetchScalarGridSpec(
            num_scalar_prefetch=1, grid=(nb,),
            in_specs=[hbm, per_b(nrow, KV_WIDTH)] + [full(a) for a in consts1],
            out_specs=[per_b(nrow, HEAD_DIM), per_b(nrow, LANES)],
            scratch_shapes=[page_buf, page_sem, pltpu.VMEM((ncb, 2 * KV_WIDTH), F32)]),
        out_shape=[jax.ShapeDtypeStruct((nb, nrow, HEAD_DIM), F32), jax.ShapeDtypeStruct((nb, nrow, LANES), F32)],
        compiler_params=_params("arbitrary"),
        name="attn_sample_cmp",
    )(page_table, cache_cmp_t, qbd, *consts1)

    ppc = min(8, npages)
    eexp = jnp.asarray(eexp, BF16)
    return pl.pallas_call(
        functools.partial(_sel_sample_body, npages=npages, past_len=past_len, tdec=tdec, ppc=ppc),
        grid_spec=pltpu.PrefetchScalarGridSpec(
            num_scalar_prefetch=1, grid=(nb,),
            in_specs=[hbm, per_b(nrow, KV_WIDTH), per_b(nrow, LANES), full(eexp),
                      per_b(sel_new.shape[1], 2 * KV_WIDTH), per_b(2 * KV_WIDTH, cache_win_t.shape[2]),
                      per_b(win_new.shape[1], 2 * KV_WIDTH), per_b(nrow, SUBLANES), per_b(nrow, HEAD_DIM)],
            out_specs=per_b(nrow, HEAD_DIM),
            scratch_shapes=[page_buf, page_sem]),
        out_shape=jax.ShapeDtypeStruct((nb, nrow, HEAD_DIM), F32),
        compiler_params=_params("arbitrary"),
        name="attn_sample_sel",
    )(page_table, cache_sel_t, qbd, mask, eexp, sel_new, cache_win_t, win_new, gt, o_cmp)


def _pool_weights(w_pos_k, w_pos_v):
    bpp = PAGE_SIZE // CMP_BLOCK
    ppi = SUBLANES // bpp
    rows = np.arange(PAGE_SIZE)
    place = np.zeros((ppi, SUBLANES, PAGE_SIZE), np.float32)
    for j in range(ppi):
        place[j, j * bpp + rows // CMP_BLOCK, rows] = 1.0
    place = jnp.asarray(place)

    def hi_lo(w):
        full = place * jnp.tile(w, bpp)[None, None, :]
        hi = full.astype(BF16)
        return hi, (full - hi.astype(F32)).astype(BF16)

    k_hi, k_lo = hi_lo(w_pos_k)
    v_hi, v_lo = hi_lo(w_pos_v)
    return jnp.concatenate([k_hi, k_lo, v_hi, v_lo], axis=1)


def _prep_weights(l, g_norm1, g_norm2, w_in, g_q, g_k_cmp, g_k_sel, g_k_win, w_pos_k, w_pos_v, conv_w, conv_b,
                  dt_bias, a_log, d_skip, g_att_out, g_ssm_out, w_out, w_rg, b_rg, w_re, b_re, w_gate, w_up, w_down):
    w = w_in[l]
    o_gt = ATT_WIDTH + 6 * KV_WIDTH
    o_z = o_gt + 3 * N_HEADS
    o_xbc = o_z + SSM_WIDTH
    o_dt = o_xbc + CONV_DIM
    pad = jnp.zeros((D_MODEL, PK_WIDTH - PK_GD - SSM_HEADS - 3 * N_HEADS), F32)
    w_packed = jnp.concatenate([w[:, :o_gt], w[:, o_z:o_xbc], w[:, o_xbc:o_dt], w[:, o_dt:], w[:, o_gt:o_z], pad],
                               axis=1).astype(BF16)
    seg = np.arange(KV_WIDTH) // HEAD_DIM
    bd_kv = jnp.asarray((seg[:, None] == seg[None, :]).astype(np.float32), BF16)
    lane_pad = lambda v: jnp.pad(v.astype(F32), (0, LANES - v.shape[0])).reshape(1, LANES)
    wr = jnp.concatenate([w_rg[l], jnp.transpose(w_re[l], (1, 0, 2)).reshape(D_MODEL, N_EXPERTS)], axis=1)
    wr = jnp.pad(wr, ((0, 0), (0, LANES - wr.shape[1])))
    wr_hi = wr.astype(BF16)
    wr_lo = (wr - wr_hi.astype(F32)).astype(BF16)
    return {
        "g1": g_norm1[l].reshape(1, D_MODEL), "w_in": w_packed,
        "gq_col": (jnp.tile(g_q[l], N_HEADS) * (SCALE * LOG2E)).reshape(ATT_WIDTH, 1),
        "gks": jnp.tile(g_k_sel[l], N_KV).reshape(1, KV_WIDTH),
        "gkw": jnp.tile(g_k_win[l], N_KV).reshape(1, KV_WIDTH),
        "gkc": jnp.tile(g_k_cmp[l], N_KV).reshape(1, KV_WIDTH),
        "bd_kv": bd_kv,
        "wkv": jnp.concatenate([jnp.broadcast_to(w_pos_k[l][:, None], (CMP_BLOCK, KV_WIDTH)),
                                jnp.broadcast_to(w_pos_v[l][:, None], (CMP_BLOCK, KV_WIDTH))], axis=1),
        "pw_all": _pool_weights(w_pos_k[l], w_pos_v[l]),
        "conv_w": conv_w[l], "conv_b": conv_b[l].reshape(1, CONV_DIM),
        "dt_bias": lane_pad(dt_bias[l]), "a_log": lane_pad(a_log[l]), "d_skip": jnp.repeat(d_skip[l].astype(F32), SSM_HEAD_DIM).reshape(1, SSM_WIDTH),
        "g_ssm": g_ssm_out[l].reshape(1, SSM_WIDTH),
        "g_att": g_att_out[l].reshape(1, ATT_WIDTH), "w_out": w_out[l].astype(BF16),
        "g2": g_norm2[l].reshape(1, D_MODEL), "wr_hi": wr_hi, "wr_lo": wr_lo,
        "br": lane_pad(jnp.concatenate([b_rg[l], b_re[l].reshape(-1)])),
        "w_gate": w_gate[l].astype(BF16), "w_up": w_up[l].astype(BF16),
        "w_down": w_down[l].astype(BF16).reshape(N_EGROUPS, E_PER_GROUP * EXPERT_FF, D_MODEL),
    }


def _finish(att2d, y2d, x2d, mods, mod_index, tm_post, tm_moe, moe_index, wts):
    x1, h2, comb = _post(att2d, y2d, x2d, mods[2], mods[3], mods[4], mod_index, tm_post, wts)
    return _moe(h2, comb, x1, mods[5], moe_index, tm_moe, wts)


def _token_major_cache(t):
    b, _, n = t.shape
    return jnp.transpose(t.reshape(b, 2, N_KV, HEAD_DIM, n), (0, 4, 1, 2, 3))[None]


def _feature_major_cache(c):
    n, rows = c.shape[:2]
    return jnp.transpose(c, (0, 2, 3, 4, 1)).reshape(n, 2 * KV_WIDTH, rows)


def kernel(x_prompt, x_sample, cache_cmp, cache_sel, cache_win, state_ssm, state_conv, page_table, c_prompt, c_sample,
           g_norm1, g_norm2, w_ada, b_ada, w_in, g_q, g_k_cmp, g_k_sel, g_k_win, w_pos_k, w_pos_v, conv_w, conv_b,
           dt_bias, a_log, d_skip, g_att_out, g_ssm_out, w_out, w_rg, b_rg, w_re, b_re, w_gate, w_up, w_down):
    depth = w_in.shape[0]
    assert depth == 1
    l = 0
    bp, seq, d = x_prompt.shape
    bs, tdec, _ = x_sample.shape
    npages = page_table.shape[1]
    past_len = npages * PAGE_SIZE
    tp, ts = bp * seq, bs * tdec

    wts = _prep_weights(l, g_norm1, g_norm2, w_in, g_q, g_k_cmp, g_k_sel, g_k_win, w_pos_k, w_pos_v, conv_w, conv_b,
                        dt_bias, a_log, d_skip, g_att_out, g_ssm_out, w_out, w_rg, b_rg, w_re, b_re, w_gate, w_up,
                        w_down)
    m_all = _ada(jnp.concatenate([c_prompt, c_sample], axis=0), w_ada[l], b_ada[l])
    mods_p = [m.reshape(bp, 1, d) for m in jnp.split(m_all[:bp], 6, axis=-1)]
    tm_s = min(512, ts)
    mods_s = [jnp.repeat(m, tdec, axis=0).reshape(ts // tm_s, tm_s, d) for m in jnp.split(m_all[bp:], 6, axis=-1)]

    tm = 512
    per_seq = seq // tm
    xp2 = x_prompt.reshape(tp, d)
    qt, cmpt_p, selt_p, selk, selvt, wint_p, wink, winvt, z, xbc, gd, gdt, kc = _in_proj(
        xp2, bp, mods_p[0], mods_p[1], lambda i: (i // per_seq, 0, 0), tm, wts, True)
    ncb = seq // CMP_BLOCK
    nch = seq // KEY_CHUNK
    r3 = lambda a, n: a.reshape(bp, n, a.shape[-1])
    kc_eo = kc.reshape(bp, ncb // 2, 2, 2 * KV_WIDTH).transpose(0, 2, 1, 3).reshape(bp, ncb, 2 * KV_WIDTH)
    kcmp = kc_eo[:, :, :KV_WIDTH].astype(BF16)
    vcmp_t = jnp.transpose(kc_eo[:, :, KV_WIDTH:], (0, 2, 1)).astype(BF16)
    chunked = lambda a: a.reshape(bp, nch, KV_WIDTH, KEY_CHUNK)
    att = _attn_prompt(qt, r3(selk, seq), chunked(selvt), r3(wink, seq), chunked(winvt), kcmp, vcmp_t, gdt)
    y_ssm, ssm_p, conv_p = _ssd(r3(xbc, seq), r3(z, seq), r3(gd, seq), jnp.zeros((bp, SUBLANES, CONV_DIM), F32),
                                jnp.zeros((bp, SSM_HEADS, SSM_HEAD_DIM, SSM_STATE), F32), wts)
    tm_moe = 512
    y_prompt = _finish(att.reshape(tp, ATT_WIDTH), y_ssm.reshape(tp, SSM_WIDTH), xp2, mods_p,
                       lambda i: (i // per_seq, 0, 0), tm, tm_moe, lambda i: (i // (seq // tm_moe), 0, 0), wts)

    xs2 = x_sample.reshape(ts, d)
    qt_s, cmpt_s, selt_s, _, _, wint_s, _, _, z, xbc, gd, _ = _in_proj(
        xs2, 1, mods_s[0], mods_s[1], lambda i: (i, 0, 0), tm_s, wts, False)
    nrow = N_HEADS * tdec
    qh = qt_s[0].T.reshape(bs, tdec, N_HEADS, HEAD_DIM).transpose(0, 2, 1, 3).reshape(bs, nrow, HEAD_DIM)
    g0 = (jnp.arange(nrow) < nrow // N_KV)[None, :, None]
    zero = jnp.zeros_like(qh)
    qbd = jnp.concatenate([jnp.where(g0, qh, zero), jnp.where(g0, zero, qh)], axis=-1)
    gt = gd[:, GD_GATE0:GD_GATE0 + 3 * N_HEADS].reshape(bs, tdec, 3, N_HEADS).transpose(0, 3, 1, 2)
    gt = jnp.pad(gt.reshape(bs, nrow, 3), ((0, 0), (0, 0), (0, SUBLANES - 3)))
    rows_s = lambda t: t[0].T.reshape(bs, tdec, 2 * KV_WIDTH)
    sel_s, win_s = rows_s(selt_s), rows_s(wint_s)
    new_pad = 2 * SUBLANES - tdec
    padded = lambda a: jnp.pad(a, ((0, 0), (0, new_pad), (0, 0)))
    cwin_t = _feature_major_cache(cache_win[l])
    att_s = _attn_sample(qbd, gt, page_table, _feature_major_cache(cache_cmp[l]), _feature_major_cache(cache_sel[l]),
                         cwin_t, padded(sel_s), padded(win_s), wts, past_len, tdec)
    att_s = att_s.reshape(bs, N_HEADS, tdec, HEAD_DIM).transpose(0, 2, 1, 3).reshape(ts, ATT_WIDTH)

    batch_minor = lambda a: jnp.transpose(a.reshape(bs, tdec, a.shape[-1]), (1, 2, 0))
    xbc3 = xbc.reshape(bs, tdec, CONV_DIM)
    y_t, ssm_t = _ssd_decode(batch_minor(xbc), jnp.transpose(state_conv[l], (1, 2, 0)), batch_minor(z),
                             batch_minor(gd[:, :SSM_HEADS]), jnp.transpose(state_ssm[l].astype(F32), (1, 2, 3, 0)),
                             wts)
    y_s = jnp.transpose(y_t, (2, 0, 1)).reshape(ts, SSM_WIDTH)
    ssm_s = jnp.transpose(ssm_t, (3, 0, 1, 2))
    conv_s = jnp.concatenate([state_conv[l], xbc3], axis=1)[:, tdec:]
    y_sample = _finish(att_s, y_s, xs2, mods_s, lambda i: (i, 0, 0), tm_s, tm_s, lambda i: (i, 0, 0), wts)

    kv6 = lambda a: a.reshape(1, bs, tdec, 2, N_KV, HEAD_DIM)
    wkeep = min(WINDOW, seq)
    wbuf = cwin_t.shape[2]
    wkeep_s = min(WINDOW, wbuf + tdec)
    win_all_t = jnp.concatenate([cwin_t, jnp.transpose(win_s, (0, 2, 1))], axis=2)[:, :, wbuf + tdec - wkeep_s:]
    tail = lambda a: a[:, SUBLANES - (CONV_W - 1):][None]
    return (y_prompt.reshape(bp, seq, d), y_sample.reshape(bs, tdec, d),
            _token_major_cache(cmpt_p), kv6(rows_s(cmpt_s)), _token_major_cache(selt_p), kv6(sel_s),
            _token_major_cache(wint_p[:, :, seq - wkeep:]), _token_major_cache(win_all_t),
            ssm_p[None], ssm_s[None].astype(state_ssm.dtype), tail(conv_p), conv_s[None])
```

```python
import functools

import numpy as np
import jax
import jax.numpy as jnp
from jax import lax
from jax.experimental import pallas as pl
from jax.experimental.pallas import tpu as pltpu

F32 = jnp.float32
BF16 = jnp.bfloat16

D_MODEL = 1024
PAGE_SIZE = 128
N_HEADS = 8
N_KV = 2
HEAD_DIM = 64
Q_PER_KV = N_HEADS // N_KV
ATT_WIDTH = N_HEADS * HEAD_DIM
KV_WIDTH = N_KV * HEAD_DIM
CMP_BLOCK = 32
SEL_BLOCK = 64
N_SEL = 16
WINDOW = 512
SCALE = HEAD_DIM ** -0.5
SSM_HEADS = 8
SSM_HEAD_DIM = 64
SSM_WIDTH = SSM_HEADS * SSM_HEAD_DIM
SSM_GROUPS = 2
SSM_STATE = 64
CONV_W = 4
SSD_CHUNK = 128
CONV_DIM = SSM_WIDTH + 2 * SSM_GROUPS * SSM_STATE
N_EGROUPS = 4
E_PER_GROUP = 4
N_EXPERTS = N_EGROUPS * E_PER_GROUP
EXPERT_FF = 256
EPS = 1e-6
NEG = -1e30
BIG = 1e9
TINY = 1e-30
LOG2E = 1.4426950408889634

LANES = 128
SUBLANES = 8
VMEM_LIMIT = 48 * 1024 * 1024
Q_TILE = 128
KEY_CHUNK = 128
SEL_UNROLL = 4
WIN_UNROLL = 5
DEAD_SHIFT = 1 << 24
PAGE_SLOTS = 3
POOL_UNROLL = 16

PK_Q = 0
PK_CMP = 512
PK_SEL = 768
PK_WIN = 1024
PK_Z = 1280
PK_XBC = 1792
PK_GD = 2560
PK_WIDTH = 2688
GD_GATE0 = 8
RT_E0 = 4


def _dot(a, b):
    return jnp.dot(a, b, preferred_element_type=F32)


def _dot_nt(a, b):
    return lax.dot_general(a, b, (((1,), (1,)), ((), ())), preferred_element_type=F32)


def _dot_tn(a, b):
    return lax.dot_general(a, b, (((0,), (0,)), ((), ())), preferred_element_type=F32)


def _split2(x):
    hi = x.astype(BF16)
    lo = (x - hi.astype(F32)).astype(BF16)
    return hi, lo


def _split3(x):
    hi = x.astype(BF16)
    r = x - hi.astype(F32)
    mid = r.astype(BF16)
    lo = (r - mid.astype(F32)).astype(BF16)
    return hi, mid, lo


def _dot_exact_lhs(a_bf16, x):
    return sum(_dot(a_bf16, p) for p in _split3(x))


def _dot_exact_rhs(x, b_bf16):
    return sum(_dot(p, b_bf16) for p in _split3(x))


def _sigmoid(x):
    return 1.0 / (1.0 + jnp.exp(-x))


def _silu(x):
    return x * _sigmoid(x)


def _softplus(x):
    return jnp.maximum(x, 0.0) + jnp.log1p(jnp.exp(-jnp.abs(x)))


def _rms(x):
    return x * lax.rsqrt(jnp.mean(x * x, axis=-1, keepdims=True) + EPS)


def _head_rms(x, bd):
    hi, lo = _split2(x * x)
    ms = (_dot(hi, bd) + _dot(lo, bd)) * (1.0 / HEAD_DIM)
    return x * lax.rsqrt(ms + EPS)


def _topk_mask_lanes(score, nblk, k):
    lane = lax.broadcasted_iota(jnp.int32, score.shape, 1)
    rank = jnp.zeros(score.shape, F32)
    for i in range(nblk):
        si = score[:, i:i + 1]
        beats = jnp.where(si > score, 1.0, jnp.where(si == score, jnp.where(lane > i, 1.0, 0.0), 0.0))
        rank = rank + beats
    return jnp.where(rank < k, 1.0, 0.0)


def _topk_mask_rows(score, k):
    row = lax.broadcasted_iota(jnp.int32, score.shape, 0)
    rank = jnp.zeros(score.shape, F32)
    for i in range(score.shape[0]):
        si = score[i:i + 1, :]
        beats = jnp.where(si > score, 1.0, jnp.where(si == score, jnp.where(row > i, 1.0, 0.0), 0.0))
        rank = rank + beats
    return jnp.where(rank < k, 1.0, 0.0)


def _params(*sem):
    return pltpu.CompilerParams(dimension_semantics=sem, vmem_limit_bytes=VMEM_LIMIT)


def _ada_body(c_ref, w_ref, b_ref, o_ref):
    c = _silu(c_ref[...]).astype(BF16)
    o_ref[...] = _dot(c, w_ref[...].astype(BF16)) + b_ref[...]


def _ada(c_all, w_ada, b_ada):
    n, d = c_all.shape
    width = w_ada.shape[1]
    tn = 1536
    return pl.pallas_call(
        _ada_body,
        grid=(width // tn,),
        in_specs=[pl.BlockSpec((n, d), lambda j: (0, 0)),
                  pl.BlockSpec((d, tn), lambda j: (0, j)),
                  pl.BlockSpec((1, tn), lambda j: (0, j))],
        out_specs=pl.BlockSpec((n, tn), lambda j: (0, j)),
        out_shape=jax.ShapeDtypeStruct((n, width), F32),
        compiler_params=_params("arbitrary"),
        name="ada",
    )(c_all, w_ada, b_ada.reshape(1, width))


def _inproj_body(with_cmp, x_ref, sh_ref, sc_ref, g1_ref, w_ref, gqc_ref, gks_ref, gkw_ref, bd_ref, *rest):
    if with_cmp:
        wkv_ref, gkc_ref = rest[:2]
        rest = rest[2:]
    (qt_ref, cmpt_ref, selt_ref, selk_ref, selvt_ref, wint_ref, wink_ref, winvt_ref,
     z_ref, xbc_ref, gd_ref, gdt_ref) = rest[:12]
    tm = x_ref.shape[0]
    h = _rms(x_ref[...]) * g1_ref[...]
    hb = (h * (1.0 + sc_ref[0]) + sh_ref[0]).astype(BF16)
    bd_kv = bd_ref[...]

    q3 = _dot(hb, w_ref[:, PK_Q:PK_CMP]).T.reshape(N_HEADS, HEAD_DIM, tm)
    q3 = q3 * lax.rsqrt(jnp.mean(q3 * q3, axis=1, keepdims=True) + EPS)
    qt_ref[...] = (q3.reshape(ATT_WIDTH, tm) * gqc_ref[...]).astype(BF16)

    kv = _dot(hb, w_ref[:, PK_CMP:PK_SEL])
    cmpt_ref[...] = kv.T
    if with_cmp:
        kc_ref = rest[12]
        pooled = (kv.reshape(tm // CMP_BLOCK, CMP_BLOCK, 2 * KV_WIDTH) * wkv_ref[...][None]).sum(axis=1)
        kc_ref[:, 0:KV_WIDTH] = _head_rms(pooled[:, :KV_WIDTH], bd_kv) * gkc_ref[...]
        kc_ref[:, KV_WIDTH:] = pooled[:, KV_WIDTH:]

    def normed_kv(lo, hi, gain_ref, t_ref, k_ref, vt_ref):
        kv = _dot(hb, w_ref[:, lo:hi])
        kn = _head_rms(kv[:, :KV_WIDTH], bd_kv) * gain_ref[...]
        k_ref[...] = kn.astype(BF16)
        kvt = jnp.concatenate([kn, kv[:, KV_WIDTH:]], axis=1).T
        t_ref[...] = kvt
        vt = kvt[KV_WIDTH:, :].astype(BF16)
        for j in range(tm // KEY_CHUNK):
            vt_ref[j] = vt[:, j * KEY_CHUNK:(j + 1) * KEY_CHUNK]

    normed_kv(PK_SEL, PK_WIN, gks_ref, selt_ref, selk_ref, selvt_ref)
    normed_kv(PK_WIN, PK_Z, gkw_ref, wint_ref, wink_ref, winvt_ref)

    z_ref[...] = _dot(hb, w_ref[:, PK_Z:PK_XBC])
    xbc_ref[...] = _dot(hb, w_ref[:, PK_XBC:PK_GD])
    gd = _dot(hb, w_ref[:, PK_GD:PK_WIDTH])
    gd_ref[...] = gd
    gdt_ref[...] = gd.T


def _in_proj(x2d, nbatch, shift3, scale3, mod_index, tm, wts, with_cmp):
    t, d = x2d.shape
    seq = t // nbatch
    per_seq = seq // tm
    rmod = shift3.shape[1]
    full = lambda a: pl.BlockSpec(a.shape, lambda i: (0,) * a.ndim)
    row = lambda w: pl.BlockSpec((tm, w), lambda i: (i, 0))
    fmaj = lambda w: pl.BlockSpec((None, w, tm), lambda i: (i // per_seq, 0, i % per_seq))
    chunks = pl.BlockSpec((tm // KEY_CHUNK, KV_WIDTH, KEY_CHUNK), lambda i: (i, 0, 0))
    mod = pl.BlockSpec((1, rmod, d), mod_index)
    ins = [x2d, shift3, scale3, wts["g1"], wts["w_in"], wts["gq_col"], wts["gks"], wts["gkw"], wts["bd_kv"]]
    in_specs = [row(d), mod, mod] + [full(a) for a in ins[3:]]
    if with_cmp:
        ins += [wts["wkv"], wts["gkc"]]
        in_specs += [full(wts["wkv"]), full(wts["gkc"])]
    sds = jax.ShapeDtypeStruct
    fm = lambda w, dt: sds((nbatch, w, seq), dt)
    ck = sds((t // KEY_CHUNK, KV_WIDTH, KEY_CHUNK), BF16)
    out_shape = [fm(ATT_WIDTH, BF16), fm(2 * KV_WIDTH, F32),
                 fm(2 * KV_WIDTH, F32), sds((t, KV_WIDTH), BF16), ck,
                 fm(2 * KV_WIDTH, F32), sds((t, KV_WIDTH), BF16), ck,
                 sds((t, SSM_WIDTH), F32), sds((t, CONV_DIM), F32), sds((t, LANES), F32), fm(LANES, F32)]
    out_specs = [fmaj(ATT_WIDTH), fmaj(2 * KV_WIDTH),
                 fmaj(2 * KV_WIDTH), row(KV_WIDTH), chunks,
                 fmaj(2 * KV_WIDTH), row(KV_WIDTH), chunks,
                 row(SSM_WIDTH), row(CONV_DIM), row(LANES), fmaj(LANES)]
    if with_cmp:
        out_shape.append(sds((t // CMP_BLOCK, 2 * KV_WIDTH), F32))
        out_specs.append(pl.BlockSpec((tm // CMP_BLOCK, 2 * KV_WIDTH), lambda i: (i, 0)))
    return pl.pallas_call(
        functools.partial(_inproj_body, with_cmp),
        grid=(t // tm,),
        in_specs=in_specs,
        out_specs=out_specs,
        out_shape=out_shape,
        compiler_params=_params("arbitrary"),
        name="in_proj",
    )(*ins)


def _attn_prompt_body(qt_ref, selk_ref, selvt_ref, wink_ref, winvt_ref, kc_ref, vct_ref, gdt_ref, o_ref, selm_scr,
                      *, tq, seq):
    i = pl.program_id(1)
    ncb = seq // CMP_BLOCK
    nblk = seq // SEL_BLOCK
    bpc = KEY_CHUNK // SEL_BLOCK
    wide = Q_PER_KV * tq
    qt = qt_ref[...]
    t_row = i * tq + lax.broadcasted_iota(jnp.int32, (1, tq), 1)
    cur_row = t_row // SEL_BLOCK
    key_i = lax.broadcasted_iota(jnp.int32, (KEY_CHUNK, tq), 0)
    gates = _sigmoid(gdt_ref[GD_GATE0:GD_GATE0 + 3 * N_HEADS, :])
    rep = lambda a: jnp.concatenate([a] * Q_PER_KV, axis=1)

    def flash(k_ref, vt_ref, qgts, c_lo, c_hi, masks_fn, groups, unroll):
        last = seq // KEY_CHUNK - 1

        def step(j, carry):
            chunks = []
            for u in range(unroll):
                c = c_lo + unroll * j + u
                cc = jnp.minimum(c, last)
                off = pl.multiple_of(cc * KEY_CHUNK, KEY_CHUNK)
                scores = {g: _dot(k_ref[pl.ds(off, KEY_CHUNK), g * HEAD_DIM:(g + 1) * HEAD_DIM], qgts[g])
                          for g in groups}
                chunks.append((c, cc, scores))
            state = dict(zip(groups, carry))
            for c, cc, scores in chunks:
                masks = masks_fn(c, cc, c < c_hi)
                for g in groups:
                    m, l, acc = state[g]
                    s = jnp.where(rep(masks[g]), scores[g], NEG)
                    m_new = jnp.maximum(m, s.max(axis=0, keepdims=True))
                    alpha = jnp.exp2(m - m_new)
                    e = jnp.exp2(s - m_new)
                    pv = _dot(vt_ref[cc, g * HEAD_DIM:(g + 1) * HEAD_DIM, :], e.astype(BF16))
                    state[g] = (m_new, alpha * l + e.sum(axis=0, keepdims=True), alpha * acc + pv)
            return tuple(state[g] for g in groups)

        init = (jnp.full((1, wide), NEG, F32), jnp.zeros((1, wide), F32), jnp.zeros((HEAD_DIM, wide), F32))
        trips = (c_hi - c_lo + unroll - 1) // unroll
        res = lax.fori_loop(0, trips, step, (init,) * len(groups))
        return [acc / jnp.maximum(l, TINY) for _, l, acc in res]

    qgts, o_cmps = [], []
    for g in range(N_KV):
        qgt = jnp.concatenate(
            [qt[(g * Q_PER_KV + h) * HEAD_DIM:(g * Q_PER_KV + h + 1) * HEAD_DIM, :] for h in range(Q_PER_KV)], axis=1)
        qgts.append(qgt)

        kcg = kc_ref[:, g * HEAD_DIM:(g + 1) * HEAD_DIM]
        vctg = vct_ref[g * HEAD_DIM:(g + 1) * HEAD_DIM, :]
        r = lax.broadcasted_iota(jnp.int32, (ncb, tq), 0)
        cblk = jnp.where(r < nblk, 2 * r, 2 * (r - nblk) + 1)
        cmask = rep((cblk * CMP_BLOCK + CMP_BLOCK - 1) <= t_row)
        s = jnp.where(cmask, _dot(kcg, qgt), NEG)
        e = jnp.where(cmask, jnp.exp2(s - s.max(axis=0, keepdims=True)), 0.0)
        p = e / jnp.maximum(e.sum(axis=0, keepdims=True), TINY)
        o_cmps.append(_dot(vctg, p.astype(BF16)))

        imp = sum(p[:, h * tq:(h + 1) * tq] for h in range(Q_PER_KV))
        imp = imp[:nblk] + imp[nblk:]
        jb = lax.broadcasted_iota(jnp.int32, (nblk, tq), 0)
        allowed = jb < cur_row
        forced = (jb == 0) | (jb == cur_row - 1)
        score = jnp.where(allowed, jnp.where(forced, BIG, imp), NEG)
        selm = jnp.where(allowed, _topk_mask_rows(score, min(N_SEL - 1, nblk)), 0.0)
        for c in range(seq // KEY_CHUNK):
            selm_scr[g, c, 0:bpc] = selm[c * bpc:(c + 1) * bpc]

    def key_pos(c, live):
        return c * KEY_CHUNK + key_i + jnp.where(live, 0, DEAD_SHIFT)

    def sel_masks(c, cc, live):
        kp = key_pos(c, live)
        cur_blk = ((kp // SEL_BLOCK) == cur_row) & (kp <= t_row)
        alive = jnp.where(live, 1.0, 0.0)
        out = []
        for g in range(N_KV):
            rows = selm_scr[g, cc] * alive
            picked = jnp.concatenate(
                [jnp.broadcast_to(rows[j:j + 1], (SEL_BLOCK, tq)) for j in range(bpc)], axis=0) > 0.5
            out.append(picked | cur_blk)
        return out

    def win_masks(c, cc, live):
        rel = t_row - key_pos(c, live)
        return [(rel >= 0) & (rel < WINDOW)] * N_KV

    c_hi = ((i + 1) * tq + KEY_CHUNK - 1) // KEY_CHUNK
    c_win = jnp.maximum(i * tq - WINDOW + 1, 0) // KEY_CHUNK
    both = tuple(range(N_KV))
    o_sels = flash(selk_ref, selvt_ref, qgts, 0, c_hi, sel_masks, both, SEL_UNROLL)
    o_wins = flash(wink_ref, winvt_ref, qgts, c_win, c_hi, win_masks, both, WIN_UNROLL)

    heads = []
    for g in range(N_KV):
        for h in range(Q_PER_KV):
            hh = g * Q_PER_KV + h
            sl = slice(h * tq, (h + 1) * tq)
            gate = lambda br: gates[br * N_HEADS + hh:br * N_HEADS + hh + 1, :]
            heads.append(gate(0) * o_cmps[g][:, sl] + gate(1) * o_sels[g][:, sl] + gate(2) * o_wins[g][:, sl])
    o_ref[...] = jnp.concatenate(heads, axis=0).T


def _attn_prompt(qt, selk, selvt, wink, winvt, kc, vct, gdt):
    b, _, s = qt.shape
    tq = Q_TILE
    ncb = s // CMP_BLOCK
    nch = s // KEY_CHUNK
    tile_t = lambda w: pl.BlockSpec((None, w, tq), lambda bi, i: (bi, 0, i))
    whole = lambda shape: pl.BlockSpec((None,) + shape, lambda bi, i: (bi,) + (0,) * len(shape))
    return pl.pallas_call(
        functools.partial(_attn_prompt_body, tq=tq, seq=s),
        grid=(b, s // tq),
        in_specs=[tile_t(ATT_WIDTH), whole((s, KV_WIDTH)), whole((nch, KV_WIDTH, KEY_CHUNK)),
                  whole((s, KV_WIDTH)), whole((nch, KV_WIDTH, KEY_CHUNK)),
                  whole((ncb, KV_WIDTH)), whole((KV_WIDTH, ncb)), tile_t(LANES)],
        out_specs=pl.BlockSpec((None, tq, ATT_WIDTH), lambda bi, i: (bi, i, 0)),
        out_shape=jax.ShapeDtypeStruct((b, s, ATT_WIDTH), F32),
        scratch_shapes=[pltpu.VMEM((N_KV, nch, SUBLANES, tq), F32)],
        compiler_params=_params("arbitrary", "arbitrary"),
        name="attn_prompt",
    )(qt, selk, selvt, wink, winvt, kc, vct, gdt)


def _ssd_body(xbc_ref, z_ref, gd_ref, cprev_ref, h0_ref, cw_ref, cb_ref, dtb_ref, alog_ref, dsk_ref, gout_ref,
              hexp_ref, y_ref, hout_ref, cout_ref, h_scr, xe_scr, yoff_scr, *, cl):
    c = pl.program_id(1)
    gn = SSM_GROUPS * SSM_STATE
    hpg = SSM_HEADS // SSM_GROUPS

    @pl.when(c == 0)
    def _():
        h_scr[...] = h0_ref[...]
        xe_scr[0:SUBLANES] = cprev_ref[...]

    xe_scr[SUBLANES:SUBLANES + cl] = xbc_ref[...]
    base = SUBLANES - (CONV_W - 1)
    xc = cb_ref[...]
    for w in range(CONV_W):
        xc = xc + cw_ref[w:w + 1, :] * xe_scr[base + w:base + w + cl, :]
    tail = xe_scr[cl:cl + SUBLANES, :]
    cout_ref[...] = tail
    xe_scr[0:SUBLANES] = tail
    xc = _silu(xc)
    xs = xc[:, :SSM_WIDTH]
    bmb = xc[:, SSM_WIDTH:SSM_WIDTH + gn].astype(BF16)
    cmb = xc[:, SSM_WIDTH + gn:].astype(BF16)

    dt = _softplus(gd_ref[...] + dtb_ref[...])
    da = dt * (-jnp.exp(alog_ref[...]))
    ri = lax.broadcasted_iota(jnp.int32, (cl, cl), 0)
    ci = lax.broadcasted_iota(jnp.int32, (cl, cl), 1)
    causal = ri >= ci
    acs = _dot_exact_lhs(jnp.where(causal, 1.0, 0.0).astype(BF16), da)
    acs_t = acs.T
    acs_last = acs[cl - 1:cl, :]
    e_last = jnp.exp(acs_last)
    hexp = hexp_ref[...]
    dt_w = _dot_exact_rhs(dt, hexp)
    wend_w = _dot_exact_rhs(jnp.exp(acs_last - acs) * dt, hexp)
    eacs_w = _dot_exact_rhs(jnp.exp(acs), hexp)
    x_dt = (xs * dt_w).astype(BF16)
    x_we = (xs * wend_w).astype(BF16)

    for g in range(SSM_GROUPS):
        bg = bmb[:, g * SSM_STATE:(g + 1) * SSM_STATE]
        cg = cmb[:, g * SSM_STATE:(g + 1) * SSM_STATE]
        cb = _dot_nt(cg, bg)
        for hq in range(hpg):
            h = g * hpg + hq
            hs = slice(h * SSM_HEAD_DIM, (h + 1) * SSM_HEAD_DIM)
            seg = acs[:, h:h + 1] - acs_t[h:h + 1, :]
            decay = jnp.where(causal, jnp.exp(jnp.where(causal, seg, 0.0)), 0.0)
            y_ref[:, hs] = _dot((cb * decay).astype(BF16), x_dt[:, hs])
            hst = h_scr[h]
            yoff_scr[:, hs] = _dot_nt(cg, hst.astype(BF16))
            h_scr[h] = e_last[:, h:h + 1] * hst + _dot_tn(x_we[:, hs], bg)

    y = y_ref[...] + yoff_scr[...] * eacs_w + dsk_ref[...] * xs
    yg = y * _silu(z_ref[...])
    y_ref[...] = _rms(yg) * gout_ref[...]
    hout_ref[...] = h_scr[...]


def _ssd(xbc3, z3, gd3, cprev, h0, wts):
    b, l, _ = xbc3.shape
    cl = SSD_CHUNK
    assert cl == LANES and l % cl == 0
    nc = l // cl
    tile = lambda w: pl.BlockSpec((None, cl, w), lambda bi, c: (bi, c, 0))
    full = lambda a: pl.BlockSpec(a.shape, lambda bi, c: (0,) * a.ndim)
    names = ["conv_w", "conv_b", "dt_bias", "a_log", "d_skip", "g_ssm"]
    hexp = (np.arange(LANES)[:, None] == np.arange(SSM_WIDTH)[None, :] // SSM_HEAD_DIM).astype(np.float32)
    consts = [wts[n] for n in names] + [jnp.asarray(hexp, BF16)]
    hshape = (SSM_HEADS, SSM_HEAD_DIM, SSM_STATE)
    return pl.pallas_call(
        functools.partial(_ssd_body, cl=cl),
        grid=(b, nc),
        in_specs=[tile(CONV_DIM), tile(SSM_WIDTH), tile(LANES),
                  pl.BlockSpec((None, SUBLANES, CONV_DIM), lambda bi, c: (bi, 0, 0)),
                  pl.BlockSpec((None,) + hshape, lambda bi, c: (bi, 0, 0, 0))] + [full(a) for a in consts],
        out_specs=[tile(SSM_WIDTH),
                   pl.BlockSpec((None,) + hshape, lambda bi, c: (bi, 0, 0, 0)),
                   pl.BlockSpec((None, SUBLANES, CONV_DIM), lambda bi, c: (bi, 0, 0))],
        out_shape=[jax.ShapeDtypeStruct((b, l, SSM_WIDTH), F32),
                   jax.ShapeDtypeStruct((b,) + hshape, F32),
                   jax.ShapeDtypeStruct((b, SUBLANES, CONV_DIM), F32)],
        scratch_shapes=[pltpu.VMEM(hshape, F32), pltpu.VMEM((SUBLANES + cl, CONV_DIM), F32),
                        pltpu.VMEM((cl, SSM_WIDTH), F32)],
        compiler_params=_params("arbitrary", "arbitrary"),
        name="ssd",
    )(xbc3, z3, gd3, cprev, h0, *consts)


def _ssd_decode_body(xbc_ref, cprev_ref, z_ref, dt_ref, h_ref, cw_ref, cb_ref, dtb_ref, alog_ref, dsk_ref, gout_ref,
                     y_ref, hout_ref, xc_scr, dtv_scr, dec_scr, *, steps):
    hd = pl.program_id(0)
    gn = SSM_GROUPS * SSM_STATE
    hpg = SSM_HEADS // SSM_GROUPS

    @pl.when(hd == 0)
    def _():
        rows = [cprev_ref[j] for j in range(CONV_W - 1)] + [xbc_ref[t] for t in range(steps)]
        for t in range(steps):
            xc = cb_ref[...]
            for w in range(CONV_W):
                xc = xc + cw_ref[w] * rows[t + w]
            xc_scr[t] = _silu(xc)
        dt = _softplus(dt_ref[...] + dtb_ref[...][None])
        dtv_scr[...] = dt
        dec_scr[...] = jnp.exp(dt * (-jnp.exp(alog_ref[...]))[None])

    x_rows = pl.ds(pl.multiple_of(hd * SSM_HEAD_DIM, SSM_HEAD_DIM), SSM_HEAD_DIM)
    grp = hd // hpg
    b_rows = pl.ds(pl.multiple_of(SSM_WIDTH + grp * SSM_STATE, SSM_STATE), SSM_STATE)
    c_rows = pl.ds(pl.multiple_of(SSM_WIDTH + gn + grp * SSM_STATE, SSM_STATE), SSM_STATE)
    xs = [xc_scr[t, x_rows, :] for t in range(steps)]
    bs_ = [xc_scr[t, b_rows, :] for t in range(steps)]
    cs = [xc_scr[t, c_rows, :] for t in range(steps)]
    dec = [dec_scr[t, pl.ds(hd, 1), :] for t in range(steps)]
    xdt = [xs[t] * dtv_scr[t, pl.ds(hd, 1), :] for t in range(steps)]
    dsk = dsk_ref[x_rows, :]
    for p in range(SSM_HEAD_DIM):
        hp = h_ref[p]
        for t in range(steps):
            hp = dec[t] * hp + xdt[t][p:p + 1, :] * bs_[t]
            y_ref[t, pl.ds(hd * SSM_HEAD_DIM + p, 1), :] = jnp.sum(hp * cs[t], axis=0, keepdims=True)
        hout_ref[p] = hp
    for t in range(steps):
        y_ref[t, x_rows, :] = y_ref[t, x_rows, :] + dsk * xs[t]

    @pl.when(hd == SSM_HEADS - 1)
    def _():
        yg = y_ref[...] * _silu(z_ref[...])
        ms = jnp.mean(yg * yg, axis=1, keepdims=True)
        y_ref[...] = yg * lax.rsqrt(ms + EPS) * gout_ref[...][None]


def _ssd_decode(xbc_t, cprev_t, z_t, dt_t, h_t, wts):
    steps, _, nb = xbc_t.shape
    lanes = lambda v: jnp.broadcast_to(v.astype(F32)[..., None], v.shape + (nb,))
    consts = [lanes(wts["conv_w"]), lanes(wts["conv_b"][0]), lanes(wts["dt_bias"][0, :SSM_HEADS]),
              lanes(wts["a_log"][0, :SSM_HEADS]), lanes(wts["d_skip"][0]), lanes(wts["g_ssm"][0])]
    full = lambda a: pl.BlockSpec(a.shape, lambda hd: (0,) * a.ndim)
    per_head = pl.BlockSpec((None, SSM_HEAD_DIM, SSM_STATE, nb), lambda hd: (hd, 0, 0, 0))
    ins = [xbc_t, cprev_t, z_t, dt_t]
    return pl.pallas_call(
        functools.partial(_ssd_decode_body, steps=steps),
        grid=(SSM_HEADS,),
        in_specs=[full(a) for a in ins] + [per_head] + [full(a) for a in consts],
        out_specs=[full(z_t), per_head],
        out_shape=[jax.ShapeDtypeStruct(z_t.shape, F32), jax.ShapeDtypeStruct(h_t.shape, F32)],
        scratch_shapes=[pltpu.VMEM(xbc_t.shape, F32), pltpu.VMEM(dt_t.shape, F32), pltpu.VMEM(dt_t.shape, F32)],
        compiler_params=_params("arbitrary"),
        name="ssd_decode",
    )(*ins, h_t, *consts)


def _post_body(att_ref, y_ref, x_ref, g1m_ref, sh2_ref, sc2_ref, gatt_ref, wo_ref, gn2_ref, wrh_ref, wrl_ref, br_ref,
               x1_ref, h2_ref, comb_ref):
    an = (_rms(att_ref[...]) * gatt_ref[...]).astype(BF16)
    mixed = _dot(an, wo_ref[0:ATT_WIDTH, :]) + _dot(y_ref[...].astype(BF16), wo_ref[ATT_WIDTH:, :])
    x1 = x_ref[...] + g1m_ref[0] * mixed
    x1_ref[...] = x1
    h2 = _rms(x1) * gn2_ref[...] * (1.0 + sc2_ref[0]) + sh2_ref[0]
    h2_ref[...] = h2.astype(BF16)

    hh, hl = _split2(h2)
    logit = _dot(hh, wrh_ref[...]) + _dot(hl, wrh_ref[...]) + _dot(hh, wrl_ref[...]) + br_ref[...]
    lane = lax.broadcasted_iota(jnp.int32, logit.shape, 1).astype(F32)
    first = lambda cond: jnp.min(jnp.where(cond, lane, 1e6), axis=-1, keepdims=True)
    top = lambda cond: jnp.max(jnp.where(cond, logit, NEG), axis=-1, keepdims=True)
    is_g = lane < N_EGROUPS
    mg = top(is_g)
    gsel = first(is_g & (logit == mg))
    p_top = 1.0 / jnp.sum(jnp.where(is_g, jnp.exp(logit - mg), 0.0), axis=-1, keepdims=True)
    lo = RT_E0 + E_PER_GROUP * gsel
    is_e = (lane >= lo) & (lane < lo + E_PER_GROUP)
    v1 = top(is_e)
    i1 = first(is_e & (logit == v1))
    is_e2 = is_e & (lane != i1)
    v2 = top(is_e2)
    i2 = first(is_e2 & (logit == v2))
    e2 = jnp.exp(v2 - v1)
    comb_ref[...] = (jnp.where(lane == i1, p_top / (1.0 + e2), 0.0)
                     + jnp.where(lane == i2, p_top * e2 / (1.0 + e2), 0.0))


def _post(att2d, y2d, x2d, gate1, shift2, scale2, mod_index, tm, wts):
    t, d = x2d.shape
    rmod = gate1.shape[1]
    row = lambda w: pl.BlockSpec((tm, w), lambda i: (i, 0))
    mod = pl.BlockSpec((1, rmod, d), mod_index)
    full = lambda a: pl.BlockSpec(a.shape, lambda i: (0,) * a.ndim)
    consts = [wts["g_att"], wts["w_out"], wts["g2"], wts["wr_hi"], wts["wr_lo"], wts["br"]]
    return pl.pallas_call(
        _post_body,
        grid=(t // tm,),
        in_specs=[row(ATT_WIDTH), row(SSM_WIDTH), row(d), mod, mod, mod] + [full(a) for a in consts],
        out_specs=[row(d), row(d), row(LANES)],
        out_shape=[jax.ShapeDtypeStruct((t, d), F32), jax.ShapeDtypeStruct((t, d), BF16),
                   jax.ShapeDtypeStruct((t, LANES), F32)],
        compiler_params=_params("arbitrary"),
        name="post",
    )(att2d, y2d, x2d, gate1, shift2, scale2, *consts)


def _moe_body(h_ref, wg_ref, wu_ref, wd_ref, comb_ref, x1_ref, g2m_ref, o_ref, acc_ref):
    g = pl.program_id(1)

    @pl.when(g == 0)
    def _():
        acc_ref[...] = jnp.zeros_like(acc_ref)

    h = h_ref[...]
    comb = comb_ref[...]
    lane = lax.broadcasted_iota(jnp.int32, comb.shape, 1)
    first = RT_E0 + E_PER_GROUP * g
    he = []
    for j in range(E_PER_GROUP):
        cw = jnp.sum(jnp.where(lane == first + j, comb, 0.0), axis=-1, keepdims=True)
        he.append((_silu(_dot(h, wg_ref[j])) * _dot(h, wu_ref[j]) * cw).astype(BF16))
    acc_ref[...] += _dot(jnp.concatenate(he, axis=1), wd_ref[...])

    @pl.when(g == N_EGROUPS - 1)
    def _():
        o_ref[...] = x1_ref[...] + g2m_ref[0] * acc_ref[...]


def _moe(h2, comb, x1, gate2, mod_index, tm, wts):
    t, d = x1.shape
    rmod = gate2.shape[1]
    gff = E_PER_GROUP * EXPERT_FF
    row = lambda w: pl.BlockSpec((tm, w), lambda i, g: (i, 0))
    return pl.pallas_call(
        _moe_body,
        grid=(t // tm, N_EGROUPS),
        in_specs=[row(d),
                  pl.BlockSpec((E_PER_GROUP, d, EXPERT_FF), lambda i, g: (g, 0, 0)),
                  pl.BlockSpec((E_PER_GROUP, d, EXPERT_FF), lambda i, g: (g, 0, 0)),
                  pl.BlockSpec((None, gff, d), lambda i, g: (g, 0, 0)),
                  row(LANES), row(d),
                  pl.BlockSpec((1, rmod, d), lambda i, g: mod_index(i))],
        out_specs=row(d),
        out_shape=jax.ShapeDtypeStruct((t, d), F32),
        scratch_shapes=[pltpu.VMEM((tm, d), F32)],
        compiler_params=_params("arbitrary", "arbitrary"),
        name="moe",
    )(h2, wts["w_gate"], wts["w_up"], wts["w_down"], comb, x1, gate2)


def _page_fetch(pt_ref, cache_ref, buf, sem, npages):
    b = pl.program_id(0)
    nb = pl.num_programs(0)
    slot = lax.rem(b, PAGE_SLOTS)
    ahead = PAGE_SLOTS - 1

    def start_all(bb, unrolled):
        sl = lax.rem(bb, PAGE_SLOTS)

        def start(p):
            pltpu.make_async_copy(cache_ref.at[pt_ref[bb, p]], buf.at[sl, p], sem.at[sl]).start()

        if unrolled:
            for p in range(npages):
                start(p)
        else:
            lax.fori_loop(0, npages, lambda p, carry: (start(p), carry)[1], 0)

    for first in range(ahead):
        @pl.when((b == 0) & (first < nb))
        def _():
            start_all(first, False)

    @pl.when(b + ahead < nb)
    def _():
        start_all(b + ahead, True)

    for p in range(npages):
        pltpu.make_async_copy(cache_ref.at[0], buf.at[slot, p], sem.at[slot]).wait()
    return slot


def _cmp_sample_body(pt_ref, cache_ref, qbd_ref, pw_all_ref, gkc_ref, bd_ref, pair_ref, rsum_ref, rexp_ref,
                     ocmp_ref, mask_ref, buf, sem, kc_scr, *, npages, past_len, tdec):
    slot = _page_fetch(pt_ref, cache_ref, buf, sem, npages)
    nblk = past_len // SEL_BLOCK
    ppi = SUBLANES // (PAGE_SIZE // CMP_BLOCK)
    kw = KV_WIDTH

    tiles = min(POOL_UNROLL, npages // ppi)

    def pool(i, carry):
        for u in range(tiles):
            kc8 = jnp.zeros((SUBLANES, kw), F32)
            vc8 = jnp.zeros((SUBLANES, kw), F32)
            for j in range(ppi):
                r1 = _dot_nt(pw_all_ref[j], buf[slot, (i * tiles + u) * ppi + j].astype(BF16))
                s8 = SUBLANES
                kc8 = kc8 + r1[0:s8, :kw] + r1[s8:2 * s8, :kw]
                vc8 = vc8 + r1[2 * s8:3 * s8, kw:] + r1[3 * s8:, kw:]
            rows = pl.ds(pl.multiple_of((i * tiles + u) * SUBLANES, SUBLANES), SUBLANES)
            kc_scr[rows, 0:kw] = kc8
            kc_scr[rows, kw:] = vc8
        return carry
    lax.fori_loop(0, npages // (ppi * tiles), pool, 0)

    kcn = _head_rms(kc_scr[:, 0:kw], bd_ref[...]) * gkc_ref[...]
    vc = kc_scr[:, kw:]
    s = _dot_nt(qbd_ref[...], kcn.astype(BF16))
    e = jnp.exp2(s - s.max(axis=-1, keepdims=True))
    p = e / jnp.maximum(e.sum(axis=-1, keepdims=True), TINY)
    o = _dot(p.astype(BF16), vc.astype(BF16))
    nrow = o.shape[0]
    first_group = lax.broadcasted_iota(jnp.int32, (nrow, HEAD_DIM), 0) < nrow // N_KV
    ocmp_ref[...] = jnp.where(first_group, o[:, :HEAD_DIM], o[:, HEAD_DIM:])

    imp = _dot_exact_rhs(_dot_exact_lhs(rsum_ref[...], p), pair_ref[...])
    lane = lax.broadcasted_iota(jnp.int32, imp.shape, 1)
    tok = lax.rem(lax.broadcasted_iota(jnp.int32, imp.shape, 0), tdec)
    cur = (past_len + tok) // SEL_BLOCK
    allowed = (lane < cur) & (lane < nblk)
    forced = (lane == 0) | (lane == cur - 1)
    score = jnp.where(allowed, jnp.where(forced, BIG, imp), NEG)
    selm = jnp.where(allowed, _topk_mask_lanes(score, nblk, min(N_SEL - 1, nblk)), 0.0)
    mask_ref[...] = _dot(rexp_ref[...], selm.astype(BF16))


def _sel_sample_body(pt_ref, cache_ref, qbd_ref, mask_ref, eexp_ref, selnew_ref, winc_ref, winnew_ref, gt_ref,
                     ocmp_ref, o_ref, buf, sem, *, npages, past_len, tdec, ppc):
    slot = _page_fetch(pt_ref, cache_ref, buf, sem, npages)
    qbd = qbd_ref[...]
    nrow = qbd.shape[0]
    tok = lax.rem(lax.broadcasted_iota(jnp.int32, (nrow, 1), 0), tdec)
    kch = ppc * PAGE_SIZE
    kw = KV_WIDTH

    def attend(pieces):
        ss = [jnp.where(mask, s, NEG) for s, mask, _ in pieces]
        m = functools.reduce(jnp.maximum, [s.max(axis=-1, keepdims=True) for s in ss])
        es = [jnp.where(mask, jnp.exp2(s - m), 0.0) for s, (_, mask, _) in zip(ss, pieces)]
        l = sum(e.sum(axis=-1, keepdims=True) for e in es)
        acc = sum(pv(e.astype(BF16)) for e, (_, _, pv) in zip(es, pieces))
        o = acc / jnp.maximum(l, TINY)
        first_group = lax.broadcasted_iota(jnp.int32, (nrow, HEAD_DIM), 0) < nrow // N_KV
        return jnp.where(first_group, o[:, :HEAD_DIM], o[:, HEAD_DIM:])

    def feature_major(kt, vt, mask):
        vtb = vt.astype(BF16)
        return _dot(qbd, kt.astype(BF16)), mask, lambda e: _dot_nt(e, vtb)

    def new_rows(new_ref):
        kv = new_ref[...]
        idx = lax.broadcasted_iota(jnp.int32, (nrow, kv.shape[0]), 1)
        v = kv[:, kw:].astype(BF16)
        return _dot_nt(qbd, kv[:, :kw].astype(BF16)), (idx <= tok) & (idx < tdec), lambda e: _dot(e, v)

    selm = mask_ref[...].astype(BF16)
    pieces = []
    for c in range(npages // ppc):
        pages = [buf[slot, c * ppc + j] for j in range(ppc)]
        kt = jnp.concatenate([pg[:kw] for pg in pages], axis=1)
        vt = jnp.concatenate([pg[kw:] for pg in pages], axis=1)
        pieces.append(feature_major(kt, vt, _dot(selm, eexp_ref[:, c * kch:(c + 1) * kch]) > 0.5))
    o_sel = attend(pieces + [new_rows(selnew_ref)])

    wbuf = winc_ref.shape[1]
    rel = wbuf + tok - lax.broadcasted_iota(jnp.int32, (nrow, wbuf), 1)
    o_win = attend([feature_major(winc_ref[0:kw, :], winc_ref[kw:, :], (rel >= 0) & (rel < WINDOW)),
                    new_rows(winnew_ref)])

    gates = _sigmoid(gt_ref[...])
    o_ref[...] = gates[:, 0:1] * ocmp_ref[...] + gates[:, 1:2] * o_sel + gates[:, 2:3] * o_win


def _attn_sample(qbd, gt, page_table, cache_cmp_t, cache_sel_t, cache_win_t, sel_new, win_new, wts, past_len, tdec):
    nb, nrow, _ = qbd.shape
    npages = page_table.shape[1]
    ncb = past_len // CMP_BLOCK
    per_b = lambda r, w: pl.BlockSpec((None, r, w), lambda b, pt: (b, 0, 0))
    full = lambda a: pl.BlockSpec(a.shape, lambda b, pt: (0,) * a.ndim)
    hbm = pl.BlockSpec(memory_space=pl.ANY)
    page_buf = pltpu.VMEM((PAGE_SLOTS, npages, 2 * KV_WIDTH, PAGE_SIZE), F32)
    page_sem = pltpu.SemaphoreType.DMA((PAGE_SLOTS,))

    r = np.arange(nrow)
    grp = (r // tdec) // Q_PER_KV * tdec + r % tdec
    rsum = (np.arange(N_KV * tdec)[:, None] == grp[None, :]).astype(np.float32)
    pair = np.zeros((ncb, LANES), np.float32)
    pair[np.arange(ncb), np.arange(ncb) // (SEL_BLOCK // CMP_BLOCK)] = 1.0
    eexp = (np.arange(LANES)[:, None] == np.arange(past_len)[None, :] // SEL_BLOCK).astype(np.float32)
    consts1 = [wts["pw_all"], wts["gkc"], wts["bd_kv"], jnp.asarray(pair, BF16),
               jnp.asarray(rsum, BF16), jnp.asarray(rsum.T, BF16)]

    o_cmp, mask = pl.pallas_call(
        functools.partial(_cmp_sample_body, npages=npages, past_len=past_len, tdec=tdec),
        grid_spec=pltpu.PrefetchScalarGridSpec(
            num_scalar_prefetch=1, grid=(nb,),
            in_specs=[hbm, per_b(nrow, KV_WIDTH)] + [full(a) for a in consts1],
            out_specs=[per_b(nrow, HEAD_DIM), per_b(nrow, LANES)],
            scratch_shapes=[page_buf, page_sem, pltpu.VMEM((ncb, 2 * KV_WIDTH), F32)]),
        out_shape=[jax.ShapeDtypeStruct((nb, nrow, HEAD_DIM), F32), jax.ShapeDtypeStruct((nb, nrow, LANES), F32)],
        compiler_params=_params("arbitrary"),
        name="attn_sample_cmp",
    )(page_table, cache_cmp_t, qbd, *consts1)

    ppc = min(8, npages)
    eexp = jnp.asarray(eexp, BF16)
    return pl.pallas_call(
        functools.partial(_sel_sample_body, npages=npages, past_len=past_len, tdec=tdec, ppc=ppc),
        grid_spec=pltpu.PrefetchScalarGridSpec(
            num_scalar_prefetch=1, grid=(nb,),
            in_specs=[hbm, per_b(nrow, KV_WIDTH), per_b(nrow, LANES), full(eexp),
                      per_b(sel_new.shape[1], 2 * KV_WIDTH), per_b(2 * KV_WIDTH, cache_win_t.shape[2]),
                      per_b(win_new.shape[1], 2 * KV_WIDTH), per_b(nrow, SUBLANES), per_b(nrow, HEAD_DIM)],
            out_specs=per_b(nrow, HEAD_DIM),
            scratch_shapes=[page_buf, page_sem]),
        out_shape=jax.ShapeDtypeStruct((nb, nrow, HEAD_DIM), F32),
        compiler_params=_params("arbitrary"),
        name="attn_sample_sel",
    )(page_table, cache_sel_t, qbd, mask, eexp, sel_new, cache_win_t, win_new, gt, o_cmp)


def _pool_weights(w_pos_k, w_pos_v):
    bpp = PAGE_SIZE // CMP_BLOCK
    ppi = SUBLANES // bpp
    rows = np.arange(PAGE_SIZE)
    place = np.zeros((ppi, SUBLANES, PAGE_SIZE), np.float32)
    for j in range(ppi):
        place[j, j * bpp + rows // CMP_BLOCK, rows] = 1.0
    place = jnp.asarray(place)

    def hi_lo(w):
        full = place * jnp.tile(w, bpp)[None, None, :]
        hi = full.astype(BF16)
        return hi, (full - hi.astype(F32)).astype(BF16)

    k_hi, k_lo = hi_lo(w_pos_k)
    v_hi, v_lo = hi_lo(w_pos_v)
    return jnp.concatenate([k_hi, k_lo, v_hi, v_lo], axis=1)


def _prep_weights(l, g_norm1, g_norm2, w_in, g_q, g_k_cmp, g_k_sel, g_k_win, w_pos_k, w_pos_v, conv_w, conv_b,
                  dt_bias, a_log, d_skip, g_att_out, g_ssm_out, w_out, w_rg, b_rg, w_re, b_re, w_gate, w_up, w_down):
    w = w_in[l]
    o_gt = ATT_WIDTH + 6 * KV_WIDTH
    o_z = o_gt + 3 * N_HEADS
    o_xbc = o_z + SSM_WIDTH
    o_dt = o_xbc + CONV_DIM
    pad = jnp.zeros((D_MODEL, PK_WIDTH - PK_GD - SSM_HEADS - 3 * N_HEADS), F32)
    w_packed = jnp.concatenate([w[:, :o_gt], w[:, o_z:o_xbc], w[:, o_xbc:o_dt], w[:, o_dt:], w[:, o_gt:o_z], pad],
                               axis=1).astype(BF16)
    seg = np.arange(KV_WIDTH) // HEAD_DIM
    bd_kv = jnp.asarray((seg[:, None] == seg[None, :]).astype(np.float32), BF16)
    lane_pad = lambda v: jnp.pad(v.astype(F32), (0, LANES - v.shape[0])).reshape(1, LANES)
    wr = jnp.concatenate([w_rg[l], jnp.transpose(w_re[l], (1, 0, 2)).reshape(D_MODEL, N_EXPERTS)], axis=1)
    wr = jnp.pad(wr, ((0, 0), (0, LANES - wr.shape[1])))
    wr_hi = wr.astype(BF16)
    wr_lo = (wr - wr_hi.astype(F32)).astype(BF16)
    return {
        "g1": g_norm1[l].reshape(1, D_MODEL), "w_in": w_packed,
        "gq_col": (jnp.tile(g_q[l], N_HEADS) * (SCALE * LOG2E)).reshape(ATT_WIDTH, 1),
        "gks": jnp.tile(g_k_sel[l], N_KV).reshape(1, KV_WIDTH),
        "gkw": jnp.tile(g_k_win[l], N_KV).reshape(1, KV_WIDTH),
        "gkc": jnp.tile(g_k_cmp[l], N_KV).reshape(1, KV_WIDTH),
        "bd_kv": bd_kv,
        "wkv": jnp.concatenate([jnp.broadcast_to(w_pos_k[l][:, None], (CMP_BLOCK, KV_WIDTH)),
                                jnp.broadcast_to(w_pos_v[l][:, None], (CMP_BLOCK, KV_WIDTH))], axis=1),
        "pw_all": _pool_weights(w_pos_k[l], w_pos_v[l]),
        "conv_w": conv_w[l], "conv_b": conv_b[l].reshape(1, CONV_DIM),
        "dt_bias": lane_pad(dt_bias[l]), "a_log": lane_pad(a_log[l]), "d_skip": jnp.repeat(d_skip[l].astype(F32), SSM_HEAD_DIM).reshape(1, SSM_WIDTH),
        "g_ssm": g_ssm_out[l].reshape(1, SSM_WIDTH),
        "g_att": g_att_out[l].reshape(1, ATT_WIDTH), "w_out": w_out[l].astype(BF16),
        "g2": g_norm2[l].reshape(1, D_MODEL), "wr_hi": wr_hi, "wr_lo": wr_lo,
        "br": lane_pad(jnp.concatenate([b_rg[l], b_re[l].reshape(-1)])),
        "w_gate": w_gate[l].astype(BF16), "w_up": w_up[l].astype(BF16),
        "w_down": w_down[l].astype(BF16).reshape(N_EGROUPS, E_PER_GROUP * EXPERT_FF, D_MODEL),
    }


def _finish(att2d, y2d, x2d, mods, mod_index, tm_post, tm_moe, moe_index, wts):
    x1, h2, comb = _post(att2d, y2d, x2d, mods[2], mods[3], mods[4], mod_index, tm_post, wts)
    return _moe(h2, comb, x1, mods[5], moe_index, tm_moe, wts)


def _token_major_cache(t):
    b, _, n = t.shape
    return jnp.transpose(t.reshape(b, 2, N_KV, HEAD_DIM, n), (0, 4, 1, 2, 3))[None]


def _feature_major_cache(c):
    n, rows = c.shape[:2]
    return jnp.transpose(c, (0, 2, 3, 4, 1)).reshape(n, 2 * KV_WIDTH, rows)


def kernel(x_prompt, x_sample, cache_cmp, cache_sel, cache_win, state_ssm, state_conv, page_table, c_prompt, c_sample,
           g_norm1, g_norm2, w_ada, b_ada, w_in, g_q, g_k_cmp, g_k_sel, g_k_win, w_pos_k, w_pos_v, conv_w, conv_b,
           dt_bias, a_log, d_skip, g_att_out, g_ssm_out, w_out, w_rg, b_rg, w_re, b_re, w_gate, w_up, w_down):
    depth = w_in.shape[0]
    assert depth == 1
    l = 0
    bp, seq, d = x_prompt.shape
    bs, tdec, _ = x_sample.shape
    npages = page_table.shape[1]
    past_len = npages * PAGE_SIZE
    tp, ts = bp * seq, bs * tdec

    wts = _prep_weights(l, g_norm1, g_norm2, w_in, g_q, g_k_cmp, g_k_sel, g_k_win, w_pos_k, w_pos_v, conv_w, conv_b,
                        dt_bias, a_log, d_skip, g_att_out, g_ssm_out, w_out, w_rg, b_rg, w_re, b_re, w_gate, w_up,
                        w_down)
    m_all = _ada(jnp.concatenate([c_prompt, c_sample], axis=0), w_ada[l], b_ada[l])
    mods_p = [m.reshape(bp, 1, d) for m in jnp.split(m_all[:bp], 6, axis=-1)]
    tm_s = min(512, ts)
    mods_s = [jnp.repeat(m, tdec, axis=0).reshape(ts // tm_s, tm_s, d) for m in jnp.split(m_all[bp:], 6, axis=-1)]

    tm = 512
    per_seq = seq // tm
    xp2 = x_prompt.reshape(tp, d)
    qt, cmpt_p, selt_p, selk, selvt, wint_p, wink, winvt, z, xbc, gd, gdt, kc = _in_proj(
        xp2, bp, mods_p[0], mods_p[1], lambda i: (i // per_seq, 0, 0), tm, wts, True)
    ncb = seq // CMP_BLOCK
    nch = seq // KEY_CHUNK
    r3 = lambda a, n: a.reshape(bp, n, a.shape[-1])
    kc_eo = kc.reshape(bp, ncb // 2, 2, 2 * KV_WIDTH).transpose(0, 2, 1, 3).reshape(bp, ncb, 2 * KV_WIDTH)
    kcmp = kc_eo[:, :, :KV_WIDTH].astype(BF16)
    vcmp_t = jnp.transpose(kc_eo[:, :, KV_WIDTH:], (0, 2, 1)).astype(BF16)
    chunked = lambda a: a.reshape(bp, nch, KV_WIDTH, KEY_CHUNK)
    att = _attn_prompt(qt, r3(selk, seq), chunked(selvt), r3(wink, seq), chunked(winvt), kcmp, vcmp_t, gdt)
    y_ssm, ssm_p, conv_p = _ssd(r3(xbc, seq), r3(z, seq), r3(gd, seq), jnp.zeros((bp, SUBLANES, CONV_DIM), F32),
                                jnp.zeros((bp, SSM_HEADS, SSM_HEAD_DIM, SSM_STATE), F32), wts)
    tm_moe = 512
    y_prompt = _finish(att.reshape(tp, ATT_WIDTH), y_ssm.reshape(tp, SSM_WIDTH), xp2, mods_p,
                       lambda i: (i // per_seq, 0, 0), tm, tm_moe, lambda i: (i // (seq // tm_moe), 0, 0), wts)

    xs2 = x_sample.reshape(ts, d)
    qt_s, cmpt_s, selt_s, _, _, wint_s, _, _, z, xbc, gd, _ = _in_proj(
        xs2, 1, mods_s[0], mods_s[1], lambda i: (i, 0, 0), tm_s, wts, False)
    nrow = N_HEADS * tdec
    qh = qt_s[0].T.reshape(bs, tdec, N_HEADS, HEAD_DIM).transpose(0, 2, 1, 3).reshape(bs, nrow, HEAD_DIM)
    g0 = (jnp.arange(nrow) < nrow // N_KV)[None, :, None]
    zero = jnp.zeros_like(qh)
    qbd = jnp.concatenate([jnp.where(g0, qh, zero), jnp.where(g0, zero, qh)], axis=-1)
    gt = gd[:, GD_GATE0:GD_GATE0 + 3 * N_HEADS].reshape(bs, tdec, 3, N_HEADS).transpose(0, 3, 1, 2)
    gt = jnp.pad(gt.reshape(bs, nrow, 3), ((0, 0), (0, 0), (0, SUBLANES - 3)))
    rows_s = lambda t: t[0].T.reshape(bs, tdec, 2 * KV_WIDTH)
    sel_s, win_s = rows_s(selt_s), rows_s(wint_s)
    new_pad = 2 * SUBLANES - tdec
    padded = lambda a: jnp.pad(a, ((0, 0), (0, new_pad), (0, 0)))
    cwin_t = _feature_major_cache(cache_win[l])
    att_s = _attn_sample(qbd, gt, page_table, _feature_major_cache(cache_cmp[l]), _feature_major_cache(cache_sel[l]),
                         cwin_t, padded(sel_s), padded(win_s), wts, past_len, tdec)
    att_s = att_s.reshape(bs, N_HEADS, tdec, HEAD_DIM).transpose(0, 2, 1, 3).reshape(ts, ATT_WIDTH)

    batch_minor = lambda a: jnp.transpose(a.reshape(bs, tdec, a.shape[-1]), (1, 2, 0))
    xbc3 = xbc.reshape(bs, tdec, CONV_DIM)
    y_t, ssm_t = _ssd_decode(batch_minor(xbc), jnp.transpose(state_conv[l], (1, 2, 0)), batch_minor(z),
                             batch_minor(gd[:, :SSM_HEADS]), jnp.transpose(state_ssm[l].astype(F32), (1, 2, 3, 0)),
                             wts)
    y_s = jnp.transpose(y_t, (2, 0, 1)).reshape(ts, SSM_WIDTH)
    ssm_s = jnp.transpose(ssm_t, (3, 0, 1, 2))
    conv_s = jnp.concatenate([state_conv[l], xbc3], axis=1)[:, tdec:]
    y_sample = _finish(att_s, y_s, xs2, mods_s, lambda i: (i, 0, 0), tm_s, tm_s, lambda i: (i, 0, 0), wts)

    kv6 = lambda a: a.reshape(1, bs, tdec, 2, N_KV, HEAD_DIM)
    wkeep = min(WINDOW, seq)
    wbuf = cwin_t.shape[2]
    wkeep_s = min(WINDOW, wbuf + tdec)
    win_all_t = jnp.concatenate([cwin_t, jnp.transpose(win_s, (0, 2, 1))], axis=2)[:, :, wbuf + tdec - wkeep_s:]
    tail = lambda a: a[:, SUBLANES - (CONV_W - 1):][None]
    return (y_prompt.reshape(bp, seq, d), y_sample.reshape(bs, tdec, d),
            _token_major_cache(cmpt_p), kv6(rows_s(cmpt_s)), _token_major_cache(selt_p), kv6(sel_s),
            _token_major_cache(wint_p[:, :, seq - wkeep:]), _token_major_cache(win_all_t),
            ssm_p[None], ssm_s[None].astype(state_ssm.dtype), tail(conv_p), conv_s[None])
```

```python
import functools

import numpy as np
import jax
import jax.numpy as jnp
from jax import lax
from jax.experimental import pallas as pl
from jax.experimental.pallas import tpu as pltpu

F32 = jnp.float32
BF16 = jnp.bfloat16

D_MODEL = 1024
PAGE_SIZE = 128
N_HEADS = 8
N_KV = 2
HEAD_DIM = 64
Q_PER_KV = N_HEADS // N_KV
ATT_WIDTH = N_HEADS * HEAD_DIM
KV_WIDTH = N_KV * HEAD_DIM
CMP_BLOCK = 32
SEL_BLOCK = 64
N_SEL = 16
WINDOW = 512
SCALE = HEAD_DIM ** -0.5
SSM_HEADS = 8
SSM_HEAD_DIM = 64
SSM_WIDTH = SSM_HEADS * SSM_HEAD_DIM
SSM_GROUPS = 2
SSM_STATE = 64
CONV_W = 4
SSD_CHUNK = 128
CONV_DIM = SSM_WIDTH + 2 * SSM_GROUPS * SSM_STATE
N_EGROUPS = 4
E_PER_GROUP = 4
N_EXPERTS = N_EGROUPS * E_PER_GROUP
EXPERT_FF = 256
EPS = 1e-6
NEG = -1e30
BIG = 1e9
TINY = 1e-30
LOG2E = 1.4426950408889634

LANES = 128
SUBLANES = 8
VMEM_LIMIT = 48 * 1024 * 1024
Q_TILE = 256
KEY_CHUNK = 128
SEL_UNROLL = 4
WIN_UNROLL = 6
DEAD_SHIFT = 1 << 24
PAGE_SLOTS = 3
POOL_UNROLL = 16

PK_Q = 0
PK_CMP = 512
PK_SEL = 768
PK_WIN = 1024
PK_Z = 1280
PK_XBC = 1792
PK_GD = 2560
PK_WIDTH = 2688
GD_GATE0 = 8
RT_E0 = 4


def _dot(a, b):
    return jnp.dot(a, b, preferred_element_type=F32)


def _dot_nt(a, b):
    return lax.dot_general(a, b, (((1,), (1,)), ((), ())), preferred_element_type=F32)


def _dot_tn(a, b):
    return lax.dot_general(a, b, (((0,), (0,)), ((), ())), preferred_element_type=F32)


def _split2(x):
    hi = x.astype(BF16)
    lo = (x - hi.astype(F32)).astype(BF16)
    return hi, lo


def _split3(x):
    hi = x.astype(BF16)
    r = x - hi.astype(F32)
    mid = r.astype(BF16)
    lo = (r - mid.astype(F32)).astype(BF16)
    return hi, mid, lo


def _dot_exact_lhs(a_bf16, x):
    return sum(_dot(a_bf16, p) for p in _split3(x))


def _dot_exact_rhs(x, b_bf16):
    return sum(_dot(p, b_bf16) for p in _split3(x))


def _sigmoid(x):
    return 1.0 / (1.0 + jnp.exp(-x))


def _silu(x):
    return x * _sigmoid(x)


def _softplus(x):
    return jnp.maximum(x, 0.0) + jnp.log1p(jnp.exp(-jnp.abs(x)))


def _rms(x):
    return x * lax.rsqrt(jnp.mean(x * x, axis=-1, keepdims=True) + EPS)


def _head_rms(x, bd):
    hi, lo = _split2(x * x)
    ms = (_dot(hi, bd) + _dot(lo, bd)) * (1.0 / HEAD_DIM)
    return x * lax.rsqrt(ms + EPS)


def _topk_mask_lanes(score, nblk, k):
    lane = lax.broadcasted_iota(jnp.int32, score.shape, 1)
    rank = jnp.zeros(score.shape, F32)
    for i in range(nblk):
        si = score[:, i:i + 1]
        beats = jnp.where(si > score, 1.0, jnp.where(si == score, jnp.where(lane > i, 1.0, 0.0), 0.0))
        rank = rank + beats
    return jnp.where(rank < k, 1.0, 0.0)


def _topk_mask_rows(score, k):
    row = lax.broadcasted_iota(jnp.int32, score.shape, 0)
    rank = jnp.zeros(score.shape, F32)
    for i in range(score.shape[0]):
        si = score[i:i + 1, :]
        beats = jnp.where(si > score, 1.0, jnp.where(si == score, jnp.where(row > i, 1.0, 0.0), 0.0))
        rank = rank + beats
    return jnp.where(rank < k, 1.0, 0.0)


def _params(*sem):
    return pltpu.CompilerParams(dimension_semantics=sem, vmem_limit_bytes=VMEM_LIMIT)


def _ada_body(c_ref, w_ref, b_ref, o_ref):
    c = _silu(c_ref[...]).astype(BF16)
    o_ref[...] = _dot(c, w_ref[...].astype(BF16)) + b_ref[...]


def _ada(c_all, w_ada, b_ada):
    n, d = c_all.shape
    width = w_ada.shape[1]
    tn = 1536
    return pl.pallas_call(
        _ada_body,
        grid=(width // tn,),
        in_specs=[pl.BlockSpec((n, d), lambda j: (0, 0)),
                  pl.BlockSpec((d, tn), lambda j: (0, j)),
                  pl.BlockSpec((1, tn), lambda j: (0, j))],
        out_specs=pl.BlockSpec((n, tn), lambda j: (0, j)),
        out_shape=jax.ShapeDtypeStruct((n, width), F32),
        compiler_params=_params("arbitrary"),
        name="ada",
    )(c_all, w_ada, b_ada.reshape(1, width))


def _inproj_body(with_cmp, x_ref, sh_ref, sc_ref, g1_ref, w_ref, gqc_ref, gks_ref, gkw_ref, bd_ref, *rest):
    if with_cmp:
        wkv_ref, gkc_ref = rest[:2]
        rest = rest[2:]
    (qt_ref, cmpt_ref, selt_ref, selk_ref, selvt_ref, wint_ref, wink_ref, winvt_ref,
     z_ref, xbc_ref, gd_ref, gdt_ref) = rest[:12]
    tm = x_ref.shape[0]
    h = _rms(x_ref[...]) * g1_ref[...]
    hb = (h * (1.0 + sc_ref[0]) + sh_ref[0]).astype(BF16)
    bd_kv = bd_ref[...]

    q3 = _dot(hb, w_ref[:, PK_Q:PK_CMP]).T.reshape(N_HEADS, HEAD_DIM, tm)
    q3 = q3 * lax.rsqrt(jnp.mean(q3 * q3, axis=1, keepdims=True) + EPS)
    qt_ref[...] = (q3.reshape(ATT_WIDTH, tm) * gqc_ref[...]).astype(BF16)

    kv = _dot(hb, w_ref[:, PK_CMP:PK_SEL])
    cmpt_ref[...] = kv.T
    if with_cmp:
        kc_ref = rest[12]
        pooled = (kv.reshape(tm // CMP_BLOCK, CMP_BLOCK, 2 * KV_WIDTH) * wkv_ref[...][None]).sum(axis=1)
        kc_ref[:, 0:KV_WIDTH] = _head_rms(pooled[:, :KV_WIDTH], bd_kv) * gkc_ref[...]
        kc_ref[:, KV_WIDTH:] = pooled[:, KV_WIDTH:]

    def normed_kv(lo, hi, gain_ref, t_ref, k_ref, vt_ref):
        kv = _dot(hb, w_ref[:, lo:hi])
        kn = _head_rms(kv[:, :KV_WIDTH], bd_kv) * gain_ref[...]
        k_ref[...] = kn.astype(BF16)
        kvt = jnp.concatenate([kn, kv[:, KV_WIDTH:]], axis=1).T
        t_ref[...] = kvt
        vt = kvt[KV_WIDTH:, :].astype(BF16)
        for j in range(tm // KEY_CHUNK):
            vt_ref[j] = vt[:, j * KEY_CHUNK:(j + 1) * KEY_CHUNK]

    normed_kv(PK_SEL, PK_WIN, gks_ref, selt_ref, selk_ref, selvt_ref)
    normed_kv(PK_WIN, PK_Z, gkw_ref, wint_ref, wink_ref, winvt_ref)

    z_ref[...] = _dot(hb, w_ref[:, PK_Z:PK_XBC])
    xbc_ref[...] = _dot(hb, w_ref[:, PK_XBC:PK_GD])
    gd = _dot(hb, w_ref[:, PK_GD:PK_WIDTH])
    gd_ref[...] = gd
    gdt_ref[...] = gd.T


def _in_proj(x2d, nbatch, shift3, scale3, mod_index, tm, wts, with_cmp):
    t, d = x2d.shape
    seq = t // nbatch
    per_seq = seq // tm
    rmod = shift3.shape[1]
    full = lambda a: pl.BlockSpec(a.shape, lambda i: (0,) * a.ndim)
    row = lambda w: pl.BlockSpec((tm, w), lambda i: (i, 0))
    fmaj = lambda w: pl.BlockSpec((None, w, tm), lambda i: (i // per_seq, 0, i % per_seq))
    chunks = pl.BlockSpec((tm // KEY_CHUNK, KV_WIDTH, KEY_CHUNK), lambda i: (i, 0, 0))
    mod = pl.BlockSpec((1, rmod, d), mod_index)
    ins = [x2d, shift3, scale3, wts["g1"], wts["w_in"], wts["gq_col"], wts["gks"], wts["gkw"], wts["bd_kv"]]
    in_specs = [row(d), mod, mod] + [full(a) for a in ins[3:]]
    if with_cmp:
        ins += [wts["wkv"], wts["gkc"]]
        in_specs += [full(wts["wkv"]), full(wts["gkc"])]
    sds = jax.ShapeDtypeStruct
    fm = lambda w, dt: sds((nbatch, w, seq), dt)
    ck = sds((t // KEY_CHUNK, KV_WIDTH, KEY_CHUNK), BF16)
    out_shape = [fm(ATT_WIDTH, BF16), fm(2 * KV_WIDTH, F32),
                 fm(2 * KV_WIDTH, F32), sds((t, KV_WIDTH), BF16), ck,
                 fm(2 * KV_WIDTH, F32), sds((t, KV_WIDTH), BF16), ck,
                 sds((t, SSM_WIDTH), F32), sds((t, CONV_DIM), F32), sds((t, LANES), F32), fm(LANES, F32)]
    out_specs = [fmaj(ATT_WIDTH), fmaj(2 * KV_WIDTH),
                 fmaj(2 * KV_WIDTH), row(KV_WIDTH), chunks,
                 fmaj(2 * KV_WIDTH), row(KV_WIDTH), chunks,
                 row(SSM_WIDTH), row(CONV_DIM), row(LANES), fmaj(LANES)]
    if with_cmp:
        out_shape.append(sds((t // CMP_BLOCK, 2 * KV_WIDTH), F32))
        out_specs.append(pl.BlockSpec((tm // CMP_BLOCK, 2 * KV_WIDTH), lambda i: (i, 0)))
    return pl.pallas_call(
        functools.partial(_inproj_body, with_cmp),
        grid=(t // tm,),
        in_specs=in_specs,
        out_specs=out_specs,
        out_shape=out_shape,
        compiler_params=_params("arbitrary"),
        name="in_proj",
    )(*ins)


def _attn_prompt_body(qt_ref, selk_ref, selvt_ref, wink_ref, winvt_ref, kc_ref, vct_ref, gdt_ref, o_ref, selm_scr,
                      *, tq, seq):
    i = pl.program_id(1)
    ncb = seq // CMP_BLOCK
    nblk = seq // SEL_BLOCK
    bpc = KEY_CHUNK // SEL_BLOCK
    wide = Q_PER_KV * tq
    qt = qt_ref[...]
    t_row = i * tq + lax.broadcasted_iota(jnp.int32, (1, tq), 1)
    cur_row = t_row // SEL_BLOCK
    key_i = lax.broadcasted_iota(jnp.int32, (KEY_CHUNK, tq), 0)
    gates = _sigmoid(gdt_ref[GD_GATE0:GD_GATE0 + 3 * N_HEADS, :])
    rep = lambda a: jnp.concatenate([a] * Q_PER_KV, axis=1)

    def flash(k_ref, vt_ref, qgts, c_lo, c_hi, masks_fn, groups, unroll):
        last = seq // KEY_CHUNK - 1

        def step(j, carry):
            chunks = []
            for u in range(unroll):
                c = c_lo + unroll * j + u
                cc = jnp.minimum(c, last)
                off = pl.multiple_of(cc * KEY_CHUNK, KEY_CHUNK)
                scores = {g: _dot(k_ref[pl.ds(off, KEY_CHUNK), g * HEAD_DIM:(g + 1) * HEAD_DIM], qgts[g])
                          for g in groups}
                chunks.append((c, cc, scores))
            state = dict(zip(groups, carry))
            for c, cc, scores in chunks:
                masks = masks_fn(c, cc, c < c_hi)
                for g in groups:
                    m, l, acc = state[g]
                    s = jnp.where(rep(masks[g]), scores[g], NEG)
                    m_new = jnp.maximum(m, s.max(axis=0, keepdims=True))
                    alpha = jnp.exp2(m - m_new)
                    e = jnp.exp2(s - m_new)
                    pv = _dot(vt_ref[cc, g * HEAD_DIM:(g + 1) * HEAD_DIM, :], e.astype(BF16))
                    state[g] = (m_new, alpha * l + e.sum(axis=0, keepdims=True), alpha * acc + pv)
            return tuple(state[g] for g in groups)

        init = (jnp.full((1, wide), NEG, F32), jnp.zeros((1, wide), F32), jnp.zeros((HEAD_DIM, wide), F32))
        trips = (c_hi - c_lo + unroll - 1) // unroll
        res = lax.fori_loop(0, trips, step, (init,) * len(groups))
        return [acc / jnp.maximum(l, TINY) for _, l, acc in res]

    qgts, o_cmps = [], []
    for g in range(N_KV):
        qgt = jnp.concatenate(
            [qt[(g * Q_PER_KV + h) * HEAD_DIM:(g * Q_PER_KV + h + 1) * HEAD_DIM, :] for h in range(Q_PER_KV)], axis=1)
        qgts.append(qgt)

        kcg = kc_ref[:, g * HEAD_DIM:(g + 1) * HEAD_DIM]
        vctg = vct_ref[g * HEAD_DIM:(g + 1) * HEAD_DIM, :]
        r = lax.broadcasted_iota(jnp.int32, (ncb, tq), 0)
        cblk = jnp.where(r < nblk, 2 * r, 2 * (r - nblk) + 1)
        cmask = rep((cblk * CMP_BLOCK + CMP_BLOCK - 1) <= t_row)
        s = jnp.where(cmask, _dot(kcg, qgt), NEG)
        e = jnp.where(cmask, jnp.exp2(s - s.max(axis=0, keepdims=True)), 0.0)
        p = e / jnp.maximum(e.sum(axis=0, keepdims=True), TINY)
        o_cmps.append(_dot(vctg, p.astype(BF16)))

        imp = sum(p[:, h * tq:(h + 1) * tq] for h in range(Q_PER_KV))
        imp = imp[:nblk] + imp[nblk:]
        jb = lax.broadcasted_iota(jnp.int32, (nblk, tq), 0)
        allowed = jb < cur_row
        forced = (jb == 0) | (jb == cur_row - 1)
        score = jnp.where(allowed, jnp.where(forced, BIG, imp), NEG)
        selm = jnp.where(allowed, _topk_mask_rows(score, min(N_SEL - 1, nblk)), 0.0)
        for c in range(seq // KEY_CHUNK):
            selm_scr[g, c, 0:bpc] = selm[c * bpc:(c + 1) * bpc]

    def key_pos(c, live):
        return c * KEY_CHUNK + key_i + jnp.where(live, 0, DEAD_SHIFT)

    def sel_masks(c, cc, live):
        kp = key_pos(c, live)
        cur_blk = ((kp // SEL_BLOCK) == cur_row) & (kp <= t_row)
        alive = jnp.where(live, 1.0, 0.0)
        out = []
        for g in range(N_KV):
            rows = selm_scr[g, cc] * alive
            picked = jnp.concatenate(
                [jnp.broadcast_to(rows[j:j + 1], (SEL_BLOCK, tq)) for j in range(bpc)], axis=0) > 0.5
            out.append(picked | cur_blk)
        return out

    def win_masks(c, cc, live):
        rel = t_row - key_pos(c, live)
        return [(rel >= 0) & (rel < WINDOW)] * N_KV

    c_hi = ((i + 1) * tq + KEY_CHUNK - 1) // KEY_CHUNK
    c_win = jnp.maximum(i * tq - WINDOW + 1, 0) // KEY_CHUNK
    both = tuple(range(N_KV))
    o_sels = flash(selk_ref, selvt_ref, qgts, 0, c_hi, sel_masks, both, SEL_UNROLL)
    o_wins = flash(wink_ref, winvt_ref, qgts, c_win, c_hi, win_masks, both, WIN_UNROLL)

    heads = []
    for g in range(N_KV):
        for h in range(Q_PER_KV):
            hh = g * Q_PER_KV + h
            sl = slice(h * tq, (h + 1) * tq)
            gate = lambda br: gates[br * N_HEADS + hh:br * N_HEADS + hh + 1, :]
            heads.append(gate(0) * o_cmps[g][:, sl] + gate(1) * o_sels[g][:, sl] + gate(2) * o_wins[g][:, sl])
    o_ref[...] = jnp.concatenate(heads, axis=0).T


def _attn_prompt(qt, selk, selvt, wink, winvt, kc, vct, gdt):
    b, _, s = qt.shape
    tq = Q_TILE
    ncb = s // CMP_BLOCK
    nch = s // KEY_CHUNK
    tile_t = lambda w: pl.BlockSpec((None, w, tq), lambda bi, i: (bi, 0, i))
    whole = lambda shape: pl.BlockSpec((None,) + shape, lambda bi, i: (bi,) + (0,) * len(shape))
    return pl.pallas_call(
        functools.partial(_attn_prompt_body, tq=tq, seq=s),
        grid=(b, s // tq),
        in_specs=[tile_t(ATT_WIDTH), whole((s, KV_WIDTH)), whole((nch, KV_WIDTH, KEY_CHUNK)),
                  whole((s, KV_WIDTH)), whole((nch, KV_WIDTH, KEY_CHUNK)),
                  whole((ncb, KV_WIDTH)), whole((KV_WIDTH, ncb)), tile_t(LANES)],
        out_specs=pl.BlockSpec((None, tq, ATT_WIDTH), lambda bi, i: (bi, i, 0)),
        out_shape=jax.ShapeDtypeStruct((b, s, ATT_WIDTH), F32),
        scratch_shapes=[pltpu.VMEM((N_KV, nch, SUBLANES, tq), F32)],
        compiler_params=_params("arbitrary", "arbitrary"),
        name="attn_prompt",
    )(qt, selk, selvt, wink, winvt, kc, vct, gdt)


def _ssd_body(xbc_ref, z_ref, gd_ref, cprev_ref, h0_ref, cw_ref, cb_ref, dtb_ref, alog_ref, dsk_ref, gout_ref,
              hexp_ref, y_ref, hout_ref, cout_ref, h_scr, xe_scr, yoff_scr, *, cl):
    c = pl.program_id(1)
    gn = SSM_GROUPS * SSM_STATE
    hpg = SSM_HEADS // SSM_GROUPS

    @pl.when(c == 0)
    def _():
        h_scr[...] = h0_ref[...]
        xe_scr[0:SUBLANES] = cprev_ref[...]

    xe_scr[SUBLANES:SUBLANES + cl] = xbc_ref[...]
    base = SUBLANES - (CONV_W - 1)
    xc = cb_ref[...]
    for w in range(CONV_W):
        xc = xc + cw_ref[w:w + 1, :] * xe_scr[base + w:base + w + cl, :]
    tail = xe_scr[cl:cl + SUBLANES, :]
    cout_ref[...] = tail
    xe_scr[0:SUBLANES] = tail
    xc = _silu(xc)
    xs = xc[:, :SSM_WIDTH]
    bmb = xc[:, SSM_WIDTH:SSM_WIDTH + gn].astype(BF16)
    cmb = xc[:, SSM_WIDTH + gn:].astype(BF16)

    dt = _softplus(gd_ref[...] + dtb_ref[...])
    da = dt * (-jnp.exp(alog_ref[...]))
    ri = lax.broadcasted_iota(jnp.int32, (cl, cl), 0)
    ci = lax.broadcasted_iota(jnp.int32, (cl, cl), 1)
    causal = ri >= ci
    acs = _dot_exact_lhs(jnp.where(causal, 1.0, 0.0).astype(BF16), da)
    acs_t = acs.T
    acs_last = acs[cl - 1:cl, :]
    e_last = jnp.exp(acs_last)
    hexp = hexp_ref[...]
    dt_w = _dot_exact_rhs(dt, hexp)
    wend_w = _dot_exact_rhs(jnp.exp(acs_last - acs) * dt, hexp)
    eacs_w = _dot_exact_rhs(jnp.exp(acs), hexp)
    x_dt = (xs * dt_w).astype(BF16)
    x_we = (xs * wend_w).astype(BF16)

    for g in range(SSM_GROUPS):
        bg = bmb[:, g * SSM_STATE:(g + 1) * SSM_STATE]
        cg = cmb[:, g * SSM_STATE:(g + 1) * SSM_STATE]
        cb = _dot_nt(cg, bg)
        for hq in range(hpg):
            h = g * hpg + hq
            hs = slice(h * SSM_HEAD_DIM, (h + 1) * SSM_HEAD_DIM)
            seg = acs[:, h:h + 1] - acs_t[h:h + 1, :]
            decay = jnp.where(causal, jnp.exp(jnp.where(causal, seg, 0.0)), 0.0)
            y_ref[:, hs] = _dot((cb * decay).astype(BF16), x_dt[:, hs])
            hst = h_scr[h]
            yoff_scr[:, hs] = _dot_nt(cg, hst.astype(BF16))
            h_scr[h] = e_last[:, h:h + 1] * hst + _dot_tn(x_we[:, hs], bg)

    y = y_ref[...] + yoff_scr[...] * eacs_w + dsk_ref[...] * xs
    yg = y * _silu(z_ref[...])
    y_ref[...] = _rms(yg) * gout_ref[...]
    hout_ref[...] = h_scr[...]


def _ssd(xbc3, z3, gd3, cprev, h0, wts):
    b, l, _ = xbc3.shape
    cl = SSD_CHUNK
    assert cl == LANES and l % cl == 0
    nc = l // cl
    tile = lambda w: pl.BlockSpec((None, cl, w), lambda bi, c: (bi, c, 0))
    full = lambda a: pl.BlockSpec(a.shape, lambda bi, c: (0,) * a.ndim)
    names = ["conv_w", "conv_b", "dt_bias", "a_log", "d_skip", "g_ssm"]
    hexp = (np.arange(LANES)[:, None] == np.arange(SSM_WIDTH)[None, :] // SSM_HEAD_DIM).astype(np.float32)
    consts = [wts[n] for n in names] + [jnp.asarray(hexp, BF16)]
    hshape = (SSM_HEADS, SSM_HEAD_DIM, SSM_STATE)
    return pl.pallas_call(
        functools.partial(_ssd_body, cl=cl),
        grid=(b, nc),
        in_specs=[tile(CONV_DIM), tile(SSM_WIDTH), tile(LANES),
                  pl.BlockSpec((None, SUBLANES, CONV_DIM), lambda bi, c: (bi, 0, 0)),
                  pl.BlockSpec((None,) + hshape, lambda bi, c: (bi, 0, 0, 0))] + [full(a) for a in consts],
        out_specs=[tile(SSM_WIDTH),
                   pl.BlockSpec((None,) + hshape, lambda bi, c: (bi, 0, 0, 0)),
                   pl.BlockSpec((None, SUBLANES, CONV_DIM), lambda bi, c: (bi, 0, 0))],
        out_shape=[jax.ShapeDtypeStruct((b, l, SSM_WIDTH), F32),
                   jax.ShapeDtypeStruct((b,) + hshape, F32),
                   jax.ShapeDtypeStruct((b, SUBLANES, CONV_DIM), F32)],
        scratch_shapes=[pltpu.VMEM(hshape, F32), pltpu.VMEM((SUBLANES + cl, CONV_DIM), F32),
                        pltpu.VMEM((cl, SSM_WIDTH), F32)],
        compiler_params=_params("arbitrary", "arbitrary"),
        name="ssd",
    )(xbc3, z3, gd3, cprev, h0, *consts)


def _ssd_decode_body(xbc_ref, cprev_ref, z_ref, dt_ref, h_ref, cw_ref, cb_ref, dtb_ref, alog_ref, dsk_ref, gout_ref,
                     y_ref, hout_ref, xc_scr, dtv_scr, dec_scr, *, steps):
    hd = pl.program_id(0)
    gn = SSM_GROUPS * SSM_STATE
    hpg = SSM_HEADS // SSM_GROUPS

    @pl.when(hd == 0)
    def _():
        rows = [cprev_ref[j] for j in range(CONV_W - 1)] + [xbc_ref[t] for t in range(steps)]
        for t in range(steps):
            xc = cb_ref[...]
            for w in range(CONV_W):
                xc = xc + cw_ref[w] * rows[t + w]
            xc_scr[t] = _silu(xc)
        dt = _softplus(dt_ref[...] + dtb_ref[...][None])
        dtv_scr[...] = dt
        dec_scr[...] = jnp.exp(dt * (-jnp.exp(alog_ref[...]))[None])

    x_rows = pl.ds(pl.multiple_of(hd * SSM_HEAD_DIM, SSM_HEAD_DIM), SSM_HEAD_DIM)
    grp = hd // hpg
    b_rows = pl.ds(pl.multiple_of(SSM_WIDTH + grp * SSM_STATE, SSM_STATE), SSM_STATE)
    c_rows = pl.ds(pl.multiple_of(SSM_WIDTH + gn + grp * SSM_STATE, SSM_STATE), SSM_STATE)
    xs = [xc_scr[t, x_rows, :] for t in range(steps)]
    bs_ = [xc_scr[t, b_rows, :] for t in range(steps)]
    cs = [xc_scr[t, c_rows, :] for t in range(steps)]
    dec = [dec_scr[t, pl.ds(hd, 1), :] for t in range(steps)]
    xdt = [xs[t] * dtv_scr[t, pl.ds(hd, 1), :] for t in range(steps)]
    dsk = dsk_ref[x_rows, :]
    for p in range(SSM_HEAD_DIM):
        hp = h_ref[p]
        for t in range(steps):
            hp = dec[t] * hp + xdt[t][p:p + 1, :] * bs_[t]
            y_ref[t, pl.ds(hd * SSM_HEAD_DIM + p, 1), :] = jnp.sum(hp * cs[t], axis=0, keepdims=True)
        hout_ref[p] = hp
    for t in range(steps):
        y_ref[t, x_rows, :] = y_ref[t, x_rows, :] + dsk * xs[t]

    @pl.when(hd == SSM_HEADS - 1)
    def _():
        yg = y_ref[...] * _silu(z_ref[...])
        ms = jnp.mean(yg * yg, axis=1, keepdims=True)
        y_ref[...] = yg * lax.rsqrt(ms + EPS) * gout_ref[...][None]


def _ssd_decode(xbc_t, cprev_t, z_t, dt_t, h_t, wts):
    steps, _, nb = xbc_t.shape
    lanes = lambda v: jnp.broadcast_to(v.astype(F32)[..., None], v.shape + (nb,))
    consts = [lanes(wts["conv_w"]), lanes(wts["conv_b"][0]), lanes(wts["dt_bias"][0, :SSM_HEADS]),
              lanes(wts["a_log"][0, :SSM_HEADS]), lanes(wts["d_skip"][0]), lanes(wts["g_ssm"][0])]
    full = lambda a: pl.BlockSpec(a.shape, lambda hd: (0,) * a.ndim)
    per_head = pl.BlockSpec((None, SSM_HEAD_DIM, SSM_STATE, nb), lambda hd: (hd, 0, 0, 0))
    ins = [xbc_t, cprev_t, z_t, dt_t]
    return pl.pallas_call(
        functools.partial(_ssd_decode_body, steps=steps),
        grid=(SSM_HEADS,),
        in_specs=[full(a) for a in ins] + [per_head] + [full(a) for a in consts],
        out_specs=[full(z_t), per_head],
        out_shape=[jax.ShapeDtypeStruct(z_t.shape, F32), jax.ShapeDtypeStruct(h_t.shape, F32)],
        scratch_shapes=[pltpu.VMEM(xbc_t.shape, F32), pltpu.VMEM(dt_t.shape, F32), pltpu.VMEM(dt_t.shape, F32)],
        compiler_params=_params("arbitrary"),
        name="ssd_decode",
    )(*ins, h_t, *consts)


def _post_body(att_ref, y_ref, x_ref, g1m_ref, sh2_ref, sc2_ref, gatt_ref, wo_ref, gn2_ref, wrh_ref, wrl_ref, br_ref,
               x1_ref, h2_ref, comb_ref):
    an = (_rms(att_ref[...]) * gatt_ref[...]).astype(BF16)
    mixed = _dot(an, wo_ref[0:ATT_WIDTH, :]) + _dot(y_ref[...].astype(BF16), wo_ref[ATT_WIDTH:, :])
    x1 = x_ref[...] + g1m_ref[0] * mixed
    x1_ref[...] = x1
    h2 = _rms(x1) * gn2_ref[...] * (1.0 + sc2_ref[0]) + sh2_ref[0]
    h2_ref[...] = h2.astype(BF16)

    hh, hl = _split2(h2)
    logit = _dot(hh, wrh_ref[...]) + _dot(hl, wrh_ref[...]) + _dot(hh, wrl_ref[...]) + br_ref[...]
    lane = lax.broadcasted_iota(jnp.int32, logit.shape, 1).astype(F32)
    first = lambda cond: jnp.min(jnp.where(cond, lane, 1e6), axis=-1, keepdims=True)
    top = lambda cond: jnp.max(jnp.where(cond, logit, NEG), axis=-1, keepdims=True)
    is_g = lane < N_EGROUPS
    mg = top(is_g)
    gsel = first(is_g & (logit == mg))
    p_top = 1.0 / jnp.sum(jnp.where(is_g, jnp.exp(logit - mg), 0.0), axis=-1, keepdims=True)
    lo = RT_E0 + E_PER_GROUP * gsel
    is_e = (lane >= lo) & (lane < lo + E_PER_GROUP)
    v1 = top(is_e)
    i1 = first(is_e & (logit == v1))
    is_e2 = is_e & (lane != i1)
    v2 = top(is_e2)
    i2 = first(is_e2 & (logit == v2))
    e2 = jnp.exp(v2 - v1)
    comb_ref[...] = (jnp.where(lane == i1, p_top / (1.0 + e2), 0.0)
                     + jnp.where(lane == i2, p_top * e2 / (1.0 + e2), 0.0))


def _post(att2d, y2d, x2d, gate1, shift2, scale2, mod_index, tm, wts):
    t, d = x2d.shape
    rmod = gate1.shape[1]
    row = lambda w: pl.BlockSpec((tm, w), lambda i: (i, 0))
    mod = pl.BlockSpec((1, rmod, d), mod_index)
    full = lambda a: pl.BlockSpec(a.shape, lambda i: (0,) * a.ndim)
    consts = [wts["g_att"], wts["w_out"], wts["g2"], wts["wr_hi"], wts["wr_lo"], wts["br"]]
    return pl.pallas_call(
        _post_body,
        grid=(t // tm,),
        in_specs=[row(ATT_WIDTH), row(SSM_WIDTH), row(d), mod, mod, mod] + [full(a) for a in consts],
        out_specs=[row(d), row(d), row(LANES)],
        out_shape=[jax.ShapeDtypeStruct((t, d), F32), jax.ShapeDtypeStruct((t, d), BF16),
                   jax.ShapeDtypeStruct((t, LANES), F32)],
        compiler_params=_params("arbitrary"),
        name="post",
    )(att2d, y2d, x2d, gate1, shift2, scale2, *consts)


def _moe_body(h_ref, wg_ref, wu_ref, wd_ref, comb_ref, x1_ref, g2m_ref, o_ref, acc_ref):
    g = pl.program_id(1)

    @pl.when(g == 0)
    def _():
        acc_ref[...] = jnp.zeros_like(acc_ref)

    h = h_ref[...]
    comb = comb_ref[...]
    lane = lax.broadcasted_iota(jnp.int32, comb.shape, 1)
    first = RT_E0 + E_PER_GROUP * g
    he = []
    for j in range(E_PER_GROUP):
        cw = jnp.sum(jnp.where(lane == first + j, comb, 0.0), axis=-1, keepdims=True)
        he.append((_silu(_dot(h, wg_ref[j])) * _dot(h, wu_ref[j]) * cw).astype(BF16))
    acc_ref[...] += _dot(jnp.concatenate(he, axis=1), wd_ref[...])

    @pl.when(g == N_EGROUPS - 1)
    def _():
        o_ref[...] = x1_ref[...] + g2m_ref[0] * acc_ref[...]


def _moe(h2, comb, x1, gate2, mod_index, tm, wts):
    t, d = x1.shape
    rmod = gate2.shape[1]
    gff = E_PER_GROUP * EXPERT_FF
    row = lambda w: pl.BlockSpec((tm, w), lambda i, g: (i, 0))
    return pl.pallas_call(
        _moe_body,
        grid=(t // tm, N_EGROUPS),
        in_specs=[row(d),
                  pl.BlockSpec((E_PER_GROUP, d, EXPERT_FF), lambda i, g: (g, 0, 0)),
                  pl.BlockSpec((E_PER_GROUP, d, EXPERT_FF), lambda i, g: (g, 0, 0)),
                  pl.BlockSpec((None, gff, d), lambda i, g: (g, 0, 0)),
                  row(LANES), row(d),
                  pl.BlockSpec((1, rmod, d), lambda i, g: mod_index(i))],
        out_specs=row(d),
        out_shape=jax.ShapeDtypeStruct((t, d), F32),
        scratch_shapes=[pltpu.VMEM((tm, d), F32)],
        compiler_params=_params("arbitrary", "arbitrary"),
        name="moe",
    )(h2, wts["w_gate"], wts["w_up"], wts["w_down"], comb, x1, gate2)


def _page_fetch(pt_ref, cache_ref, buf, sem, npages):
    b = pl.program_id(0)
    nb = pl.num_programs(0)
    slot = lax.rem(b, PAGE_SLOTS)
    ahead = PAGE_SLOTS - 1

    def start_all(bb, unrolled):
        sl = lax.rem(bb, PAGE_SLOTS)

        def start(p):
            pltpu.make_async_copy(cache_ref.at[pt_ref[bb, p]], buf.at[sl, p], sem.at[sl]).start()

        if unrolled:
            for p in range(npages):
                start(p)
        else:
            lax.fori_loop(0, npages, lambda p, carry: (start(p), carry)[1], 0)

    for first in range(ahead):
        @pl.when((b == 0) & (first < nb))
        def _():
            start_all(first, False)

    @pl.when(b + ahead < nb)
    def _():
        start_all(b + ahead, True)

    for p in range(npages):
        pltpu.make_async_copy(cache_ref.at[0], buf.at[slot, p], sem.at[slot]).wait()
    return slot


def _cmp_sample_body(pt_ref, cache_ref, qbd_ref, pw_all_ref, gkc_ref, bd_ref, pair_ref, rsum_ref, rexp_ref,
                     ocmp_ref, mask_ref, buf, sem, kc_scr, *, npages, past_len, tdec):
    slot = _page_fetch(pt_ref, cache_ref, buf, sem, npages)
    nblk = past_len // SEL_BLOCK
    ppi = SUBLANES // (PAGE_SIZE // CMP_BLOCK)
    kw = KV_WIDTH

    tiles = min(POOL_UNROLL, npages // ppi)

    def pool(i, carry):
        for u in range(tiles):
            kc8 = jnp.zeros((SUBLANES, kw), F32)
            vc8 = jnp.zeros((SUBLANES, kw), F32)
            for j in range(ppi):
                r1 = _dot_nt(pw_all_ref[j], buf[slot, (i * tiles + u) * ppi + j].astype(BF16))
                s8 = SUBLANES
                kc8 = kc8 + r1[0:s8, :kw] + r1[s8:2 * s8, :kw]
                vc8 = vc8 + r1[2 * s8:3 * s8, kw:] + r1[3 * s8:, kw:]
            rows = pl.ds(pl.multiple_of((i * tiles + u) * SUBLANES, SUBLANES), SUBLANES)
            kc_scr[rows, 0:kw] = kc8
            kc_scr[rows, kw:] = vc8
        return carry
    lax.fori_loop(0, npages // (ppi * tiles), pool, 0)

    kcn = _head_rms(kc_scr[:, 0:kw], bd_ref[...]) * gkc_ref[...]
    vc = kc_scr[:, kw:]
    s = _dot_nt(qbd_ref[...], kcn.astype(BF16))
    e = jnp.exp2(s - s.max(axis=-1, keepdims=True))
    p = e / jnp.maximum(e.sum(axis=-1, keepdims=True), TINY)
    o = _dot(p.astype(BF16), vc.astype(BF16))
    nrow = o.shape[0]
    first_group = lax.broadcasted_iota(jnp.int32, (nrow, HEAD_DIM), 0) < nrow // N_KV
    ocmp_ref[...] = jnp.where(first_group, o[:, :HEAD_DIM], o[:, HEAD_DIM:])

    imp = _dot_exact_rhs(_dot_exact_lhs(rsum_ref[...], p), pair_ref[...])
    lane = lax.broadcasted_iota(jnp.int32, imp.shape, 1)
    tok = lax.rem(lax.broadcasted_iota(jnp.int32, imp.shape, 0), tdec)
    cur = (past_len + tok) // SEL_BLOCK
    allowed = (lane < cur) & (lane < nblk)
    forced = (lane == 0) | (lane == cur - 1)
    score = jnp.where(allowed, jnp.where(forced, BIG, imp), NEG)
    selm = jnp.where(allowed, _topk_mask_lanes(score, nblk, min(N_SEL - 1, nblk)), 0.0)
    mask_ref[...] = _dot(rexp_ref[...], selm.astype(BF16))


def _sel_sample_body(pt_ref, cache_ref, qbd_ref, mask_ref, eexp_ref, selnew_ref, winc_ref, winnew_ref, gt_ref,
                     ocmp_ref, o_ref, buf, sem, *, npages, past_len, tdec, ppc):
    slot = _page_fetch(pt_ref, cache_ref, buf, sem, npages)
    qbd = qbd_ref[...]
    nrow = qbd.shape[0]
    tok = lax.rem(lax.broadcasted_iota(jnp.int32, (nrow, 1), 0), tdec)
    kch = ppc * PAGE_SIZE
    kw = KV_WIDTH

    def attend(pieces):
        ss = [jnp.where(mask, s, NEG) for s, mask, _ in pieces]
        m = functools.reduce(jnp.maximum, [s.max(axis=-1, keepdims=True) for s in ss])
        es = [jnp.where(mask, jnp.exp2(s - m), 0.0) for s, (_, mask, _) in zip(ss, pieces)]
        l = sum(e.sum(axis=-1, keepdims=True) for e in es)
        acc = sum(pv(e.astype(BF16)) for e, (_, _, pv) in zip(es, pieces))
        o = acc / jnp.maximum(l, TINY)
        first_group = lax.broadcasted_iota(jnp.int32, (nrow, HEAD_DIM), 0) < nrow // N_KV
        return jnp.where(first_group, o[:, :HEAD_DIM], o[:, HEAD_DIM:])

    def feature_major(kt, vt, mask):
        vtb = vt.astype(BF16)
        return _dot(qbd, kt.astype(BF16)), mask, lambda e: _dot_nt(e, vtb)

    def new_rows(new_ref):
        kv = new_ref[...]
        idx = lax.broadcasted_iota(jnp.int32, (nrow, kv.shape[0]), 1)
        v = kv[:, kw:].astype(BF16)
        return _dot_nt(qbd, kv[:, :kw].astype(BF16)), (idx <= tok) & (idx < tdec), lambda e: _dot(e, v)

    selm = mask_ref[...].astype(BF16)
    pieces = []
    for c in range(npages // ppc):
        pages = [buf[slot, c * ppc + j] for j in range(ppc)]
        kt = jnp.concatenate([pg[:kw] for pg in pages], axis=1)
        vt = jnp.concatenate([pg[kw:] for pg in pages], axis=1)
        pieces.append(feature_major(kt, vt, _dot(selm, eexp_ref[:, c * kch:(c + 1) * kch]) > 0.5))
    o_sel = attend(pieces + [new_rows(selnew_ref)])

    wbuf = winc_ref.shape[1]
    rel = wbuf + tok - lax.broadcasted_iota(jnp.int32, (nrow, wbuf), 1)
    o_win = attend([feature_major(winc_ref[0:kw, :], winc_ref[kw:, :], (rel >= 0) & (rel < WINDOW)),
                    new_rows(winnew_ref)])

    gates = _sigmoid(gt_ref[...])
    o_ref[...] = gates[:, 0:1] * ocmp_ref[...] + gates[:, 1:2] * o_sel + gates[:, 2:3] * o_win


def _attn_sample(qbd, gt, page_table, cache_cmp_t, cache_sel_t, cache_win_t, sel_new, win_new, wts, past_len, tdec):
    nb, nrow, _ = qbd.shape
    npages = page_table.shape[1]
    ncb = past_len // CMP_BLOCK
    per_b = lambda r, w: pl.BlockSpec((None, r, w), lambda b, pt: (b, 0, 0))
    full = lambda a: pl.BlockSpec(a.shape, lambda b, pt: (0,) * a.ndim)
    hbm = pl.BlockSpec(memory_space=pl.ANY)
    page_buf = pltpu.VMEM((PAGE_SLOTS, npages, 2 * KV_WIDTH, PAGE_SIZE), F32)
    page_sem = pltpu.SemaphoreType.DMA((PAGE_SLOTS,))

    r = np.arange(nrow)
    grp = (r // tdec) // Q_PER_KV * tdec + r % tdec
    rsum = (np.arange(N_KV * tdec)[:, None] == grp[None, :]).astype(np.float32)
    pair = np.zeros((ncb, LANES), np.float32)
    pair[np.arange(ncb), np.arange(ncb) // (SEL_BLOCK // CMP_BLOCK)] = 1.0
    eexp = (np.arange(LANES)[:, None] == np.arange(past_len)[None, :] // SEL_BLOCK).astype(np.float32)
    consts1 = [wts["pw_all"], wts["gkc"], wts["bd_kv"], jnp.asarray(pair, BF16),
               jnp.asarray(rsum, BF16), jnp.asarray(rsum.T, BF16)]

    o_cmp, mask = pl.pallas_call(
        functools.partial(_cmp_sample_body, npages=npages, past_len=past_len, tdec=tdec),
        grid_spec=pltpu.PrefetchScalarGridSpec(
            num_scalar_prefetch=1, grid=(nb,),
            in_specs=[hbm, per_b(nrow, KV_WIDTH)] + [full(a) for a in consts1],
            out_specs=[per_b(nrow, HEAD_DIM), per_b(nrow, LANES)],
            scratch_shapes=[page_buf, page_sem, pltpu.VMEM((ncb, 2 * KV_WIDTH), F32)]),
        out_shape=[jax.ShapeDtypeStruct((nb, nrow, HEAD_DIM), F32), jax.ShapeDtypeStruct((nb, nrow, LANES), F32)],
        compiler_params=_params("arbitrary"),
        name="attn_sample_cmp",
    )(page_table, cache_cmp_t, qbd, *consts1)

    ppc = min(8, npages)
    eexp = jnp.asarray(eexp, BF16)
    return pl.pallas_call(
        functools.partial(_sel_sample_body, npages=npages, past_len=past_len, tdec=tdec, ppc=ppc),
        grid_spec=pltpu.PrefetchScalarGridSpec(
            num_scalar_prefetch=1, grid=(nb,),
            in_specs=[hbm, per_b(nrow, KV_WIDTH), per_b(nrow, LANES), full(eexp),
                      per_b(sel_new.shape[1], 2 * KV_WIDTH), per_b(2 * KV_WIDTH, cache_win_t.shape[2]),
                      per_b(win_new.shape[1], 2 * KV_WIDTH), per_b(nrow, SUBLANES), per_b(nrow, HEAD_DIM)],
            out_specs=per_b(nrow, HEAD_DIM),
            scratch_shapes=[page_buf, page_sem]),
        out_shape=jax.ShapeDtypeStruct((nb, nrow, HEAD_DIM), F32),
        compiler_params=_params("arbitrary"),
        name="attn_sample_sel",
    )(page_table, cache_sel_t, qbd, mask, eexp, sel_new, cache_win_t, win_new, gt, o_cmp)


def _pool_weights(w_pos_k, w_pos_v):
    bpp = PAGE_SIZE // CMP_BLOCK
    ppi = SUBLANES // bpp
    rows = np.arange(PAGE_SIZE)
    place = np.zeros((ppi, SUBLANES, PAGE_SIZE), np.float32)
    for j in range(ppi):
        place[j, j * bpp + rows // CMP_BLOCK, rows] = 1.0
    place = jnp.asarray(place)

    def hi_lo(w):
        full = place * jnp.tile(w, bpp)[None, None, :]
        hi = full.astype(BF16)
        return hi, (full - hi.astype(F32)).astype(BF16)

    k_hi, k_lo = hi_lo(w_pos_k)
    v_hi, v_lo = hi_lo(w_pos_v)
    return jnp.concatenate([k_hi, k_lo, v_hi, v_lo], axis=1)


def _prep_weights(l, g_norm1, g_norm2, w_in, g_q, g_k_cmp, g_k_sel, g_k_win, w_pos_k, w_pos_v, conv_w, conv_b,
                  dt_bias, a_log, d_skip, g_att_out, g_ssm_out, w_out, w_rg, b_rg, w_re, b_re, w_gate, w_up, w_down):
    w = w_in[l]
    o_gt = ATT_WIDTH + 6 * KV_WIDTH
    o_z = o_gt + 3 * N_HEADS
    o_xbc = o_z + SSM_WIDTH
    o_dt = o_xbc + CONV_DIM
    pad = jnp.zeros((D_MODEL, PK_WIDTH - PK_GD - SSM_HEADS - 3 * N_HEADS), F32)
    w_packed = jnp.concatenate([w[:, :o_gt], w[:, o_z:o_xbc], w[:, o_xbc:o_dt], w[:, o_dt:], w[:, o_gt:o_z], pad],
                               axis=1).astype(BF16)
    seg = np.arange(KV_WIDTH) // HEAD_DIM
    bd_kv = jnp.asarray((seg[:, None] == seg[None, :]).astype(np.float32), BF16)
    lane_pad = lambda v: jnp.pad(v.astype(F32), (0, LANES - v.shape[0])).reshape(1, LANES)
    wr = jnp.concatenate([w_rg[l], jnp.transpose(w_re[l], (1, 0, 2)).reshape(D_MODEL, N_EXPERTS)], axis=1)
    wr = jnp.pad(wr, ((0, 0), (0, LANES - wr.shape[1])))
    wr_hi = wr.astype(BF16)
    wr_lo = (wr - wr_hi.astype(F32)).astype(BF16)
    return {
        "g1": g_norm1[l].reshape(1, D_MODEL), "w_in": w_packed,
        "gq_col": (jnp.tile(g_q[l], N_HEADS) * (SCALE * LOG2E)).reshape(ATT_WIDTH, 1),
        "gks": jnp.tile(g_k_sel[l], N_KV).reshape(1, KV_WIDTH),
        "gkw": jnp.tile(g_k_win[l], N_KV).reshape(1, KV_WIDTH),
        "gkc": jnp.tile(g_k_cmp[l], N_KV).reshape(1, KV_WIDTH),
        "bd_kv": bd_kv,
        "wkv": jnp.concatenate([jnp.broadcast_to(w_pos_k[l][:, None], (CMP_BLOCK, KV_WIDTH)),
                                jnp.broadcast_to(w_pos_v[l][:, None], (CMP_BLOCK, KV_WIDTH))], axis=1),
        "pw_all": _pool_weights(w_pos_k[l], w_pos_v[l]),
        "conv_w": conv_w[l], "conv_b": conv_b[l].reshape(1, CONV_DIM),
        "dt_bias": lane_pad(dt_bias[l]), "a_log": lane_pad(a_log[l]), "d_skip": jnp.repeat(d_skip[l].astype(F32), SSM_HEAD_DIM).reshape(1, SSM_WIDTH),
        "g_ssm": g_ssm_out[l].reshape(1, SSM_WIDTH),
        "g_att": g_att_out[l].reshape(1, ATT_WIDTH), "w_out": w_out[l].astype(BF16),
        "g2": g_norm2[l].reshape(1, D_MODEL), "wr_hi": wr_hi, "wr_lo": wr_lo,
        "br": lane_pad(jnp.concatenate([b_rg[l], b_re[l].reshape(-1)])),
        "w_gate": w_gate[l].astype(BF16), "w_up": w_up[l].astype(BF16),
        "w_down": w_down[l].astype(BF16).reshape(N_EGROUPS, E_PER_GROUP * EXPERT_FF, D_MODEL),
    }


def _finish(att2d, y2d, x2d, mods, mod_index, tm_post, tm_moe, moe_index, wts):
    x1, h2, comb = _post(att2d, y2d, x2d, mods[2], mods[3], mods[4], mod_index, tm_post, wts)
    return _moe(h2, comb, x1, mods[5], moe_index, tm_moe, wts)


def _token_major_cache(t):
    b, _, n = t.shape
    return jnp.transpose(t.reshape(b, 2, N_KV, HEAD_DIM, n), (0, 4, 1, 2, 3))[None]


def _feature_major_cache(c):
    n, rows = c.shape[:2]
    return jnp.transpose(c, (0, 2, 3, 4, 1)).reshape(n, 2 * KV_WIDTH, rows)


def kernel(x_prompt, x_sample, cache_cmp, cache_sel, cache_win, state_ssm, state_conv, page_table, c_prompt, c_sample,
           g_norm1, g_norm2, w_ada, b_ada, w_in, g_q, g_k_cmp, g_k_sel, g_k_win, w_pos_k, w_pos_v, conv_w, conv_b,
           dt_bias, a_log, d_skip, g_att_out, g_ssm_out, w_out, w_rg, b_rg, w_re, b_re, w_gate, w_up, w_down):
    depth = w_in.shape[0]
    assert depth == 1
    l = 0
    bp, seq, d = x_prompt.shape
    bs, tdec, _ = x_sample.shape
    npages = page_table.shape[1]
    past_len = npages * PAGE_SIZE
    tp, ts = bp * seq, bs * tdec

    wts = _prep_weights(l, g_norm1, g_norm2, w_in, g_q, g_k_cmp, g_k_sel, g_k_win, w_pos_k, w_pos_v, conv_w, conv_b,
                        dt_bias, a_log, d_skip, g_att_out, g_ssm_out, w_out, w_rg, b_rg, w_re, b_re, w_gate, w_up,
                        w_down)
    m_all = _ada(jnp.concatenate([c_prompt, c_sample], axis=0), w_ada[l], b_ada[l])
    mods_p = [m.reshape(bp, 1, d) for m in jnp.split(m_all[:bp], 6, axis=-1)]
    tm_s = min(512, ts)
    mods_s = [jnp.repeat(m, tdec, axis=0).reshape(ts // tm_s, tm_s, d) for m in jnp.split(m_all[bp:], 6, axis=-1)]

    tm = 512
    per_seq = seq // tm
    xp2 = x_prompt.reshape(tp, d)
    qt, cmpt_p, selt_p, selk, selvt, wint_p, wink, winvt, z, xbc, gd, gdt, kc = _in_proj(
        xp2, bp, mods_p[0], mods_p[1], lambda i: (i // per_seq, 0, 0), tm, wts, True)
    ncb = seq // CMP_BLOCK
    nch = seq // KEY_CHUNK
    r3 = lambda a, n: a.reshape(bp, n, a.shape[-1])
    kc_eo = kc.reshape(bp, ncb // 2, 2, 2 * KV_WIDTH).transpose(0, 2, 1, 3).reshape(bp, ncb, 2 * KV_WIDTH)
    kcmp = kc_eo[:, :, :KV_WIDTH].astype(BF16)
    vcmp_t = jnp.transpose(kc_eo[:, :, KV_WIDTH:], (0, 2, 1)).astype(BF16)
    chunked = lambda a: a.reshape(bp, nch, KV_WIDTH, KEY_CHUNK)
    att = _attn_prompt(qt, r3(selk, seq), chunked(selvt), r3(wink, seq), chunked(winvt), kcmp, vcmp_t, gdt)
    y_ssm, ssm_p, conv_p = _ssd(r3(xbc, seq), r3(z, seq), r3(gd, seq), jnp.zeros((bp, SUBLANES, CONV_DIM), F32),
                                jnp.zeros((bp, SSM_HEADS, SSM_HEAD_DIM, SSM_STATE), F32), wts)
    tm_moe = 1024
    y_prompt = _finish(att.reshape(tp, ATT_WIDTH), y_ssm.reshape(tp, SSM_WIDTH), xp2, mods_p,
                       lambda i: (i // per_seq, 0, 0), tm, tm_moe, lambda i: (i // (seq // tm_moe), 0, 0), wts)

    xs2 = x_sample.reshape(ts, d)
    qt_s, cmpt_s, selt_s, _, _, wint_s, _, _, z, xbc, gd, _ = _in_proj(
        xs2, 1, mods_s[0], mods_s[1], lambda i: (i, 0, 0), tm_s, wts, False)
    nrow = N_HEADS * tdec
    qh = qt_s[0].T.reshape(bs, tdec, N_HEADS, HEAD_DIM).transpose(0, 2, 1, 3).reshape(bs, nrow, HEAD_DIM)
    g0 = (jnp.arange(nrow) < nrow // N_KV)[None, :, None]
    zero = jnp.zeros_like(qh)
    qbd = jnp.concatenate([jnp.where(g0, qh, zero), jnp.where(g0, zero, qh)], axis=-1)
    gt = gd[:, GD_GATE0:GD_GATE0 + 3 * N_HEADS].reshape(bs, tdec, 3, N_HEADS).transpose(0, 3, 1, 2)
    gt = jnp.pad(gt.reshape(bs, nrow, 3), ((0, 0), (0, 0), (0, SUBLANES - 3)))
    rows_s = lambda t: t[0].T.reshape(bs, tdec, 2 * KV_WIDTH)
    sel_s, win_s = rows_s(selt_s), rows_s(wint_s)
    new_pad = 2 * SUBLANES - tdec
    padded = lambda a: jnp.pad(a, ((0, 0), (0, new_pad), (0, 0)))
    cwin_t = _feature_major_cache(cache_win[l])
    att_s = _attn_sample(qbd, gt, page_table, _feature_major_cache(cache_cmp[l]), _feature_major_cache(cache_sel[l]),
                         cwin_t, padded(sel_s), padded(win_s), wts, past_len, tdec)
    att_s = att_s.reshape(bs, N_HEADS, tdec, HEAD_DIM).transpose(0, 2, 1, 3).reshape(ts, ATT_WIDTH)

    batch_minor = lambda a: jnp.transpose(a.reshape(bs, tdec, a.shape[-1]), (1, 2, 0))
    xbc3 = xbc.reshape(bs, tdec, CONV_DIM)
    y_t, ssm_t = _ssd_decode(batch_minor(xbc), jnp.transpose(state_conv[l], (1, 2, 0)), batch_minor(z),
                             batch_minor(gd[:, :SSM_HEADS]), jnp.transpose(state_ssm[l].astype(F32), (1, 2, 3, 0)),
                             wts)
    y_s = jnp.transpose(y_t, (2, 0, 1)).reshape(ts, SSM_WIDTH)
    ssm_s = jnp.transpose(ssm_t, (3, 0, 1, 2))
    conv_s = jnp.concatenate([state_conv[l], xbc3], axis=1)[:, tdec:]
    y_sample = _finish(att_s, y_s, xs2, mods_s, lambda i: (i, 0, 0), tm_s, tm_s, lambda i: (i, 0, 0), wts)

    kv6 = lambda a: a.reshape(1, bs, tdec, 2, N_KV, HEAD_DIM)
    wkeep = min(WINDOW, seq)
    wbuf = cwin_t.shape[2]
    wkeep_s = min(WINDOW, wbuf + tdec)
    win_all_t = jnp.concatenate([cwin_t, jnp.transpose(win_s, (0, 2, 1))], axis=2)[:, :, wbuf + tdec - wkeep_s:]
    tail = lambda a: a[:, SUBLANES - (CONV_W - 1):][None]
    return (y_prompt.reshape(bp, seq, d), y_sample.reshape(bs, tdec, d),
            _token_major_cache(cmpt_p), kv6(rows_s(cmpt_s)), _token_major_cache(selt_p), kv6(sel_s),
            _token_major_cache(wint_p[:, :, seq - wkeep:]), _token_major_cache(win_all_t),
            ssm_p[None], ssm_s[None].astype(state_ssm.dtype), tail(conv_p), conv_s[None])
```
